```python
import math
import jax, jax.numpy as jnp
from jax import lax
import numpy as np

D_MODEL = 2048
BATCH = 4
SEQ = 4096
DEPTH = 1

N_META = 16
EPS = 1e-6
ROPE_THETA = 500000.0
NEG = -1e30

DA_WIDTH = D_MODEL // 2
DA_QK_DIM = 64
DA_V_DIM = 2 * DA_QK_DIM
DA_HEADS = DA_WIDTH // DA_V_DIM
DA_ROT_DIM = DA_QK_DIM // 4
Q_BLOCK = 128

ML_WIDTH = D_MODEL - DA_WIDTH
ML_HEADS = 4
ML_V_DIM = ML_WIDTH // ML_HEADS
ML_QK_DIM = ML_V_DIM // 2
ML_CHUNK = 64
ML_PAD = ML_CHUNK - N_META
CONV_W = 5
N_GATES = 4

MIX_WIDTH = DA_WIDTH + ML_WIDTH
D_FF = ((8 * D_MODEL // 3 + 255) // 256) * 256

IN_SIZES = [
    DA_HEADS * 2 * DA_QK_DIM,
    DA_HEADS * 2 * DA_QK_DIM,
    DA_HEADS * DA_V_DIM,
    ML_HEADS * ML_QK_DIM,
    ML_HEADS * ML_QK_DIM,
    ML_HEADS * ML_V_DIM,
    ML_WIDTH,
    N_GATES * ML_HEADS,
]
IN_COLS = int(sum(IN_SIZES))
IN_SPLITS = [int(s) for s in np.cumsum(IN_SIZES)[:-1]]

kernel_name = "hymba_diffattn_mlstm_encoder"


def rms_norm(x, g):
    xf = x.astype(jnp.float32)
    y = xf * lax.rsqrt(jnp.mean(xf * xf, axis=-1, keepdims=True) + EPS)
    return (y * g.astype(jnp.float32)).astype(x.dtype)


def head_rms(o, g):
    of = o.astype(jnp.float32)
    y = of * lax.rsqrt(jnp.mean(of * of, axis=-1, keepdims=True) + EPS)
    return (y * g.astype(jnp.float32)).astype(o.dtype)


def rope_tables(length):
    pos = jnp.arange(length, dtype=jnp.float32)
    inv = ROPE_THETA ** (-jnp.arange(0, DA_ROT_DIM, 2, dtype=jnp.float32) / DA_ROT_DIM)
    ang = pos[:, None] * inv[None, :]
    ang = jnp.concatenate([ang, ang], axis=-1)
    return jnp.cos(ang), jnp.sin(ang)


def apply_partial_rope(t, cos, sin):
    c = cos[None, :, None, None, :].astype(t.dtype)
    s = sin[None, :, None, None, :].astype(t.dtype)
    tr, tp = t[..., :DA_ROT_DIM], t[..., DA_ROT_DIM:]
    x1, x2 = tr[..., :DA_ROT_DIM // 2], tr[..., DA_ROT_DIM // 2:]
    rot = jnp.concatenate([-x2, x1], axis=-1)
    return jnp.concatenate([tr * c + rot * s, tp], axis=-1)


def diff_attention(q, k, v, lam, lam_init, head_gain, cos, sin):
    bsz, length = q.shape[0], q.shape[1]
    q = apply_partial_rope(q, cos, sin) * (DA_QK_DIM ** -0.5)
    k = apply_partial_rope(k, cos, sin)
    q = jnp.transpose(q, (0, 2, 3, 1, 4))
    k = jnp.transpose(k, (0, 2, 3, 1, 4))
    v = jnp.transpose(v, (0, 2, 1, 3))

    def attend(qb):
        s = jnp.einsum('bhcqd,bhckd->bhcqk', qb, k).astype(jnp.float32)
        p = jax.nn.softmax(s, axis=-1)
        a = p[:, :, 0] - lam * p[:, :, 1]
        return jnp.einsum('bhqk,bhkd->bhqd', a.astype(v.dtype), v)

    out_meta = attend(q[:, :, :, :N_META])
    qr = q[:, :, :, N_META:]
    s_len = qr.shape[3]
    nb = s_len // Q_BLOCK
    qr = qr.reshape(bsz, DA_HEADS, 2, nb, Q_BLOCK, DA_QK_DIM)
    qr = jnp.moveaxis(qr, 3, 0)
    out_real = lax.map(attend, qr)
    out_real = jnp.moveaxis(out_real, 0, 2).reshape(bsz, DA_HEADS, s_len, DA_V_DIM)
    o = jnp.concatenate([out_meta, out_real], axis=2)
    o = jnp.transpose(o, (0, 2, 1, 3))
    o = head_rms(o, head_gain) * (1.0 - lam_init)
    return o.reshape(bsz, length, DA_WIDTH)


def mlstm_chunkwise(q, k, v, log_i, log_f):
    bsz, nh, t_len, dk = q.shape
    dv = v.shape[-1]
    nc = t_len // ML_CHUNK

    def to_chunks(a):
        a = a.reshape(a.shape[:2] + (nc, ML_CHUNK) + a.shape[3:])
        return jnp.moveaxis(a, 2, 0)

    xs = (to_chunks(q), to_chunks(k), to_chunks(v), to_chunks(log_i), to_chunks(log_f))
    mask = jnp.tril(jnp.ones((ML_CHUNK, ML_CHUNK), dtype=bool))

    def step(carry, inp):
        c_mat, n_vec, m = carry
        qc, kc, vc, li, lf = inp
        b = jnp.cumsum(lf, axis=-1)
        d = b[..., :, None] - b[..., None, :] + li[..., None, :]
        d = jnp.where(mask, d, NEG)
        m_inter = b + m[..., None]
        m_t = jnp.maximum(m_inter, jnp.max(d, axis=-1))
        w_inter = jnp.exp(m_inter - m_t)
        s = jnp.einsum('bhtd,bhsd->bhts', qc, kc) * jnp.exp(d - m_t[..., None])
        num = (w_inter[..., None] * jnp.einsum('bhtd,bhde->bhte', qc, c_mat)
               + jnp.einsum('bhts,bhse->bhte', s, vc))
        den = w_inter * jnp.einsum('bhtd,bhd->bht', qc, n_vec) + jnp.sum(s, axis=-1)
        h = num / jnp.maximum(jnp.abs(den), jnp.exp(-m_t))[..., None]
        b_end = b[..., -1]
        g = b_end[..., None] - b + li
        m_new = jnp.maximum(b_end + m, jnp.max(g, axis=-1))
        decay = jnp.exp(b_end + m - m_new)
        wk = jnp.exp(g - m_new[..., None])
        c_new = decay[..., None, None] * c_mat + jnp.einsum('bhs,bhsd,bhse->bhde', wk, kc, vc)
        n_new = decay[..., None] * n_vec + jnp.einsum('bhs,bhsd->bhd', wk, kc)
        return (c_new, n_new, m_new), h

    init = (jnp.zeros((bsz, nh, dk, dv), jnp.float32),
            jnp.zeros((bsz, nh, dk), jnp.float32),
            jnp.zeros((bsz, nh), jnp.float32))
    _, hs = lax.scan(step, init, xs)
    return jnp.moveaxis(hs, 0, 2).reshape(bsz, nh, t_len, dv)


def depthwise_conv_centred(x, w, b):
    ch = x.shape[-1]
    y = lax.conv_general_dilated(
        x, w.astype(x.dtype)[:, None, :], window_strides=(1,),
        padding=[(CONV_W // 2, CONV_W // 2)],
        dimension_numbers=('NWC', 'WIO', 'NWC'), feature_group_count=ch)
    return y + b.astype(x.dtype)


def mlstm_mixer(q, k, v, o_pre, gates, conv_w, conv_b, gate_bias, head_gain):
    bsz, length = q.shape[0], q.shape[1]
    out_dtype = v.dtype
    qk = jax.nn.silu(depthwise_conv_centred(jnp.concatenate([q, k], axis=-1), conv_w, conv_b))
    q, k = qk[..., :ML_HEADS * ML_QK_DIM], qk[..., ML_HEADS * ML_QK_DIM:]

    def heads(a, dh):
        a = a.astype(jnp.float32).reshape(bsz, length, ML_HEADS, dh)
        a = jnp.transpose(a, (0, 2, 1, 3))
        return jnp.pad(a, ((0, 0), (0, 0), (ML_PAD, 0), (0, 0)))

    qh = heads(q, ML_QK_DIM) * (ML_QK_DIM ** -0.5)
    kh = heads(k, ML_QK_DIM)
    vh = heads(v, ML_V_DIM)
    g = gates.astype(jnp.float32).reshape(bsz, length, N_GATES, ML_HEADS) + gate_bias.astype(jnp.float32)
    g = jnp.transpose(g, (2, 0, 3, 1))
    pad_t = ((0, 0), (0, 0), (ML_PAD, 0))
    li_f = jnp.pad(g[0], pad_t, constant_values=NEG)
    lf_f = jnp.pad(jax.nn.log_sigmoid(g[1]), pad_t)
    li_b = jnp.pad(g[2], pad_t, constant_values=NEG)
    lf_b = jnp.pad(jax.nn.log_sigmoid(g[3]), pad_t)

    flip = lambda a: jnp.flip(a, axis=2)
    h_fwd = mlstm_chunkwise(qh, kh, vh, li_f, lf_f)
    h_bwd = flip(mlstm_chunkwise(flip(qh), flip(kh), flip(vh), flip(li_b), flip(lf_b)))
    h = (h_fwd + h_bwd)[:, :, ML_PAD:]
    h = jnp.transpose(h, (0, 2, 1, 3))
    h = head_rms(h, head_gain).reshape(bsz, length, ML_WIDTH)
    return (jax.nn.sigmoid(o_pre.astype(jnp.float32)) * h).astype(out_dtype)


def setup_inputs(seed: int = 0) -> dict:
    key = jax.random.key(seed)
    ks = jax.random.split(key, 20)
    nrm = jax.random.normal
    f32 = jnp.float32
    gate_base = jnp.stack([jnp.zeros((ML_HEADS,), f32), jnp.linspace(3.0, 6.0, ML_HEADS, dtype=f32),
                           jnp.zeros((ML_HEADS,), f32), jnp.linspace(3.0, 6.0, ML_HEADS, dtype=f32)])
    return {
        "x": nrm(ks[0], (BATCH, SEQ, D_MODEL), f32),
        "meta_tokens": nrm(ks[1], (N_META, D_MODEL), f32),
        "norm_mix": 1.0 + 0.02 * nrm(ks[2], (DEPTH, D_MODEL), f32),
        "w_in": nrm(ks[3], (DEPTH, D_MODEL, IN_COLS), f32) * D_MODEL ** -0.5,
        "da_lambda_q1": 0.1 * nrm(ks[4], (DEPTH, DA_QK_DIM), f32),
        "da_lambda_k1": 0.1 * nrm(ks[5], (DEPTH, DA_QK_DIM), f32),
        "da_lambda_q2": 0.1 * nrm(ks[6], (DEPTH, DA_QK_DIM), f32),
        "da_lambda_k2": 0.1 * nrm(ks[7], (DEPTH, DA_QK_DIM), f32),
        "da_head_norm": 1.0 + 0.02 * nrm(ks[8], (DEPTH, DA_HEADS, DA_V_DIM), f32),
        "ml_conv_w": nrm(ks[9], (DEPTH, CONV_W, 2 * ML_HEADS * ML_QK_DIM), f32) * CONV_W ** -0.5,
        "ml_conv_b": 0.01 * nrm(ks[10], (DEPTH, 2 * ML_HEADS * ML_QK_DIM), f32),
        "ml_gate_bias": gate_base[None] + 0.1 * nrm(ks[11], (DEPTH, N_GATES, ML_HEADS), f32),
        "ml_head_norm": 1.0 + 0.02 * nrm(ks[12], (DEPTH, ML_HEADS, ML_V_DIM), f32),
        "w_out": nrm(ks[13], (DEPTH, MIX_WIDTH, D_MODEL), f32) * MIX_WIDTH ** -0.5,
        "norm_ffn": 1.0 + 0.02 * nrm(ks[14], (DEPTH, D_MODEL), f32),
        "w_gate": nrm(ks[15], (DEPTH, D_MODEL, D_FF), f32) * D_MODEL ** -0.5,
        "w_up": nrm(ks[16], (DEPTH, D_MODEL, D_FF), f32) * D_MODEL ** -0.5,
        "w_down": nrm(ks[17], (DEPTH, D_FF, D_MODEL), f32) * D_FF ** -0.5,
        "norm_final": 1.0 + 0.02 * nrm(ks[18], (D_MODEL,), f32),
    }


def reference(x, meta_tokens, norm_mix, w_in, da_lambda_q1, da_lambda_k1, da_lambda_q2,
              da_lambda_k2, da_head_norm, ml_conv_w, ml_conv_b, ml_gate_bias, ml_head_norm,
              w_out, norm_ffn, w_gate, w_up, w_down, norm_final):
    bsz = x.shape[0]
    meta = jnp.broadcast_to(meta_tokens.astype(x.dtype)[None], (bsz, N_META, D_MODEL))
    h = jnp.concatenate([meta, x], axis=1)
    length = h.shape[1]
    cos, sin = rope_tables(length)

    for l in range(DEPTH):
        u = rms_norm(h, norm_mix[l])
        proj = jnp.einsum('bld,dc->blc', u, w_in[l])
        aq, ak, av, mq, mk, mv, mo, mg = jnp.split(proj, IN_SPLITS, axis=-1)
        lam_init = 0.8 - 0.6 * math.exp(-0.3 * l)
        lam = (jnp.exp(jnp.sum(da_lambda_q1[l].astype(jnp.float32) * da_lambda_k1[l].astype(jnp.float32)))
               - jnp.exp(jnp.sum(da_lambda_q2[l].astype(jnp.float32) * da_lambda_k2[l].astype(jnp.float32)))
               + lam_init)
        attn_out = diff_attention(
            aq.reshape(bsz, length, DA_HEADS, 2, DA_QK_DIM),
            ak.reshape(bsz, length, DA_HEADS, 2, DA_QK_DIM),
            av.reshape(bsz, length, DA_HEADS, DA_V_DIM),
            lam, lam_init, da_head_norm[l], cos, sin)
        ml_out = mlstm_mixer(mq, mk, mv, mo, mg, ml_conv_w[l], ml_conv_b[l],
                             ml_gate_bias[l], ml_head_norm[l])
        mixed = jnp.concatenate([attn_out, ml_out.astype(attn_out.dtype)], axis=-1)
        h = h + jnp.einsum('blc,cd->bld', mixed, w_out[l])
        u = rms_norm(h, norm_ffn[l])
        ff = jax.nn.silu(jnp.einsum('bld,df->blf', u, w_gate[l])) * jnp.einsum('bld,df->blf', u, w_up[l])
        h = h + jnp.einsum('blf,fd->bld', ff, w_down[l])

    h = rms_norm(h, norm_final)
    return h[:, N_META:]
```

```python
import functools

import jax
import jax.numpy as jnp
from jax import lax
from jax.experimental import pallas as pl
from jax.experimental.pallas import tpu as pltpu

F32 = jnp.float32
BF16 = jnp.bfloat16

D_MODEL = 2048
N_META = 16
EPS = 1e-6
ROPE_THETA = 500000.0
NEG = -1e30

DA_QK_DIM = 64
DA_V_DIM = 128
DA_HEADS = 8
DA_ROT_DIM = 16
DA_WIDTH = DA_HEADS * DA_V_DIM
LAM_INIT = 0.8 - 0.6 * 1.0

ML_HEADS = 4
ML_V_DIM = 256
ML_QK_DIM = 128
ML_WIDTH = ML_HEADS * ML_V_DIM
CONV_W = 5
N_GATES = 4
N_GATE_COLS = N_GATES * ML_HEADS

D_FF = 5632
MAIN_COLS = 6144

COL_AQ, COL_AK, COL_AV, COL_MQK, COL_MV, COL_MO = 0, 1, 2, 3, 4, 5

LANES = 128
SUBLANES = 8
VMEM_LIMIT = 56 * 1024 * 1024

IN_TN = 1024
IN_TM = 512
ATT_TQ = 512
ATT_KB = 512
ML_CHUNK = 256
OUT_TM = 256
FFN_TM = 512
FFN_TF = 512
CONV_ROWS = 512


def _cparams(sem):
    return pltpu.CompilerParams(dimension_semantics=sem, vmem_limit_bytes=VMEM_LIMIT)


def _dot(a, b):
    return jnp.dot(a, b, preferred_element_type=F32)


def _dot_nt(a, b):
    return lax.dot_general(a, b, (((1,), (1,)), ((), ())), preferred_element_type=F32)


def _dot_tn(a, b):
    return lax.dot_general(a, b, (((0,), (0,)), ((), ())), preferred_element_type=F32)


def _dot_exact(a, b):
    return jnp.dot(a, b, preferred_element_type=F32, precision=lax.Precision.HIGHEST)


def _sigmoid(x):
    return 1.0 / (1.0 + jnp.exp(-x))


def _log_sigmoid(x):
    return jnp.minimum(x, 0.0) - jnp.log(1.0 + jnp.exp(-jnp.abs(x)))


def _inproj_kernel(x_ref, g_ref, w_ref, wg_ref, cos_ref, sa_ref, sb_ref,
                   proj_ref, gate_ref, u_ref):
    j = pl.program_id(1)

    @pl.when(j == 0)
    def _():
        x = x_ref[...]
        ms = jnp.mean(x * x, axis=-1, keepdims=True)
        u = (x * lax.rsqrt(ms + EPS) * g_ref[...]).astype(BF16)
        u_ref[...] = u
        gate_ref[...] = _dot(u, wg_ref[...])

    acc = _dot(u_ref[...], w_ref[...])

    @pl.when(j <= COL_AK)
    def _():
        scale = jnp.where(j == COL_AQ, DA_QK_DIM ** -0.5, 1.0).astype(F32)
        cos = cos_ref[...] * scale
        sa = sa_ref[...] * scale
        sb = sb_ref[...] * scale
        for h in range(DA_HEADS):
            t = acc[:, h * LANES:(h + 1) * LANES]
            r = (t * cos + pltpu.roll(t, LANES - DA_ROT_DIM // 2, 1) * sa
                 + pltpu.roll(t, DA_ROT_DIM // 2, 1) * sb)
            proj_ref[:, h * LANES:(h + 1) * LANES] = r.astype(BF16)

    @pl.when(j > COL_AK)
    def _():
        proj_ref[...] = acc.astype(BF16)


def _inproj(rows, norm_g, w_main, w_gate, cos_t, sa_t, sb_t, tm):
    m = rows.shape[0]
    tiles_per_seq = cos_t.shape[0] // tm
    tab_spec = pl.BlockSpec((tm, LANES), lambda i, j: (i % tiles_per_seq, 0))
    return pl.pallas_call(
        _inproj_kernel,
        grid=(m // tm, MAIN_COLS // IN_TN),
        in_specs=[
            pl.BlockSpec((tm, D_MODEL), lambda i, j: (i, 0)),
            pl.BlockSpec((1, D_MODEL), lambda i, j: (0, 0)),
            pl.BlockSpec((D_MODEL, IN_TN), lambda i, j: (0, j)),
            pl.BlockSpec((D_MODEL, LANES), lambda i, j: (0, 0)),
            tab_spec, tab_spec, tab_spec,
        ],
        out_specs=[
            pl.BlockSpec((tm, IN_TN), lambda i, j: (i, j)),
            pl.BlockSpec((tm, LANES), lambda i, j: (i, 0)),
        ],
        out_shape=[
            jax.ShapeDtypeStruct((m, MAIN_COLS), BF16),
            jax.ShapeDtypeStruct((m, LANES), F32),
        ],
        scratch_shapes=[pltpu.VMEM((tm, D_MODEL), BF16)],
        compiler_params=_cparams(("parallel", "arbitrary")),
        name="inproj",
    )(rows, norm_g, w_main, w_gate, cos_t, sa_t, sb_t)


_CONV_PAD = SUBLANES


def _conv_kernel(x_ref, m_ref, w_ref, b_ref, o_ref, om_ref, s_ref):
    seq = x_ref.shape[0]
    j = pl.program_id(1)
    zeros = jnp.zeros((_CONV_PAD, LANES), F32)
    s_ref[0:_CONV_PAD, :] = zeros
    s_ref[_CONV_PAD:_CONV_PAD + N_META, :] = m_ref[...].astype(F32)
    s_ref[_CONV_PAD + N_META:_CONV_PAD + N_META + seq, :] = x_ref[...].astype(F32)
    s_ref[_CONV_PAD + N_META + seq:, :] = zeros
    w = w_ref[...]
    bias = b_ref[...]
    scale = jnp.where(j < ML_HEADS, ML_QK_DIM ** -0.5, 1.0).astype(F32)

    def conv(pos, n):
        acc = jnp.zeros((n, LANES), F32) + bias
        for t in range(CONV_W):
            start = _CONV_PAD + pos + t - CONV_W // 2
            acc = acc + s_ref[start:start + n, :] * w[t:t + 1, :]
        return (acc * _sigmoid(acc) * scale).astype(BF16)

    om_ref[0] = conv(0, N_META)
    for c in range(seq // CONV_ROWS):
        o_ref[c * CONV_ROWS:(c + 1) * CONV_ROWS, :] = conv(N_META + c * CONV_ROWS, CONV_ROWS)


def _conv(proj_x, proj_m, conv_w, conv_b, bsz, seq):
    ngrp = 2 * ML_HEADS
    col0 = COL_MQK * (IN_TN // LANES)
    return pl.pallas_call(
        _conv_kernel,
        grid=(bsz, ngrp),
        in_specs=[
            pl.BlockSpec((seq, LANES), lambda b, j: (b, col0 + j)),
            pl.BlockSpec((N_META, LANES), lambda b, j: (0, col0 + j)),
            pl.BlockSpec((SUBLANES, LANES), lambda b, j: (0, j)),
            pl.BlockSpec((1, LANES), lambda b, j: (0, j)),
        ],
        out_specs=[
            pl.BlockSpec((seq, LANES), lambda b, j: (b, j)),
            pl.BlockSpec((1, N_META, LANES), lambda b, j: (b, 0, j)),
        ],
        out_shape=[
            jax.ShapeDtypeStruct((bsz * seq, ngrp * LANES), BF16),
            jax.ShapeDtypeStruct((bsz, N_META, ngrp * LANES), BF16),
        ],
        scratch_shapes=[pltpu.VMEM((seq + N_META + 2 * _CONV_PAD, LANES), F32)],
        compiler_params=_cparams(("parallel", "parallel")),
        name="mlstm_conv",
    )(proj_x, proj_m, conv_w, conv_b)


def _attn_kernel(lam_ref, q_ref, k_ref, v_ref, km_ref, vm_ref, gain_ref, o_ref, s_ref):
    tq = q_ref.shape[0]
    nkb = k_ref.shape[0] // ATT_KB
    lp = lam_ref[...]
    lam = (jnp.exp(jnp.sum(lp[0:1] * lp[1:2], axis=1, keepdims=True))
           - jnp.exp(jnp.sum(lp[2:3] * lp[3:4], axis=1, keepdims=True)) + LAM_INIT)
    q = q_ref[...]
    lane = lax.broadcasted_iota(jnp.int32, (tq, LANES), 1)
    outs = []
    for comp in range(2):
        in_comp = (lane < DA_QK_DIM) if comp == 0 else (lane >= DA_QK_DIM)
        qz = jnp.where(in_comp, q, jnp.zeros_like(q))
        mx = jnp.full((tq, LANES), -jnp.inf, F32)
        for kb in range(nkb):
            s = _dot_nt(qz, k_ref[kb * ATT_KB:(kb + 1) * ATT_KB, :])
            s_ref[:, kb * ATT_KB:(kb + 1) * ATT_KB] = s
            for u in range(ATT_KB // LANES):
                mx = jnp.maximum(mx, s[:, u * LANES:(u + 1) * LANES])
        sm = _dot_nt(qz, km_ref[...])
        m = jnp.maximum(jnp.max(mx, axis=1, keepdims=True), jnp.max(sm, axis=1, keepdims=True))
        pm = jnp.exp(sm - m)
        acc = _dot(pm.astype(BF16), vm_ref[...])
        lsum = jnp.zeros((tq, LANES), F32)
        for kb in range(nkb):
            p = jnp.exp(s_ref[:, kb * ATT_KB:(kb + 1) * ATT_KB] - m)
            for u in range(ATT_KB // LANES):
                lsum = lsum + p[:, u * LANES:(u + 1) * LANES]
            acc = acc + _dot(p.astype(BF16), v_ref[kb * ATT_KB:(kb + 1) * ATT_KB, :])
        l = jnp.sum(lsum, axis=1, keepdims=True) + jnp.sum(pm, axis=1, keepdims=True)
        outs.append(acc / l)
    o = outs[0] - lam * outs[1]
    y = o * lax.rsqrt(jnp.mean(o * o, axis=1, keepdims=True) + EPS) * gain_ref[...] * (1.0 - LAM_INIT)
    o_ref[...] = y.astype(BF16)


def _attention(lam_params, proj_x, proj_m, gain, bsz, seq):
    nq = seq // ATT_TQ
    hb = IN_TN // LANES
    return pl.pallas_call(
        _attn_kernel,
        grid=(bsz, DA_HEADS, nq),
        in_specs=[
            pl.BlockSpec((SUBLANES, LANES), lambda b, h, i: (0, 0)),
            pl.BlockSpec((ATT_TQ, LANES), lambda b, h, i: (b * nq + i, COL_AQ * hb + h)),
            pl.BlockSpec((seq, LANES), lambda b, h, i: (b, COL_AK * hb + h)),
            pl.BlockSpec((seq, LANES), lambda b, h, i: (b, COL_AV * hb + h)),
            pl.BlockSpec((N_META, LANES), lambda b, h, i: (0, COL_AK * hb + h)),
            pl.BlockSpec((N_META, LANES), lambda b, h, i: (0, COL_AV * hb + h)),
            pl.BlockSpec((1, LANES), lambda b, h, i: (0, h)),
        ],
        out_specs=pl.BlockSpec((ATT_TQ, LANES), lambda b, h, i: (b * nq + i, h)),
        out_shape=jax.ShapeDtypeStruct((bsz * seq, DA_WIDTH), BF16),
        scratch_shapes=[pltpu.VMEM((ATT_TQ, seq), F32)],
        compiler_params=_cparams(("parallel", "parallel", "arbitrary")),
        name="diff_attention",
    )(lam_params, proj_x, proj_x, proj_x, proj_m, proj_m, gain)


def _ml_state_update(h, k, v, b_col, li_col, b_end, cmat_ref, n_ref, m_ref):
    m_prev = m_ref[h][0:1, 0:1]
    g = b_end - b_col + li_col
    m_new = jnp.maximum(b_end + m_prev, jnp.max(g, axis=0, keepdims=True))
    decay = jnp.exp(b_end + m_prev - m_new)
    kw = (k.astype(F32) * jnp.exp(g - m_new)).astype(BF16)
    cmat_ref[h] = decay * cmat_ref[h] + _dot_tn(kw, v)
    n_ref[h] = decay * n_ref[h] + jnp.sum(kw.astype(F32), axis=0, keepdims=True)
    m_ref[h] = jnp.broadcast_to(m_new, (SUBLANES, LANES))


def _mlstm_kernel(reverse, *refs):
    if reverse:
        (q_ref, k_ref, v_ref, g_ref, gt_ref, br_ref, bc_ref,
         out_ref, cmat_ref, n_ref, m_ref) = refs
    else:
        (q_ref, k_ref, v_ref, g_ref, gt_ref, br_ref, bc_ref,
         km_ref, vm_ref, gm_ref, hb_ref, op_ref, gain_ref,
         out_ref, cmat_ref, n_ref, m_ref) = refs
    c = pl.program_id(1)
    chunk = q_ref.shape[0]
    gate0 = 2 * ML_HEADS if reverse else 0

    @pl.when(c == 0)
    def _():
        cmat_ref[...] = jnp.zeros_like(cmat_ref)
        n_ref[...] = jnp.zeros_like(n_ref)
        m_ref[...] = jnp.zeros_like(m_ref)
        if not reverse:
            gm = gm_ref[...] + br_ref[...]
            r = lax.broadcasted_iota(jnp.int32, (N_META, N_META), 0)
            s = lax.broadcasted_iota(jnp.int32, (N_META, N_META), 1)
            bcm = _dot_exact((s <= r).astype(F32), _log_sigmoid(gm))
            for h in range(ML_HEADS):
                gi, gf = gate0 + h, gate0 + ML_HEADS + h
                _ml_state_update(
                    h, km_ref[0][:, h * ML_QK_DIM:(h + 1) * ML_QK_DIM],
                    vm_ref[:, h * ML_V_DIM:(h + 1) * ML_V_DIM],
                    bcm[:, gf:gf + 1], gm[:, gi:gi + 1], bcm[N_META - 1:N_META, gf:gf + 1],
                    cmat_ref, n_ref, m_ref)

    gcol = g_ref[...] + br_ref[...]
    grow = gt_ref[0] + bc_ref[...]
    row = lax.broadcasted_iota(jnp.int32, (chunk, chunk), 0)
    col = lax.broadcasted_iota(jnp.int32, (chunk, chunk), 1)
    mask = (col >= row) if reverse else (col <= row)
    tri = mask.astype(F32)
    bcol_all = _dot_exact(tri, _log_sigmoid(gcol))
    tri_t = ((row >= col) if reverse else (row <= col)).astype(F32)
    brow_all = _dot_exact(_log_sigmoid(grow), tri_t)
    end = 0 if reverse else chunk - 1

    for h in range(ML_HEADS):
        gi, gf = gate0 + h, gate0 + ML_HEADS + h
        q = q_ref[:, h * ML_QK_DIM:(h + 1) * ML_QK_DIM]
        k = k_ref[:, h * ML_QK_DIM:(h + 1) * ML_QK_DIM]
        v = v_ref[:, h * ML_V_DIM:(h + 1) * ML_V_DIM]
        b_col = bcol_all[:, gf:gf + 1]
        li_col = gcol[:, gi:gi + 1]
        b_row = brow_all[gf:gf + 1, :]
        li_row = grow[gi:gi + 1, :]
        m_prev = m_ref[h][0:1, 0:1]

        d = jnp.where(mask, b_col - b_row + li_row, NEG)
        m_inter = b_col + m_prev
        m_t = jnp.maximum(m_inter, jnp.max(d, axis=1, keepdims=True))
        w_inter = jnp.exp(m_inter - m_t)
        sg = _dot_nt(q, k) * jnp.exp(d - m_t)
        num = w_inter * _dot(q, cmat_ref[h].astype(BF16)) + _dot(sg.astype(BF16), v)
        den = (w_inter * jnp.sum(q.astype(F32) * n_ref[h], axis=1, keepdims=True)
               + jnp.sum(sg, axis=1, keepdims=True))
        hout = num / jnp.maximum(jnp.abs(den), jnp.exp(-m_t))

        _ml_state_update(h, k, v, b_col, li_col, b_col[end:end + 1, :], cmat_ref, n_ref, m_ref)

        cols = slice(h * ML_V_DIM, (h + 1) * ML_V_DIM)
        if reverse:
            out_ref[:, cols] = hout
        else:
            ht = hout + hb_ref[:, cols]
            y = ht * lax.rsqrt(jnp.mean(ht * ht, axis=1, keepdims=True) + EPS) * gain_ref[:, cols]
            out_ref[:, cols] = (_sigmoid(op_ref[:, cols].astype(F32)) * y).astype(BF16)


def _mlstm(reverse, qk, proj_x, gates, gates_t, bias_row, bias_col, bsz, seq, fwd_extra=None):
    nc = seq // ML_CHUNK
    if reverse:
        rowblk = lambda b, c: b * nc + (nc - 1 - c)
        tblk = lambda c: nc - 1 - c
    else:
        rowblk = lambda b, c: b * nc + c
        tblk = lambda c: c
    qk_w = ML_HEADS * ML_QK_DIM
    in_specs = [
        pl.BlockSpec((ML_CHUNK, qk_w), lambda b, c: (rowblk(b, c), 0)),
        pl.BlockSpec((ML_CHUNK, qk_w), lambda b, c: (rowblk(b, c), 1)),
        pl.BlockSpec((ML_CHUNK, ML_WIDTH), lambda b, c: (rowblk(b, c), COL_MV)),
        pl.BlockSpec((ML_CHUNK, LANES), lambda b, c: (rowblk(b, c), 0)),
        pl.BlockSpec((1, N_GATE_COLS, ML_CHUNK), lambda b, c: (b, 0, tblk(c))),
        pl.BlockSpec((1, LANES), lambda b, c: (0, 0)),
        pl.BlockSpec((N_GATE_COLS, 1), lambda b, c: (0, 0)),
    ]
    args = [qk, qk, proj_x, gates, gates_t, bias_row, bias_col]
    if reverse:
        out_dtype = F32
    else:
        qk_m, proj_m, gates_m, hb, gain = fwd_extra
        in_specs += [
            pl.BlockSpec((1, N_META, qk_w), lambda b, c: (b, 0, 1)),
            pl.BlockSpec((N_META, ML_WIDTH), lambda b, c: (0, COL_MV)),
            pl.BlockSpec((N_META, LANES), lambda b, c: (0, 0)),
            pl.BlockSpec((ML_CHUNK, ML_WIDTH), lambda b, c: (rowblk(b, c), 0)),
            pl.BlockSpec((ML_CHUNK, ML_WIDTH), lambda b, c: (rowblk(b, c), COL_MO)),
            pl.BlockSpec((1, ML_WIDTH), lambda b, c: (0, 0)),
        ]
        args += [qk_m, proj_m, gates_m, hb, proj_x, gain]
        out_dtype = BF16
    return pl.pallas_call(
        functools.partial(_mlstm_kernel, reverse),
        grid=(bsz, nc),
        in_specs=in_specs,
        out_specs=pl.BlockSpec((ML_CHUNK, ML_WIDTH), lambda b, c: (rowblk(b, c), 0)),
        out_shape=jax.ShapeDtypeStruct((bsz * seq, ML_WIDTH), out_dtype),
        scratch_shapes=[
            pltpu.VMEM((ML_HEADS, ML_QK_DIM, ML_V_DIM), F32),
            pltpu.VMEM((ML_HEADS, 1, ML_QK_DIM), F32),
            pltpu.VMEM((ML_HEADS, SUBLANES, LANES), F32),
        ],
        compiler_params=_cparams(("parallel", "arbitrary")),
        name="mlstm_bwd" if reverse else "mlstm_fwd",
    )(*args)


def _outproj_kernel(a_ref, m_ref, x_ref, wa_ref, wm_ref, g_ref, h_ref, u_ref):
    h = _dot(a_ref[...], wa_ref[...]) + _dot(m_ref[...], wm_ref[...]) + x_ref[...]
    h_ref[...] = h
    u = h * lax.rsqrt(jnp.mean(h * h, axis=-1, keepdims=True) + EPS) * g_ref[...]
    u_ref[...] = u.astype(BF16)


def _outproj(attn, ml, x_rows, w_a, w_m, norm_g):
    m = x_rows.shape[0]
    return pl.pallas_call(
        _outproj_kernel,
        grid=(m // OUT_TM,),
        in_specs=[
            pl.BlockSpec((OUT_TM, DA_WIDTH), lambda i: (i, 0)),
            pl.BlockSpec((OUT_TM, ML_WIDTH), lambda i: (i, 0)),
            pl.BlockSpec((OUT_TM, D_MODEL), lambda i: (i, 0)),
            pl.BlockSpec((DA_WIDTH, D_MODEL), lambda i: (0, 0)),
            pl.BlockSpec((ML_WIDTH, D_MODEL), lambda i: (0, 0)),
            pl.BlockSpec((1, D_MODEL), lambda i: (0, 0)),
        ],
        out_specs=[
            pl.BlockSpec((OUT_TM, D_MODEL), lambda i: (i, 0)),
            pl.BlockSpec((OUT_TM, D_MODEL), lambda i: (i, 0)),
        ],
        out_shape=[
            jax.ShapeDtypeStruct((m, D_MODEL), F32),
            jax.ShapeDtypeStruct((m, D_MODEL), BF16),
        ],
        compiler_params=_cparams(("parallel",)),
        name="outproj",
    )(attn, ml, x_rows, w_a, w_m, norm_g)


def _ffn_kernel(u_ref, h_ref, wg_ref, wu_ref, wd_ref, gf_ref, o_ref, acc_ref):
    j = pl.program_id(1)

    @pl.when(j == 0)
    def _():
        acc_ref[...] = h_ref[...]

    u = u_ref[...]
    g = _dot(u, wg_ref[...])
    up = _dot(u, wu_ref[...])
    ff = (g * _sigmoid(g) * up).astype(BF16)
    acc_ref[...] += _dot(ff, wd_ref[...])

    @pl.when(j == pl.num_programs(1) - 1)
    def _():
        y = acc_ref[...]
        o_ref[...] = y * lax.rsqrt(jnp.mean(y * y, axis=-1, keepdims=True) + EPS) * gf_ref[...]


def _ffn(u, h, w_gate, w_up, w_down, norm_g):
    m = u.shape[0]
    return pl.pallas_call(
        _ffn_kernel,
        grid=(m // FFN_TM, D_FF // FFN_TF),
        in_specs=[
            pl.BlockSpec((FFN_TM, D_MODEL), lambda i, j: (i, 0)),
            pl.BlockSpec((FFN_TM, D_MODEL), lambda i, j: (i, 0)),
            pl.BlockSpec((D_MODEL, FFN_TF), lambda i, j: (0, j)),
            pl.BlockSpec((D_MODEL, FFN_TF), lambda i, j: (0, j)),
            pl.BlockSpec((FFN_TF, D_MODEL), lambda i, j: (j, 0)),
            pl.BlockSpec((1, D_MODEL), lambda i, j: (0, 0)),
        ],
        out_specs=pl.BlockSpec((FFN_TM, D_MODEL), lambda i, j: (i, 0)),
        out_shape=jax.ShapeDtypeStruct((m, D_MODEL), F32),
        scratch_shapes=[pltpu.VMEM((FFN_TM, D_MODEL), F32)],
        compiler_params=_cparams(("parallel", "arbitrary")),
        name="ffn",
    )(u, h, w_gate, w_up, w_down, norm_g)


def _rope_tables(pos0, n):
    pos = jnp.arange(pos0, pos0 + n, dtype=F32)
    inv = ROPE_THETA ** (-jnp.arange(0, DA_ROT_DIM, 2, dtype=F32) / DA_ROT_DIM)
    ang = pos[:, None] * inv[None, :]
    cos8, sin8 = jnp.cos(ang), jnp.sin(ang)
    half = DA_ROT_DIM // 2
    pad = jnp.zeros((n, DA_QK_DIM - DA_ROT_DIM), F32)
    zero8 = jnp.zeros((n, half), F32)
    cos_c = jnp.concatenate([cos8, cos8, pad + 1.0], axis=1)
    sa_c = jnp.concatenate([-sin8, zero8, pad], axis=1)
    sb_c = jnp.concatenate([zero8, sin8, pad], axis=1)
    two = lambda t: jnp.concatenate([t, t], axis=1)
    return two(cos_c), two(sa_c), two(sb_c)


def _pad_lanes(a, width=LANES):
    return jnp.pad(a, ((0, 0), (0, width - a.shape[1])))


def kernel(x, meta_tokens, norm_mix, w_in, da_lambda_q1, da_lambda_k1, da_lambda_q2, da_lambda_k2,
           da_head_norm, ml_conv_w, ml_conv_b, ml_gate_bias, ml_head_norm, w_out, norm_ffn,
           w_gate, w_up, w_down, norm_final):
    bsz, seq, _ = x.shape
    x_rows = x.reshape(bsz * seq, D_MODEL)

    w_in0 = w_in[0]
    w_main = w_in0[:, :MAIN_COLS].astype(BF16)
    w_gates = _pad_lanes(w_in0[:, MAIN_COLS:]).astype(BF16)
    norm_mix0 = norm_mix[0][None, :]

    tabs_x = _rope_tables(N_META, seq)
    tabs_m = _rope_tables(0, N_META)
    proj_x, gates_x = _inproj(x_rows, norm_mix0, w_main, w_gates, *tabs_x, tm=IN_TM)
    proj_m, gates_m = _inproj(meta_tokens, norm_mix0, w_main, w_gates, *tabs_m, tm=N_META)

    conv_w = jnp.pad(ml_conv_w[0], ((0, SUBLANES - CONV_W), (0, 0)))
    qk_x, qk_m = _conv(proj_x, proj_m, conv_w, ml_conv_b[0][None, :], bsz, seq)

    lam_params = jnp.pad(
        jnp.stack([da_lambda_q1[0], da_lambda_k1[0], da_lambda_q2[0], da_lambda_k2[0]]),
        ((0, SUBLANES - 4), (0, LANES - DA_QK_DIM)))
    attn = _attention(lam_params, proj_x, proj_m, da_head_norm[0].reshape(1, DA_WIDTH), bsz, seq)

    bias = ml_gate_bias[0].reshape(N_GATE_COLS)
    bias_row = _pad_lanes(bias[None, :])
    bias_col = bias[:, None]
    gates_t = jnp.transpose(gates_x[:, :N_GATE_COLS].reshape(bsz, seq, N_GATE_COLS), (0, 2, 1))
    h_bwd = _mlstm(True, qk_x, proj_x, gates_x, gates_t, bias_row, bias_col, bsz, seq)
    ml = _mlstm(False, qk_x, proj_x, gates_x, gates_t, bias_row, bias_col, bsz, seq,
                fwd_extra=(qk_m, proj_m, gates_m, h_bwd, ml_head_norm[0].reshape(1, ML_WIDTH)))

    w_out0 = w_out[0].astype(BF16)
    h1, u2 = _outproj(attn, ml, x_rows, w_out0[:DA_WIDTH], w_out0[DA_WIDTH:], norm_ffn[0][None, :])

    out = _ffn(u2, h1, w_gate[0].astype(BF16), w_up[0].astype(BF16), w_down[0].astype(BF16),
               norm_final[None, :])
    return out.reshape(bsz, seq, D_MODEL)
```

```python
import functools

import jax
import jax.numpy as jnp
from jax import lax
from jax.experimental import pallas as pl
from jax.experimental.pallas import tpu as pltpu

F32 = jnp.float32
BF16 = jnp.bfloat16

D_MODEL = 2048
N_META = 16
EPS = 1e-6
ROPE_THETA = 500000.0
NEG = -1e30

DA_QK_DIM = 64
DA_V_DIM = 128
DA_HEADS = 8
DA_ROT_DIM = 16
DA_WIDTH = DA_HEADS * DA_V_DIM
LAM_INIT = 0.8 - 0.6 * 1.0
LOG2_E = 1.4426950408889634

ML_HEADS = 4
ML_V_DIM = 256
ML_QK_DIM = 128
ML_WIDTH = ML_HEADS * ML_V_DIM
CONV_W = 5
N_GATES = 4
N_GATE_COLS = N_GATES * ML_HEADS

D_FF = 5632
MAIN_COLS = 6144

COL_AQ, COL_AK, COL_AV, COL_MQK, COL_MV, COL_MO = 0, 1, 2, 3, 4, 5

LANES = 128
SUBLANES = 8
VMEM_LIMIT = 56 * 1024 * 1024

IN_TN = 1024
IN_TM = 512
ATT_TQ = 512
ATT_KB = 512
ML_CHUNK = 256
OUT_TM = 256
FFN_TM = 512
FFN_TF = 512
CONV_ROWS = 512


def _cparams(sem):
    return pltpu.CompilerParams(dimension_semantics=sem, vmem_limit_bytes=VMEM_LIMIT)


def _dot(a, b):
    return jnp.dot(a, b, preferred_element_type=F32)


def _dot_nt(a, b):
    return lax.dot_general(a, b, (((1,), (1,)), ((), ())), preferred_element_type=F32)


def _dot_tn(a, b):
    return lax.dot_general(a, b, (((0,), (0,)), ((), ())), preferred_element_type=F32)


def _dot_exact(a, b):
    return jnp.dot(a, b, preferred_element_type=F32, precision=lax.Precision.HIGHEST)


def _sigmoid(x):
    return 1.0 / (1.0 + jnp.exp(-x))


def _log_sigmoid(x):
    return jnp.minimum(x, 0.0) - jnp.log(1.0 + jnp.exp(-jnp.abs(x)))


def _inproj_kernel(x_ref, g_ref, w_ref, wg_ref, cos_ref, sa_ref, sb_ref,
                   proj_ref, gate_ref, u_ref):
    j = pl.program_id(1)

    @pl.when(j == 0)
    def _():
        x = x_ref[...]
        ms = jnp.mean(x * x, axis=-1, keepdims=True)
        u = (x * lax.rsqrt(ms + EPS) * g_ref[...]).astype(BF16)
        u_ref[...] = u
        gate_ref[...] = _dot(u, wg_ref[...])

    acc = _dot(u_ref[...], w_ref[...])

    @pl.when(j <= COL_AK)
    def _():
        scale = jnp.where(j == COL_AQ, DA_QK_DIM ** -0.5 * LOG2_E, 1.0).astype(F32)
        cos = cos_ref[...] * scale
        sa = sa_ref[...] * scale
        sb = sb_ref[...] * scale
        for h in range(DA_HEADS):
            t = acc[:, h * LANES:(h + 1) * LANES]
            r = (t * cos + pltpu.roll(t, LANES - DA_ROT_DIM // 2, 1) * sa
                 + pltpu.roll(t, DA_ROT_DIM // 2, 1) * sb)
            proj_ref[:, h * LANES:(h + 1) * LANES] = r.astype(BF16)

    @pl.when(j > COL_AK)
    def _():
        proj_ref[...] = acc.astype(BF16)


def _inproj(rows, norm_g, w_main, w_gate, cos_t, sa_t, sb_t, tm):
    m = rows.shape[0]
    tiles_per_seq = cos_t.shape[0] // tm
    tab_spec = pl.BlockSpec((tm, LANES), lambda i, j: (i % tiles_per_seq, 0))
    return pl.pallas_call(
        _inproj_kernel,
        grid=(m // tm, MAIN_COLS // IN_TN),
        in_specs=[
            pl.BlockSpec((tm, D_MODEL), lambda i, j: (i, 0)),
            pl.BlockSpec((1, D_MODEL), lambda i, j: (0, 0)),
            pl.BlockSpec((D_MODEL, IN_TN), lambda i, j: (0, j)),
            pl.BlockSpec((D_MODEL, LANES), lambda i, j: (0, 0)),
            tab_spec, tab_spec, tab_spec,
        ],
        out_specs=[
            pl.BlockSpec((tm, IN_TN), lambda i, j: (i, j)),
            pl.BlockSpec((tm, LANES), lambda i, j: (i, 0)),
        ],
        out_shape=[
            jax.ShapeDtypeStruct((m, MAIN_COLS), BF16),
            jax.ShapeDtypeStruct((m, LANES), F32),
        ],
        scratch_shapes=[pltpu.VMEM((tm, D_MODEL), BF16)],
        compiler_params=_cparams(("parallel", "arbitrary")),
        name="inproj",
    )(rows, norm_g, w_main, w_gate, cos_t, sa_t, sb_t)


_CONV_PAD = SUBLANES


def _conv_kernel(x_ref, m_ref, w_ref, b_ref, o_ref, om_ref, s_ref):
    seq = x_ref.shape[0]
    j = pl.program_id(1)
    zeros = jnp.zeros((_CONV_PAD, LANES), F32)
    s_ref[0:_CONV_PAD, :] = zeros
    s_ref[_CONV_PAD:_CONV_PAD + N_META, :] = m_ref[...].astype(F32)
    s_ref[_CONV_PAD + N_META:_CONV_PAD + N_META + seq, :] = x_ref[...].astype(F32)
    s_ref[_CONV_PAD + N_META + seq:, :] = zeros
    w = w_ref[...]
    bias = b_ref[...]
    scale = jnp.where(j < ML_HEADS, ML_QK_DIM ** -0.5, 1.0).astype(F32)

    def conv(pos, n):
        acc = jnp.zeros((n, LANES), F32) + bias
        for t in range(CONV_W):
            start = _CONV_PAD + pos + t - CONV_W // 2
            acc = acc + s_ref[start:start + n, :] * w[t:t + 1, :]
        return (acc * _sigmoid(acc) * scale).astype(BF16)

    om_ref[0] = conv(0, N_META)
    for c in range(seq // CONV_ROWS):
        o_ref[c * CONV_ROWS:(c + 1) * CONV_ROWS, :] = conv(N_META + c * CONV_ROWS, CONV_ROWS)


def _conv(proj_x, proj_m, conv_w, conv_b, bsz, seq):
    ngrp = 2 * ML_HEADS
    col0 = COL_MQK * (IN_TN // LANES)
    return pl.pallas_call(
        _conv_kernel,
        grid=(bsz, ngrp),
        in_specs=[
            pl.BlockSpec((seq, LANES), lambda b, j: (b, col0 + j)),
            pl.BlockSpec((N_META, LANES), lambda b, j: (0, col0 + j)),
            pl.BlockSpec((SUBLANES, LANES), lambda b, j: (0, j)),
            pl.BlockSpec((1, LANES), lambda b, j: (0, j)),
        ],
        out_specs=[
            pl.BlockSpec((seq, LANES), lambda b, j: (b, j)),
            pl.BlockSpec((1, N_META, LANES), lambda b, j: (b, 0, j)),
        ],
        out_shape=[
            jax.ShapeDtypeStruct((bsz * seq, ngrp * LANES), BF16),
            jax.ShapeDtypeStruct((bsz, N_META, ngrp * LANES), BF16),
        ],
        scratch_shapes=[pltpu.VMEM((seq + N_META + 2 * _CONV_PAD, LANES), F32)],
        compiler_params=_cparams(("parallel", "parallel")),
        name="mlstm_conv",
    )(proj_x, proj_m, conv_w, conv_b)


def _attn_kernel(lam_ref, q_ref, k_ref, vt_ref, km_ref, vmt_ref, gain_ref, o_ref,
                 s_ref, sm_ref, acc_ref, o_t_ref):
    tq = ATT_TQ
    nq = q_ref.shape[0] // tq
    nkb = k_ref.shape[0] // ATT_KB
    lp = lam_ref[...]
    lam = (jnp.exp(jnp.sum(lp[0:1] * lp[1:2], axis=1, keepdims=True))
           - jnp.exp(jnp.sum(lp[2:3] * lp[3:4], axis=1, keepdims=True)) + LAM_INIT)
    lane = lax.broadcasted_iota(jnp.int32, (tq, LANES), 1)
    comps = range(2)

    def q_rows(qt):
        return pl.ds(pl.multiple_of(qt * tq, tq), tq)

    def masked_q(qt):
        q = q_ref[q_rows(qt), :]
        zero = jnp.zeros_like(q)
        return (jnp.where(lane < DA_QK_DIM, q, zero), jnp.where(lane >= DA_QK_DIM, q, zero))

    def fold(a):
        return a.reshape(a.shape[0] // SUBLANES, SUBLANES, tq)

    def scores(qz, kb, slot):
        k = k_ref[kb * ATT_KB:(kb + 1) * ATT_KB, :]
        bmax = []
        for c in comps:
            s = _dot_nt(k, qz[c])
            s_ref[slot, c] = s
            bm = jnp.max(fold(s), axis=0)
            if kb == 0:
                sm = _dot_nt(km_ref[...], qz[c])
                sm_ref[c] = sm
                bm = jnp.maximum(bm, jnp.max(fold(sm), axis=0))
            bmax.append(bm)
        return tuple(bmax)

    def consume(kb, slot, bmax, m, l):
        vt = vt_ref[0, kb]
        m_out, l_out = [], []
        for c in comps:
            bm = jnp.max(bmax[c], axis=0, keepdims=True)
            if kb == 0:
                m_new = bm
                pm = jnp.exp2(sm_ref[c] - m_new)
                p = jnp.exp2(s_ref[slot, c] - m_new)
                l_out.append(jnp.sum(fold(p), axis=0) + jnp.sum(fold(pm), axis=0))
                acc_ref[c] = _dot(vt, p.astype(BF16)) + _dot(vmt_ref[...], pm.astype(BF16))
            else:
                m_new = jnp.maximum(m[c], bm)
                alpha = jnp.exp2(m[c] - m_new)
                p = jnp.exp2(s_ref[slot, c] - m_new)
                l_out.append(alpha * l[c] + jnp.sum(fold(p), axis=0))
                acc_ref[c] = alpha * acc_ref[c] + _dot(vt, p.astype(BF16))
            m_out.append(m_new)
        return tuple(m_out), tuple(l_out)

    def finalize(qt):
        o = o_t_ref[...].T
        y = (o * lax.rsqrt(jnp.mean(o * o, axis=1, keepdims=True) + EPS)
             * gain_ref[...] * (1.0 - LAM_INIT))
        o_ref[q_rows(qt), :] = y.astype(BF16)

    def tile(qt, bmax):
        finalize(jnp.maximum(qt - 1, 0))
        qz = masked_q(qt)
        m = l = None
        for kb in range(nkb):
            if kb + 1 < nkb:
                bmax_next = scores(qz, kb + 1, (kb + 1) % 2)
            else:
                bmax_next = scores(masked_q(jnp.minimum(qt + 1, nq - 1)), 0, 0)
            m, l = consume(kb, kb % 2, bmax, m, l)
            bmax = bmax_next
        l1 = jnp.sum(l[0], axis=0, keepdims=True)
        l2 = jnp.sum(l[1], axis=0, keepdims=True)
        o_t_ref[...] = acc_ref[0] / l1 - lam * (acc_ref[1] / l2)
        return bmax

    o_t_ref[...] = jnp.zeros_like(o_t_ref)
    lax.fori_loop(0, nq, tile, scores(masked_q(0), 0, 0))
    finalize(nq - 1)


def _attention(lam_params, proj_x, proj_m, vt_x, vt_m, gain, bsz, seq):
    hb = IN_TN // LANES
    return pl.pallas_call(
        _attn_kernel,
        grid=(bsz, DA_HEADS),
        in_specs=[
            pl.BlockSpec((SUBLANES, LANES), lambda b, h: (0, 0)),
            pl.BlockSpec((seq, LANES), lambda b, h: (b, COL_AQ * hb + h)),
            pl.BlockSpec((seq, LANES), lambda b, h: (b, COL_AK * hb + h)),
            pl.BlockSpec((1, seq // ATT_KB, DA_V_DIM, ATT_KB),
                         lambda b, h: (b * DA_HEADS + h, 0, 0, 0)),
            pl.BlockSpec((N_META, LANES), lambda b, h: (0, COL_AK * hb + h)),
            pl.BlockSpec((DA_V_DIM, N_META), lambda b, h: (h, 0)),
            pl.BlockSpec((1, LANES), lambda b, h: (0, h)),
        ],
        out_specs=pl.BlockSpec((seq, LANES), lambda b, h: (b, h)),
        out_shape=jax.ShapeDtypeStruct((bsz * seq, DA_WIDTH), BF16),
        scratch_shapes=[
            pltpu.VMEM((2, 2, ATT_KB, ATT_TQ), F32),
            pltpu.VMEM((2, N_META, ATT_TQ), F32),
            pltpu.VMEM((2, DA_V_DIM, ATT_TQ), F32),
            pltpu.VMEM((DA_V_DIM, ATT_TQ), F32),
        ],
        compiler_params=_cparams(("parallel", "parallel")),
        name="diff_attention",
    )(lam_params, proj_x, proj_x, vt_x, proj_m, vt_m, gain)


def _ml_state_update(h, k, v, b_col, li_col, b_end, cmat_ref, n_ref, m_ref):
    m_prev = m_ref[h][0:1, 0:1]
    g = b_end - b_col + li_col
    m_new = jnp.maximum(b_end + m_prev, jnp.max(g, axis=0, keepdims=True))
    decay = jnp.exp(b_end + m_prev - m_new)
    kw = (k.astype(F32) * jnp.exp(g - m_new)).astype(BF16)
    cmat_ref[h] = decay * cmat_ref[h] + _dot_tn(kw, v)
    n_ref[h] = decay * n_ref[h] + jnp.sum(kw.astype(F32), axis=0, keepdims=True)
    m_ref[h] = jnp.broadcast_to(m_new, (SUBLANES, LANES))


def _mlstm_kernel(reverse, *refs):
    if reverse:
        (q_ref, k_ref, v_ref, g_ref, gt_ref, br_ref, bc_ref,
         out_ref, cmat_ref, n_ref, m_ref) = refs
    else:
        (q_ref, k_ref, v_ref, g_ref, gt_ref, br_ref, bc_ref,
         km_ref, vm_ref, gm_ref, hb_ref, op_ref, gain_ref,
         out_ref, cmat_ref, n_ref, m_ref) = refs
    c = pl.program_id(1)
    chunk = q_ref.shape[0]
    gate0 = 2 * ML_HEADS if reverse else 0

    @pl.when(c == 0)
    def _():
        cmat_ref[...] = jnp.zeros_like(cmat_ref)
        n_ref[...] = jnp.zeros_like(n_ref)
        m_ref[...] = jnp.zeros_like(m_ref)
        if not reverse:
            gm = gm_ref[...] + br_ref[...]
            r = lax.broadcasted_iota(jnp.int32, (N_META, N_META), 0)
            s = lax.broadcasted_iota(jnp.int32, (N_META, N_META), 1)
            bcm = _dot_exact((s <= r).astype(F32), _log_sigmoid(gm))
            for h in range(ML_HEADS):
                gi, gf = gate0 + h, gate0 + ML_HEADS + h
                _ml_state_update(
                    h, km_ref[0][:, h * ML_QK_DIM:(h + 1) * ML_QK_DIM],
                    vm_ref[:, h * ML_V_DIM:(h + 1) * ML_V_DIM],
                    bcm[:, gf:gf + 1], gm[:, gi:gi + 1], bcm[N_META - 1:N_META, gf:gf + 1],
                    cmat_ref, n_ref, m_ref)

    gcol = g_ref[...] + br_ref[...]
    grow = gt_ref[0] + bc_ref[...]
    row = lax.broadcasted_iota(jnp.int32, (chunk, chunk), 0)
    col = lax.broadcasted_iota(jnp.int32, (chunk, chunk), 1)
    mask = (col >= row) if reverse else (col <= row)
    tri = mask.astype(F32)
    bcol_all = _dot_exact(tri, _log_sigmoid(gcol))
    tri_t = ((row >= col) if reverse else (row <= col)).astype(F32)
    brow_all = _dot_exact(_log_sigmoid(grow), tri_t)
    end = 0 if reverse else chunk - 1

    for h in range(ML_HEADS):
        gi, gf = gate0 + h, gate0 + ML_HEADS + h
        q = q_ref[:, h * ML_QK_DIM:(h + 1) * ML_QK_DIM]
        k = k_ref[:, h * ML_QK_DIM:(h + 1) * ML_QK_DIM]
        v = v_ref[:, h * ML_V_DIM:(h + 1) * ML_V_DIM]
        b_col = bcol_all[:, gf:gf + 1]
        li_col = gcol[:, gi:gi + 1]
        b_row = brow_all[gf:gf + 1, :]
        li_row = grow[gi:gi + 1, :]
        m_prev = m_ref[h][0:1, 0:1]

        d = jnp.where(mask, b_col - b_row + li_row, NEG)
        m_inter = b_col + m_prev
        m_t = jnp.maximum(m_inter, jnp.max(d, axis=1, keepdims=True))
        w_inter = jnp.exp(m_inter - m_t)
        sg = _dot_nt(q, k) * jnp.exp(d - m_t)
        num = w_inter * _dot(q, cmat_ref[h].astype(BF16)) + _dot(sg.astype(BF16), v)
        den = (w_inter * jnp.sum(q.astype(F32) * n_ref[h], axis=1, keepdims=True)
               + jnp.sum(sg, axis=1, keepdims=True))
        hout = num / jnp.maximum(jnp.abs(den), jnp.exp(-m_t))

        _ml_state_update(h, k, v, b_col, li_col, b_col[end:end + 1, :], cmat_ref, n_ref, m_ref)

        cols = slice(h * ML_V_DIM, (h + 1) * ML_V_DIM)
        if reverse:
            out_ref[:, cols] = hout
        else:
            ht = hout + hb_ref[:, cols]
            y = ht * lax.rsqrt(jnp.mean(ht * ht, axis=1, keepdims=True) + EPS) * gain_ref[:, cols]
            out_ref[:, cols] = (_sigmoid(op_ref[:, cols].astype(F32)) * y).astype(BF16)


def _mlstm(reverse, qk, proj_x, gates, gates_t, bias_row, bias_col, bsz, seq, fwd_extra=None):
    nc = seq // ML_CHUNK
    if reverse:
        rowblk = lambda b, c: b * nc + (nc - 1 - c)
        tblk = lambda c: nc - 1 - c
    else:
        rowblk = lambda b, c: b * nc + c
        tblk = lambda c: c
    qk_w = ML_HEADS * ML_QK_DIM
    in_specs = [
        pl.BlockSpec((ML_CHUNK, qk_w), lambda b, c: (rowblk(b, c), 0)),
        pl.BlockSpec((ML_CHUNK, qk_w), lambda b, c: (rowblk(b, c), 1)),
        pl.BlockSpec((ML_CHUNK, ML_WIDTH), lambda b, c: (rowblk(b, c), COL_MV)),
        pl.BlockSpec((ML_CHUNK, LANES), lambda b, c: (rowblk(b, c), 0)),
        pl.BlockSpec((1, N_GATE_COLS, ML_CHUNK), lambda b, c: (b, 0, tblk(c))),
        pl.BlockSpec((1, LANES), lambda b, c: (0, 0)),
        pl.BlockSpec((N_GATE_COLS, 1), lambda b, c: (0, 0)),
    ]
    args = [qk, qk, proj_x, gates, gates_t, bias_row, bias_col]
    if reverse:
        out_dtype = F32
    else:
        qk_m, proj_m, gates_m, hb, gain = fwd_extra
        in_specs += [
            pl.BlockSpec((1, N_META, qk_w), lambda b, c: (b, 0, 1)),
            pl.BlockSpec((N_META, ML_WIDTH), lambda b, c: (0, COL_MV)),
            pl.BlockSpec((N_META, LANES), lambda b, c: (0, 0)),
            pl.BlockSpec((ML_CHUNK, ML_WIDTH), lambda b, c: (rowblk(b, c), 0)),
            pl.BlockSpec((ML_CHUNK, ML_WIDTH), lambda b, c: (rowblk(b, c), COL_MO)),
            pl.BlockSpec((1, ML_WIDTH), lambda b, c: (0, 0)),
        ]
        args += [qk_m, proj_m, gates_m, hb, proj_x, gain]
        out_dtype = BF16
    return pl.pallas_call(
        functools.partial(_mlstm_kernel, reverse),
        grid=(bsz, nc),
        in_specs=in_specs,
        out_specs=pl.BlockSpec((ML_CHUNK, ML_WIDTH), lambda b, c: (rowblk(b, c), 0)),
        out_shape=jax.ShapeDtypeStruct((bsz * seq, ML_WIDTH), out_dtype),
        scratch_shapes=[
            pltpu.VMEM((ML_HEADS, ML_QK_DIM, ML_V_DIM), F32),
            pltpu.VMEM((ML_HEADS, 1, ML_QK_DIM), F32),
            pltpu.VMEM((ML_HEADS, SUBLANES, LANES), F32),
        ],
        compiler_params=_cparams(("parallel", "arbitrary")),
        name="mlstm_bwd" if reverse else "mlstm_fwd",
    )(*args)


def _outproj_kernel(a_ref, m_ref, x_ref, wa_ref, wm_ref, g_ref, h_ref, u_ref):
    h = _dot(a_ref[...], wa_ref[...]) + _dot(m_ref[...], wm_ref[...]) + x_ref[...]
    h_ref[...] = h
    u = h * lax.rsqrt(jnp.mean(h * h, axis=-1, keepdims=True) + EPS) * g_ref[...]
    u_ref[...] = u.astype(BF16)


def _outproj(attn, ml, x_rows, w_a, w_m, norm_g):
    m = x_rows.shape[0]
    return pl.pallas_call(
        _outproj_kernel,
        grid=(m // OUT_TM,),
        in_specs=[
            pl.BlockSpec((OUT_TM, DA_WIDTH), lambda i: (i, 0)),
            pl.BlockSpec((OUT_TM, ML_WIDTH), lambda i: (i, 0)),
            pl.BlockSpec((OUT_TM, D_MODEL), lambda i: (i, 0)),
            pl.BlockSpec((DA_WIDTH, D_MODEL), lambda i: (0, 0)),
            pl.BlockSpec((ML_WIDTH, D_MODEL), lambda i: (0, 0)),
            pl.BlockSpec((1, D_MODEL), lambda i: (0, 0)),
        ],
        out_specs=[
            pl.BlockSpec((OUT_TM, D_MODEL), lambda i: (i, 0)),
            pl.BlockSpec((OUT_TM, D_MODEL), lambda i: (i, 0)),
        ],
        out_shape=[
            jax.ShapeDtypeStruct((m, D_MODEL), F32),
            jax.ShapeDtypeStruct((m, D_MODEL), BF16),
        ],
        compiler_params=_cparams(("parallel",)),
        name="outproj",
    )(attn, ml, x_rows, w_a, w_m, norm_g)


def _ffn_kernel(u_ref, h_ref, wg_ref, wu_ref, wd_ref, gf_ref, o_ref, acc_ref):
    j = pl.program_id(1)

    @pl.when(j == 0)
    def _():
        acc_ref[...] = h_ref[...]

    u = u_ref[...]
    g = _dot(u, wg_ref[...])
    up = _dot(u, wu_ref[...])
    ff = (g * _sigmoid(g) * up).astype(BF16)
    acc_ref[...] += _dot(ff, wd_ref[...])

    @pl.when(j == pl.num_programs(1) - 1)
    def _():
        y = acc_ref[...]
        o_ref[...] = y * lax.rsqrt(jnp.mean(y * y, axis=-1, keepdims=True) + EPS) * gf_ref[...]


def _ffn(u, h, w_gate, w_up, w_down, norm_g):
    m = u.shape[0]
    return pl.pallas_call(
        _ffn_kernel,
        grid=(m // FFN_TM, D_FF // FFN_TF),
        in_specs=[
            pl.BlockSpec((FFN_TM, D_MODEL), lambda i, j: (i, 0)),
            pl.BlockSpec((FFN_TM, D_MODEL), lambda i, j: (i, 0)),
            pl.BlockSpec((D_MODEL, FFN_TF), lambda i, j: (0, j)),
            pl.BlockSpec((D_MODEL, FFN_TF), lambda i, j: (0, j)),
            pl.BlockSpec((FFN_TF, D_MODEL), lambda i, j: (j, 0)),
            pl.BlockSpec((1, D_MODEL), lambda i, j: (0, 0)),
        ],
        out_specs=pl.BlockSpec((FFN_TM, D_MODEL), lambda i, j: (i, 0)),
        out_shape=jax.ShapeDtypeStruct((m, D_MODEL), F32),
        scratch_shapes=[pltpu.VMEM((FFN_TM, D_MODEL), F32)],
        compiler_params=_cparams(("parallel", "arbitrary")),
        name="ffn",
    )(u, h, w_gate, w_up, w_down, norm_g)


def _rope_tables(pos0, n):
    pos = jnp.arange(pos0, pos0 + n, dtype=F32)
    inv = ROPE_THETA ** (-jnp.arange(0, DA_ROT_DIM, 2, dtype=F32) / DA_ROT_DIM)
    ang = pos[:, None] * inv[None, :]
    cos8, sin8 = jnp.cos(ang), jnp.sin(ang)
    half = DA_ROT_DIM // 2
    pad = jnp.zeros((n, DA_QK_DIM - DA_ROT_DIM), F32)
    zero8 = jnp.zeros((n, half), F32)
    cos_c = jnp.concatenate([cos8, cos8, pad + 1.0], axis=1)
    sa_c = jnp.concatenate([-sin8, zero8, pad], axis=1)
    sb_c = jnp.concatenate([zero8, sin8, pad], axis=1)
    two = lambda t: jnp.concatenate([t, t], axis=1)
    return two(cos_c), two(sa_c), two(sb_c)


def _pad_lanes(a, width=LANES):
    return jnp.pad(a, ((0, 0), (0, width - a.shape[1])))


def kernel(x, meta_tokens, norm_mix, w_in, da_lambda_q1, da_lambda_k1, da_lambda_q2, da_lambda_k2,
           da_head_norm, ml_conv_w, ml_conv_b, ml_gate_bias, ml_head_norm, w_out, norm_ffn,
           w_gate, w_up, w_down, norm_final):
    bsz, seq, _ = x.shape
    x_rows = x.reshape(bsz * seq, D_MODEL)

    w_in0 = w_in[0]
    w_main = w_in0[:, :MAIN_COLS].astype(BF16)
    w_gates = _pad_lanes(w_in0[:, MAIN_COLS:]).astype(BF16)
    norm_mix0 = norm_mix[0][None, :]

    tabs_x = _rope_tables(N_META, seq)
    tabs_m = _rope_tables(0, N_META)
    proj_x, gates_x = _inproj(x_rows, norm_mix0, w_main, w_gates, *tabs_x, tm=IN_TM)
    proj_m, gates_m = _inproj(meta_tokens, norm_mix0, w_main, w_gates, *tabs_m, tm=N_META)

    conv_w = jnp.pad(ml_conv_w[0], ((0, SUBLANES - CONV_W), (0, 0)))
    qk_x, qk_m = _conv(proj_x, proj_m, conv_w, ml_conv_b[0][None, :], bsz, seq)

    lam_params = jnp.pad(
        jnp.stack([da_lambda_q1[0], da_lambda_k1[0], da_lambda_q2[0], da_lambda_k2[0]]),
        ((0, SUBLANES - 4), (0, LANES - DA_QK_DIM)))
    av0 = COL_AV * IN_TN
    nkb = seq // ATT_KB
    vt_x = jnp.transpose(
        proj_x[:, av0:av0 + DA_WIDTH].reshape(bsz, nkb, ATT_KB, DA_HEADS, DA_V_DIM),
        (0, 3, 1, 4, 2)).reshape(bsz * DA_HEADS, nkb, DA_V_DIM, ATT_KB)
    vt_m = jnp.transpose(proj_m[:, av0:av0 + DA_WIDTH])
    attn = _attention(lam_params, proj_x, proj_m, vt_x, vt_m,
                      da_head_norm[0].reshape(1, DA_WIDTH), bsz, seq)

    bias = ml_gate_bias[0].reshape(N_GATE_COLS)
    bias_row = _pad_lanes(bias[None, :])
    bias_col = bias[:, None]
    gates_t = jnp.transpose(gates_x[:, :N_GATE_COLS].reshape(bsz, seq, N_GATE_COLS), (0, 2, 1))
    h_bwd = _mlstm(True, qk_x, proj_x, gates_x, gates_t, bias_row, bias_col, bsz, seq)
    ml = _mlstm(False, qk_x, proj_x, gates_x, gates_t, bias_row, bias_col, bsz, seq,
                fwd_extra=(qk_m, proj_m, gates_m, h_bwd, ml_head_norm[0].reshape(1, ML_WIDTH)))

    w_out0 = w_out[0].astype(BF16)
    h1, u2 = _outproj(attn, ml, x_rows, w_out0[:DA_WIDTH], w_out0[DA_WIDTH:], norm_ffn[0][None, :])

    out = _ffn(u2, h1, w_gate[0].astype(BF16), w_up[0].astype(BF16), w_down[0].astype(BF16),
               norm_final[None, :])
    return out.reshape(bsz, seq, D_MODEL)
```

```python
import functools

import jax
import jax.numpy as jnp
from jax import lax
from jax.experimental import pallas as pl
from jax.experimental.pallas import tpu as pltpu

F32 = jnp.float32
BF16 = jnp.bfloat16

D_MODEL = 2048
N_META = 16
EPS = 1e-6
ROPE_THETA = 500000.0
NEG = -1e30

DA_QK_DIM = 64
DA_V_DIM = 128
DA_HEADS = 8
DA_ROT_DIM = 16
DA_WIDTH = DA_HEADS * DA_V_DIM
LAM_INIT = 0.8 - 0.6 * 1.0
LOG2_E = 1.4426950408889634

ML_HEADS = 4
ML_V_DIM = 256
ML_QK_DIM = 128
ML_WIDTH = ML_HEADS * ML_V_DIM
ML_QK_WIDTH = ML_HEADS * ML_QK_DIM
CONV_W = 5
N_GATES = 4
N_GATE_COLS = N_GATES * ML_HEADS

D_FF = 5632

IN_TN = 1024
NAT_AQ, NAT_AK, NAT_MQK = 0, 1, 2
TR_AV, TR_MV, TR_MO = 0, 1, 2
N_NAT = 3
N_TR = 3

LANES = 128
SUBLANES = 8
VMEM_LIMIT = 56 * 1024 * 1024

IN_TM = 512
ATT_TQ = 512
ATT_KB = 512
ML_CHUNK = 256
OUT_TM = 256
FFN_TM = 512
FFN_TF = 512
CONV_ROWS = 512


def _cparams(sem):
    return pltpu.CompilerParams(dimension_semantics=sem, vmem_limit_bytes=VMEM_LIMIT)


def _dot(a, b):
    return jnp.dot(a, b, preferred_element_type=F32)


def _dot_nt(a, b):
    return lax.dot_general(a, b, (((1,), (1,)), ((), ())), preferred_element_type=F32)


def _dot_tn(a, b):
    return lax.dot_general(a, b, (((0,), (0,)), ((), ())), preferred_element_type=F32)


def _bf16_pieces(x):
    hi = x.astype(BF16)
    rest = x - hi.astype(F32)
    mid = rest.astype(BF16)
    return hi, mid, (rest - mid.astype(F32)).astype(BF16)


def _cumsum_rows(x, visible):
    r = x.shape[0]
    ones = jnp.where(visible, 1.0, 0.0).astype(BF16)
    parts = _dot(jnp.concatenate(_bf16_pieces(x), axis=0), ones)
    return parts[0:r] + parts[r:2 * r] + parts[2 * r:3 * r]


def _cumsum_cols(x, visible):
    c = x.shape[1]
    ones = jnp.where(visible, 1.0, 0.0).astype(BF16)
    parts = _dot(ones, jnp.concatenate(_bf16_pieces(x), axis=1))
    return parts[:, 0:c] + parts[:, c:2 * c] + parts[:, 2 * c:3 * c]


def _sigmoid(x):
    return 1.0 / (1.0 + jnp.exp(-x))


def _log_sigmoid(x):
    return jnp.minimum(x, 0.0) - jnp.log(1.0 + jnp.exp(-jnp.abs(x)))


def _inproj_kernel(x_ref, g_ref, wn_ref, wt_ref, wg_ref, cos_ref, sa_ref, sb_ref,
                   nat_ref, tr_ref, gate_ref, u_ref):
    j = pl.program_id(1)

    @pl.when(j == 0)
    def _():
        x = x_ref[...]
        ms = jnp.mean(x * x, axis=-1, keepdims=True)
        u = (x * lax.rsqrt(ms + EPS) * g_ref[...]).astype(BF16)
        u_ref[...] = u
        gate_ref[...] = _dot(u, wg_ref[...])

    @pl.when(j <= NAT_AK)
    def _():
        acc = _dot(u_ref[...], wn_ref[...])
        scale = jnp.where(j == NAT_AQ, DA_QK_DIM ** -0.5 * LOG2_E, 1.0).astype(F32)
        cos = cos_ref[...] * scale
        sa = sa_ref[...] * scale
        sb = sb_ref[...] * scale
        for h in range(DA_HEADS):
            t = acc[:, h * LANES:(h + 1) * LANES]
            r = (t * cos + pltpu.roll(t, LANES - DA_ROT_DIM // 2, 1) * sa
                 + pltpu.roll(t, DA_ROT_DIM // 2, 1) * sb)
            nat_ref[:, h * LANES:(h + 1) * LANES] = r.astype(BF16)

    @pl.when(j == NAT_MQK)
    def _():
        nat_ref[...] = _dot(u_ref[...], wn_ref[...]).astype(BF16)

    @pl.when(j >= N_NAT)
    def _():
        tr_ref[...] = _dot_nt(wt_ref[...], u_ref[...]).astype(BF16)


def _inproj(rows, norm_g, w_nat, w_tr_t, w_gate, cos_t, sa_t, sb_t, tm):
    m = rows.shape[0]
    tiles_per_seq = cos_t.shape[0] // tm
    tab_spec = pl.BlockSpec((tm, LANES), lambda i, j: (i % tiles_per_seq, 0))
    nat_j = lambda j: jnp.minimum(j, N_NAT - 1)
    tr_j = lambda j: jnp.maximum(j - N_NAT, 0)
    return pl.pallas_call(
        _inproj_kernel,
        grid=(m // tm, N_NAT + N_TR),
        in_specs=[
            pl.BlockSpec((tm, D_MODEL), lambda i, j: (i, 0)),
            pl.BlockSpec((1, D_MODEL), lambda i, j: (0, 0)),
            pl.BlockSpec((D_MODEL, IN_TN), lambda i, j: (0, nat_j(j))),
            pl.BlockSpec((IN_TN, D_MODEL), lambda i, j: (tr_j(j), 0)),
            pl.BlockSpec((D_MODEL, LANES), lambda i, j: (0, 0)),
            tab_spec, tab_spec, tab_spec,
        ],
        out_specs=[
            pl.BlockSpec((tm, IN_TN), lambda i, j: (i, nat_j(j))),
            pl.BlockSpec((IN_TN, tm), lambda i, j: (tr_j(j), i)),
            pl.BlockSpec((tm, LANES), lambda i, j: (i, 0)),
        ],
        out_shape=[
            jax.ShapeDtypeStruct((m, N_NAT * IN_TN), BF16),
            jax.ShapeDtypeStruct((N_TR * IN_TN, m), BF16),
            jax.ShapeDtypeStruct((m, LANES), F32),
        ],
        scratch_shapes=[pltpu.VMEM((tm, D_MODEL), BF16)],
        compiler_params=_cparams(("parallel", "arbitrary")),
        name="inproj",
    )(rows, norm_g, w_nat, w_tr_t, w_gate, cos_t, sa_t, sb_t)


_CONV_PAD = SUBLANES


def _conv_kernel(x_ref, m_ref, w_ref, b_ref, qt_ref, k_ref, km_ref, s_ref):
    seq = x_ref.shape[0]
    j = pl.program_id(1)
    zeros = jnp.zeros((_CONV_PAD, LANES), F32)
    s_ref[0:_CONV_PAD, :] = zeros
    s_ref[_CONV_PAD:_CONV_PAD + N_META, :] = m_ref[...].astype(F32)
    s_ref[_CONV_PAD + N_META:_CONV_PAD + N_META + seq, :] = x_ref[...].astype(F32)
    s_ref[_CONV_PAD + N_META + seq:, :] = zeros
    w = w_ref[...]
    bias = b_ref[...]

    def conv(pos, n):
        acc = jnp.zeros((n, LANES), F32) + bias
        for t in range(CONV_W):
            start = _CONV_PAD + pos + t - CONV_W // 2
            acc = acc + s_ref[start:start + n, :] * w[t:t + 1, :]
        return acc * _sigmoid(acc)

    @pl.when(j < ML_HEADS)
    def _():
        for c in range(seq // CONV_ROWS):
            y = conv(N_META + c * CONV_ROWS, CONV_ROWS) * (ML_QK_DIM ** -0.5)
            qt_ref[:, c * CONV_ROWS:(c + 1) * CONV_ROWS] = y.T.astype(BF16)

    @pl.when(j >= ML_HEADS)
    def _():
        km_ref[0] = conv(0, N_META).astype(BF16)
        for c in range(seq // CONV_ROWS):
            k_ref[c * CONV_ROWS:(c + 1) * CONV_ROWS, :] = conv(
                N_META + c * CONV_ROWS, CONV_ROWS).astype(BF16)


def _conv(nat_x, nat_m, conv_w, conv_b, bsz, seq):
    col0 = NAT_MQK * (IN_TN // LANES)
    q_j = lambda j: jnp.minimum(j, ML_HEADS - 1)
    k_j = lambda j: jnp.maximum(j - ML_HEADS, 0)
    return pl.pallas_call(
        _conv_kernel,
        grid=(bsz, 2 * ML_HEADS),
        in_specs=[
            pl.BlockSpec((seq, LANES), lambda b, j: (b, col0 + j)),
            pl.BlockSpec((N_META, LANES), lambda b, j: (0, col0 + j)),
            pl.BlockSpec((SUBLANES, LANES), lambda b, j: (0, j)),
            pl.BlockSpec((1, LANES), lambda b, j: (0, j)),
        ],
        out_specs=[
            pl.BlockSpec((ML_QK_DIM, seq), lambda b, j: (q_j(j), b)),
            pl.BlockSpec((seq, ML_QK_DIM), lambda b, j: (b, k_j(j))),
            pl.BlockSpec((1, N_META, ML_QK_DIM), lambda b, j: (b, 0, k_j(j))),
        ],
        out_shape=[
            jax.ShapeDtypeStruct((ML_QK_WIDTH, bsz * seq), BF16),
            jax.ShapeDtypeStruct((bsz * seq, ML_QK_WIDTH), BF16),
            jax.ShapeDtypeStruct((bsz, N_META, ML_QK_WIDTH), BF16),
        ],
        scratch_shapes=[pltpu.VMEM((seq + N_META + 2 * _CONV_PAD, LANES), F32)],
        compiler_params=_cparams(("parallel", "arbitrary")),
        name="mlstm_conv",
    )(nat_x, nat_m, conv_w, conv_b)


def _attn_kernel(lam_ref, q_ref, k_ref, vt_ref, km_ref, vmt_ref, gain_ref, o_ref,
                 s_ref, sm_ref, acc_ref, o_t_ref):
    tq = ATT_TQ
    nq = q_ref.shape[0] // tq
    nkb = k_ref.shape[0] // ATT_KB
    lp = lam_ref[...]
    lam = (jnp.exp(jnp.sum(lp[0:1] * lp[1:2], axis=1, keepdims=True))
           - jnp.exp(jnp.sum(lp[2:3] * lp[3:4], axis=1, keepdims=True)) + LAM_INIT)
    lane = lax.broadcasted_iota(jnp.int32, (tq, LANES), 1)
    comps = range(2)

    def q_rows(qt):
        return pl.ds(pl.multiple_of(qt * tq, tq), tq)

    def masked_q(qt):
        q = q_ref[q_rows(qt), :]
        zero = jnp.zeros_like(q)
        return (jnp.where(lane < DA_QK_DIM, q, zero), jnp.where(lane >= DA_QK_DIM, q, zero))

    def fold(a):
        return a.reshape(a.shape[0] // SUBLANES, SUBLANES, tq)

    def scores(qz, kb, slot):
        k = k_ref[kb * ATT_KB:(kb + 1) * ATT_KB, :]
        bmax = []
        for c in comps:
            s = _dot_nt(k, qz[c])
            s_ref[slot, c] = s
            bm = jnp.max(fold(s), axis=0)
            if kb == 0:
                sm = _dot_nt(km_ref[...], qz[c])
                sm_ref[c] = sm
                bm = jnp.maximum(bm, jnp.max(fold(sm), axis=0))
            bmax.append(bm)
        return tuple(bmax)

    def consume(kb, slot, bmax, m, l):
        vt = vt_ref[:, kb * ATT_KB:(kb + 1) * ATT_KB]
        m_out, l_out = [], []
        for c in comps:
            bm = jnp.max(bmax[c], axis=0, keepdims=True)
            if kb == 0:
                m_new = bm
                pm = jnp.exp2(sm_ref[c] - m_new)
                p = jnp.exp2(s_ref[slot, c] - m_new)
                l_out.append(jnp.sum(fold(p), axis=0) + jnp.sum(fold(pm), axis=0))
                acc_ref[c] = _dot(vt, p.astype(BF16)) + _dot(vmt_ref[...], pm.astype(BF16))
            else:
                m_new = jnp.maximum(m[c], bm)
                alpha = jnp.exp2(m[c] - m_new)
                p = jnp.exp2(s_ref[slot, c] - m_new)
                l_out.append(alpha * l[c] + jnp.sum(fold(p), axis=0))
                acc_ref[c] = alpha * acc_ref[c] + _dot(vt, p.astype(BF16))
            m_out.append(m_new)
        return tuple(m_out), tuple(l_out)

    def finalize(qt):
        o = o_t_ref[...].T
        y = (o * lax.rsqrt(jnp.mean(o * o, axis=1, keepdims=True) + EPS)
             * gain_ref[...] * (1.0 - LAM_INIT))
        o_ref[q_rows(qt), :] = y.astype(BF16)

    def tile(qt, bmax):
        finalize(jnp.maximum(qt - 1, 0))
        qz = masked_q(qt)
        m = l = None
        for kb in range(nkb):
            if kb + 1 < nkb:
                bmax_next = scores(qz, kb + 1, (kb + 1) % 2)
            else:
                bmax_next = scores(masked_q(jnp.minimum(qt + 1, nq - 1)), 0, 0)
            m, l = consume(kb, kb % 2, bmax, m, l)
            bmax = bmax_next
        l1 = jnp.sum(l[0], axis=0, keepdims=True)
        l2 = jnp.sum(l[1], axis=0, keepdims=True)
        o_t_ref[...] = acc_ref[0] / l1 - lam * (acc_ref[1] / l2)
        return bmax

    o_t_ref[...] = jnp.zeros_like(o_t_ref)
    lax.fori_loop(0, nq, tile, scores(masked_q(0), 0, 0))
    finalize(nq - 1)


def _attention(lam_params, nat_x, nat_m, tr_x, tr_m, gain, bsz, seq):
    hb = IN_TN // LANES
    return pl.pallas_call(
        _attn_kernel,
        grid=(bsz, DA_HEADS),
        in_specs=[
            pl.BlockSpec((SUBLANES, LANES), lambda b, h: (0, 0)),
            pl.BlockSpec((seq, LANES), lambda b, h: (b, NAT_AQ * hb + h)),
            pl.BlockSpec((seq, LANES), lambda b, h: (b, NAT_AK * hb + h)),
            pl.BlockSpec((DA_V_DIM, seq), lambda b, h: (TR_AV * DA_HEADS + h, b)),
            pl.BlockSpec((N_META, LANES), lambda b, h: (0, NAT_AK * hb + h)),
            pl.BlockSpec((DA_V_DIM, N_META), lambda b, h: (TR_AV * DA_HEADS + h, 0)),
            pl.BlockSpec((1, LANES), lambda b, h: (0, h)),
        ],
        out_specs=pl.BlockSpec((seq, LANES), lambda b, h: (b, h)),
        out_shape=jax.ShapeDtypeStruct((bsz * seq, DA_WIDTH), BF16),
        scratch_shapes=[
            pltpu.VMEM((2, 2, ATT_KB, ATT_TQ), F32),
            pltpu.VMEM((2, N_META, ATT_TQ), F32),
            pltpu.VMEM((2, DA_V_DIM, ATT_TQ), F32),
            pltpu.VMEM((DA_V_DIM, ATT_TQ), F32),
        ],
        compiler_params=_cparams(("parallel", "parallel")),
        name="diff_attention",
    )(lam_params, nat_x, nat_x, tr_x, nat_m, tr_m, gain)


def _ml_state_update(h, k, vt, a_col, b_end, g_row, ct_ref, n_ref, m_ref):
    m_prev = m_ref[h][0:1, 0:1]
    m_new = jnp.maximum(b_end + m_prev, jnp.max(g_row, axis=1, keepdims=True))
    decay = jnp.exp(b_end + m_prev - m_new)
    kw = (k.astype(F32) * jnp.exp(b_end + a_col - m_new)).astype(BF16)
    ct_ref[h] = decay * ct_ref[h] + _dot(vt, kw)
    n_ref[h] = decay * n_ref[h] + jnp.sum(kw.astype(F32), axis=0, keepdims=True)
    m_ref[h] = jnp.broadcast_to(m_new, (SUBLANES, LANES))


def _mlstm_kernel(reverse, *refs):
    if reverse:
        (qt_ref, k_ref, vt_ref, gt_ref, bc_ref,
         out_ref, ct_ref, n_ref, m_ref) = refs
    else:
        (qt_ref, k_ref, vt_ref, gt_ref, bc_ref,
         km_ref, vmt_ref, gm_ref, gmt_ref, br_ref, hb_ref, op_ref, gain_ref,
         out_ref, ct_ref, n_ref, m_ref) = refs
    c = pl.program_id(1)
    chunk = k_ref.shape[0]
    gate0 = 2 * ML_HEADS if reverse else 0

    @pl.when(c == 0)
    def _():
        ct_ref[...] = jnp.zeros_like(ct_ref)
        n_ref[...] = jnp.zeros_like(n_ref)
        m_ref[...] = jnp.zeros_like(m_ref)
        if not reverse:
            gm = gm_ref[...] + br_ref[...]
            gmt = gmt_ref[...] + bc_ref[...]
            r = lax.broadcasted_iota(jnp.int32, (N_META, N_META), 0)
            s = lax.broadcasted_iota(jnp.int32, (N_META, N_META), 1)
            bcm = _cumsum_cols(_log_sigmoid(gm), s <= r)
            brm = _cumsum_rows(_log_sigmoid(gmt), r <= s)
            for h in range(ML_HEADS):
                gi, gf = gate0 + h, gate0 + ML_HEADS + h
                b_end = brm[gf:gf + 1, N_META - 1:N_META]
                _ml_state_update(
                    h, km_ref[0][:, h * ML_QK_DIM:(h + 1) * ML_QK_DIM],
                    vmt_ref[h * ML_V_DIM:(h + 1) * ML_V_DIM, :],
                    gm[:, gi:gi + 1] - bcm[:, gf:gf + 1], b_end,
                    b_end - brm[gf:gf + 1, :] + gmt[gi:gi + 1, :],
                    ct_ref, n_ref, m_ref)

    grow = gt_ref[0] + bc_ref[...]
    row = lax.broadcasted_iota(jnp.int32, (chunk, chunk), 0)
    col = lax.broadcasted_iota(jnp.int32, (chunk, chunk), 1)
    mask = (row >= col) if reverse else (row <= col)
    brow_all = _cumsum_rows(_log_sigmoid(grow), mask)
    a_rows = grow - pltpu.roll(brow_all, N_GATE_COLS - ML_HEADS, 0)
    a_cols = jnp.concatenate(
        [a_rows, jnp.zeros((LANES - N_GATE_COLS, chunk), F32)], axis=0).T
    end = 0 if reverse else chunk - 1

    for h in range(ML_HEADS):
        gi, gf = gate0 + h, gate0 + ML_HEADS + h
        k = k_ref[:, h * ML_QK_DIM:(h + 1) * ML_QK_DIM]
        qt = qt_ref[h * ML_QK_DIM:(h + 1) * ML_QK_DIM, :]
        vt = vt_ref[h * ML_V_DIM:(h + 1) * ML_V_DIM, :]
        a_col = a_cols[:, gi:gi + 1]
        b_row = brow_all[gf:gf + 1, :]
        li_row = grow[gi:gi + 1, :]
        b_end = b_row[:, end:end + 1]
        m_prev = m_ref[h][0:1, 0:1]

        d = jnp.where(mask, b_row + a_col, NEG)
        m_inter = b_row + m_prev
        m_t = jnp.maximum(m_inter, jnp.max(d, axis=0, keepdims=True))
        w_inter = jnp.exp(m_inter - m_t)
        sg = _dot(k, qt) * jnp.exp(d - m_t)
        num = w_inter * _dot(ct_ref[h].astype(BF16), qt) + _dot(vt, sg.astype(BF16))
        n16 = jnp.broadcast_to(n_ref[h], (2 * SUBLANES, ML_QK_DIM)).astype(BF16)
        den = w_inter * _dot(n16, qt)[0:1, :] + jnp.sum(sg, axis=0, keepdims=True)
        hout = num / jnp.maximum(jnp.abs(den), jnp.exp(-m_t))

        _ml_state_update(h, k, vt, a_col, b_end, b_end - b_row + li_row, ct_ref, n_ref, m_ref)

        rows = slice(h * ML_V_DIM, (h + 1) * ML_V_DIM)
        if reverse:
            out_ref[rows, :] = hout
        else:
            ht = hout + hb_ref[rows, :]
            y = ht * lax.rsqrt(jnp.mean(ht * ht, axis=0, keepdims=True) + EPS) * gain_ref[rows, :]
            out_ref[rows, :] = (_sigmoid(op_ref[rows, :].astype(F32)) * y).astype(BF16)


def _mlstm(reverse, q_t, k_x, tr_x, gates_t, bias_col, bsz, seq, fwd_extra=None):
    nc = seq // ML_CHUNK
    if reverse:
        rowblk = lambda b, c: b * nc + (nc - 1 - c)
        tblk = lambda c: nc - 1 - c
    else:
        rowblk = lambda b, c: b * nc + c
        tblk = lambda c: c
    in_specs = [
        pl.BlockSpec((ML_QK_WIDTH, ML_CHUNK), lambda b, c: (0, rowblk(b, c))),
        pl.BlockSpec((ML_CHUNK, ML_QK_WIDTH), lambda b, c: (rowblk(b, c), 0)),
        pl.BlockSpec((ML_WIDTH, ML_CHUNK), lambda b, c: (TR_MV, rowblk(b, c))),
        pl.BlockSpec((1, N_GATE_COLS, ML_CHUNK), lambda b, c: (b, 0, tblk(c))),
        pl.BlockSpec((N_GATE_COLS, 1), lambda b, c: (0, 0)),
    ]
    args = [q_t, k_x, tr_x, gates_t, bias_col]
    if reverse:
        out_dtype = F32
    else:
        k_m, tr_m, gates_m, gates_m_t, bias_row, hb, gain_b = fwd_extra
        in_specs += [
            pl.BlockSpec((1, N_META, ML_QK_WIDTH), lambda b, c: (b, 0, 0)),
            pl.BlockSpec((ML_WIDTH, N_META), lambda b, c: (TR_MV, 0)),
            pl.BlockSpec((N_META, LANES), lambda b, c: (0, 0)),
            pl.BlockSpec((N_GATE_COLS, N_META), lambda b, c: (0, 0)),
            pl.BlockSpec((1, LANES), lambda b, c: (0, 0)),
            pl.BlockSpec((ML_WIDTH, ML_CHUNK), lambda b, c: (0, rowblk(b, c))),
            pl.BlockSpec((ML_WIDTH, ML_CHUNK), lambda b, c: (TR_MO, rowblk(b, c))),
            pl.BlockSpec((ML_WIDTH, ML_CHUNK), lambda b, c: (0, 0)),
        ]
        args += [k_m, tr_m, gates_m, gates_m_t, bias_row, hb, tr_x, gain_b]
        out_dtype = BF16
    return pl.pallas_call(
        functools.partial(_mlstm_kernel, reverse),
        grid=(bsz, nc),
        in_specs=in_specs,
        out_specs=pl.BlockSpec((ML_WIDTH, ML_CHUNK), lambda b, c: (0, rowblk(b, c))),
        out_shape=jax.ShapeDtypeStruct((ML_WIDTH, bsz * seq), out_dtype),
        scratch_shapes=[
            pltpu.VMEM((ML_HEADS, ML_V_DIM, ML_QK_DIM), F32),
            pltpu.VMEM((ML_HEADS, 1, ML_QK_DIM), F32),
            pltpu.VMEM((ML_HEADS, SUBLANES, LANES), F32),
        ],
        compiler_params=_cparams(("parallel", "arbitrary")),
        name="mlstm_bwd" if reverse else "mlstm_fwd",
    )(*args)


def _outproj_kernel(a_ref, mt_ref, x_ref, wa_ref, wm_ref, g_ref, h_ref, u_ref):
    h = _dot(a_ref[...], wa_ref[...]) + _dot_tn(mt_ref[...], wm_ref[...]) + x_ref[...]
    h_ref[...] = h
    u = h * lax.rsqrt(jnp.mean(h * h, axis=-1, keepdims=True) + EPS) * g_ref[...]
    u_ref[...] = u.astype(BF16)


def _outproj(attn, ml_t, x_rows, w_a, w_m, norm_g):
    m = x_rows.shape[0]
    return pl.pallas_call(
        _outproj_kernel,
        grid=(m // OUT_TM,),
        in_specs=[
            pl.BlockSpec((OUT_TM, DA_WIDTH), lambda i: (i, 0)),
            pl.BlockSpec((ML_WIDTH, OUT_TM), lambda i: (0, i)),
            pl.BlockSpec((OUT_TM, D_MODEL), lambda i: (i, 0)),
            pl.BlockSpec((DA_WIDTH, D_MODEL), lambda i: (0, 0)),
            pl.BlockSpec((ML_WIDTH, D_MODEL), lambda i: (0, 0)),
            pl.BlockSpec((1, D_MODEL), lambda i: (0, 0)),
        ],
        out_specs=[
            pl.BlockSpec((OUT_TM, D_MODEL), lambda i: (i, 0)),
            pl.BlockSpec((OUT_TM, D_MODEL), lambda i: (i, 0)),
        ],
        out_shape=[
            jax.ShapeDtypeStruct((m, D_MODEL), F32),
            jax.ShapeDtypeStruct((m, D_MODEL), BF16),
        ],
        compiler_params=_cparams(("parallel",)),
        name="outproj",
    )(attn, ml_t, x_rows, w_a, w_m, norm_g)


def _ffn_kernel(u_ref, h_ref, wg_ref, wu_ref, wd_ref, gf_ref, o_ref, acc_ref):
    j = pl.program_id(1)

    @pl.when(j == 0)
    def _():
        acc_ref[...] = h_ref[...]

    u = u_ref[...]
    g = _dot(u, wg_ref[...])
    up = _dot(u, wu_ref[...])
    ff = (g * _sigmoid(g) * up).astype(BF16)
    acc_ref[...] += _dot(ff, wd_ref[...])

    @pl.when(j == pl.num_programs(1) - 1)
    def _():
        y = acc_ref[...]
        o_ref[...] = y * lax.rsqrt(jnp.mean(y * y, axis=-1, keepdims=True) + EPS) * gf_ref[...]


def _ffn(u, h, w_gate, w_up, w_down, norm_g):
    m = u.shape[0]
    return pl.pallas_call(
        _ffn_kernel,
        grid=(m // FFN_TM, D_FF // FFN_TF),
        in_specs=[
            pl.BlockSpec((FFN_TM, D_MODEL), lambda i, j: (i, 0)),
            pl.BlockSpec((FFN_TM, D_MODEL), lambda i, j: (i, 0)),
            pl.BlockSpec((D_MODEL, FFN_TF), lambda i, j: (0, j)),
            pl.BlockSpec((D_MODEL, FFN_TF), lambda i, j: (0, j)),
            pl.BlockSpec((FFN_TF, D_MODEL), lambda i, j: (j, 0)),
            pl.BlockSpec((1, D_MODEL), lambda i, j: (0, 0)),
        ],
        out_specs=pl.BlockSpec((FFN_TM, D_MODEL), lambda i, j: (i, 0)),
        out_shape=jax.ShapeDtypeStruct((m, D_MODEL), F32),
        scratch_shapes=[pltpu.VMEM((FFN_TM, D_MODEL), F32)],
        compiler_params=_cparams(("parallel", "arbitrary")),
        name="ffn",
    )(u, h, w_gate, w_up, w_down, norm_g)


def _rope_tables(pos0, n):
    pos = jnp.arange(pos0, pos0 + n, dtype=F32)
    inv = ROPE_THETA ** (-jnp.arange(0, DA_ROT_DIM, 2, dtype=F32) / DA_ROT_DIM)
    ang = pos[:, None] * inv[None, :]
    cos8, sin8 = jnp.cos(ang), jnp.sin(ang)
    half = DA_ROT_DIM // 2
    pad = jnp.zeros((n, DA_QK_DIM - DA_ROT_DIM), F32)
    zero8 = jnp.zeros((n, half), F32)
    cos_c = jnp.concatenate([cos8, cos8, pad + 1.0], axis=1)
    sa_c = jnp.concatenate([-sin8, zero8, pad], axis=1)
    sb_c = jnp.concatenate([zero8, sin8, pad], axis=1)
    two = lambda t: jnp.concatenate([t, t], axis=1)
    return two(cos_c), two(sa_c), two(sb_c)


def _pad_lanes(a, width=LANES):
    return jnp.pad(a, ((0, 0), (0, width - a.shape[1])))


def _split_w_in(w):
    aq, ak, av, mq, mk, mv, mo, mg = jnp.split(
        w, [1024, 2048, 3072, 3584, 4096, 5120, 6144], axis=1)
    w_nat = jnp.concatenate([aq, ak, mq, mk], axis=1).astype(BF16)
    w_tr_t = jnp.concatenate([av, mv, mo], axis=1).T.astype(BF16)
    return w_nat, w_tr_t, _pad_lanes(mg).astype(BF16)


def kernel(x, meta_tokens, norm_mix, w_in, da_lambda_q1, da_lambda_k1, da_lambda_q2, da_lambda_k2,
           da_head_norm, ml_conv_w, ml_conv_b, ml_gate_bias, ml_head_norm, w_out, norm_ffn,
           w_gate, w_up, w_down, norm_final):
    bsz, seq, _ = x.shape
    x_rows = x.reshape(bsz * seq, D_MODEL)

    w_nat, w_tr_t, w_gates = _split_w_in(w_in[0])
    norm_mix0 = norm_mix[0][None, :]

    tabs_x = _rope_tables(N_META, seq)
    tabs_m = _rope_tables(0, N_META)
    nat_x, tr_x, gates_x = _inproj(x_rows, norm_mix0, w_nat, w_tr_t, w_gates, *tabs_x, tm=IN_TM)
    nat_m, tr_m, gates_m = _inproj(meta_tokens, norm_mix0, w_nat, w_tr_t, w_gates, *tabs_m,
                                   tm=N_META)

    conv_w = jnp.pad(ml_conv_w[0], ((0, SUBLANES - CONV_W), (0, 0)))
    q_t, k_x, k_m = _conv(nat_x, nat_m, conv_w, ml_conv_b[0][None, :], bsz, seq)

    lam_params = jnp.pad(
        jnp.stack([da_lambda_q1[0], da_lambda_k1[0], da_lambda_q2[0], da_lambda_k2[0]]),
        ((0, SUBLANES - 4), (0, LANES - DA_QK_DIM)))
    attn = _attention(lam_params, nat_x, nat_m, tr_x, tr_m,
                      da_head_norm[0].reshape(1, DA_WIDTH), bsz, seq)

    bias = ml_gate_bias[0].reshape(N_GATE_COLS)
    bias_row = _pad_lanes(bias[None, :])
    bias_col = bias[:, None]
    gates_t = jnp.transpose(gates_x[:, :N_GATE_COLS].reshape(bsz, seq, N_GATE_COLS), (0, 2, 1))
    gates_m_t = jnp.transpose(gates_m[:, :N_GATE_COLS])
    gain_b = jnp.broadcast_to(ml_head_norm[0].reshape(ML_WIDTH, 1), (ML_WIDTH, ML_CHUNK))
    h_bwd = _mlstm(True, q_t, k_x, tr_x, gates_t, bias_col, bsz, seq)
    ml_t = _mlstm(False, q_t, k_x, tr_x, gates_t, bias_col, bsz, seq,
                  fwd_extra=(k_m, tr_m, gates_m, gates_m_t, bias_row, h_bwd, gain_b))

    w_out0 = w_out[0].astype(BF16)
    h1, u2 = _outproj(attn, ml_t, x_rows, w_out0[:DA_WIDTH], w_out0[DA_WIDTH:],
                      norm_ffn[0][None, :])

    out = _ffn(u2, h1, w_gate[0].astype(BF16), w_up[0].astype(BF16), w_down[0].astype(BF16),
               norm_final[None, :])
    return out.reshape(bsz, seq, D_MODEL)
```

```python
import functools

import jax
import jax.numpy as jnp
from jax import lax
from jax.experimental import pallas as pl
from jax.experimental.pallas import tpu as pltpu

F32 = jnp.float32
BF16 = jnp.bfloat16

D_MODEL = 2048
N_META = 16
EPS = 1e-6
ROPE_THETA = 500000.0
NEG = -1e30

DA_QK_DIM = 64
DA_V_DIM = 128
DA_HEADS = 8
DA_ROT_DIM = 16
DA_WIDTH = DA_HEADS * DA_V_DIM
LAM_INIT = 0.8 - 0.6 * 1.0
LOG2_E = 1.4426950408889634

ML_HEADS = 4
ML_V_DIM = 256
ML_QK_DIM = 128
ML_WIDTH = ML_HEADS * ML_V_DIM
ML_QK_WIDTH = ML_HEADS * ML_QK_DIM
CONV_W = 5
N_GATES = 4
N_GATE_COLS = N_GATES * ML_HEADS

D_FF = 5632

IN_TN = 1024
NAT_AQ, NAT_AK, NAT_MQK = 0, 1, 2
TR_AV, TR_MV, TR_MO = 0, 1, 2
N_NAT = 3
N_TR = 3

LANES = 128
SUBLANES = 8
VMEM_LIMIT = 56 * 1024 * 1024

IN_TM = 512
ATT_TQ = 512
ATT_KB = 512
ML_CHUNK = 256
OUT_TM = 256
FFN_TM = 512
FFN_TF = 512
CONV_ROWS = 512


def _cparams(sem):
    return pltpu.CompilerParams(dimension_semantics=sem, vmem_limit_bytes=VMEM_LIMIT)


def _dot(a, b):
    return jnp.dot(a, b, preferred_element_type=F32)


def _dot_nt(a, b):
    return lax.dot_general(a, b, (((1,), (1,)), ((), ())), preferred_element_type=F32)


def _dot_tn(a, b):
    return lax.dot_general(a, b, (((0,), (0,)), ((), ())), preferred_element_type=F32)


def _bf16_pieces(x):
    hi = x.astype(BF16)
    rest = x - hi.astype(F32)
    mid = rest.astype(BF16)
    return hi, mid, (rest - mid.astype(F32)).astype(BF16)


def _cumsum_rows(x, visible):
    r = x.shape[0]
    ones = jnp.where(visible, 1.0, 0.0).astype(BF16)
    parts = _dot(jnp.concatenate(_bf16_pieces(x), axis=0), ones)
    return parts[0:r] + parts[r:2 * r] + parts[2 * r:3 * r]


def _cumsum_cols(x, visible):
    c = x.shape[1]
    ones = jnp.where(visible, 1.0, 0.0).astype(BF16)
    parts = _dot(ones, jnp.concatenate(_bf16_pieces(x), axis=1))
    return parts[:, 0:c] + parts[:, c:2 * c] + parts[:, 2 * c:3 * c]


def _sigmoid(x):
    return 1.0 / (1.0 + jnp.exp(-x))


def _log_sigmoid(x):
    return jnp.minimum(x, 0.0) - jnp.log(1.0 + jnp.exp(-jnp.abs(x)))


def _inproj_kernel(x_ref, g_ref, wn_ref, wt_ref, wg_ref, cos_ref, sa_ref, sb_ref,
                   nat_ref, tr_ref, gate_ref, u_ref):
    j = pl.program_id(1)

    @pl.when(j == 0)
    def _():
        x = x_ref[...]
        ms = jnp.mean(x * x, axis=-1, keepdims=True)
        u = (x * lax.rsqrt(ms + EPS) * g_ref[...]).astype(BF16)
        u_ref[...] = u
        gate_ref[...] = _dot(u, wg_ref[...])

    @pl.when(j <= NAT_AK)
    def _():
        acc = _dot(u_ref[...], wn_ref[j])
        scale = jnp.where(j == NAT_AQ, DA_QK_DIM ** -0.5 * LOG2_E, 1.0).astype(F32)
        cos = cos_ref[...] * scale
        sa = sa_ref[...] * scale
        sb = sb_ref[...] * scale
        for h in range(DA_HEADS):
            t = acc[:, h * LANES:(h + 1) * LANES]
            r = (t * cos + pltpu.roll(t, LANES - DA_ROT_DIM // 2, 1) * sa
                 + pltpu.roll(t, DA_ROT_DIM // 2, 1) * sb)
            nat_ref[:, h * LANES:(h + 1) * LANES] = r.astype(BF16)

    @pl.when(j == NAT_MQK)
    def _():
        nat_ref[...] = _dot(u_ref[...], wn_ref[NAT_MQK]).astype(BF16)

    @pl.when(j >= N_NAT)
    def _():
        tr_ref[...] = _dot_nt(wt_ref[j - N_NAT], u_ref[...]).astype(BF16)


def _inproj(rows, norm_g, w_nat, w_tr_t, w_gate, cos_t, sa_t, sb_t, tm):
    m = rows.shape[0]
    tiles_per_seq = cos_t.shape[0] // tm
    tab_spec = pl.BlockSpec((tm, LANES), lambda i, j: (i % tiles_per_seq, 0))
    nat_j = lambda j: jnp.minimum(j, N_NAT - 1)
    tr_j = lambda j: jnp.maximum(j - N_NAT, 0)
    return pl.pallas_call(
        _inproj_kernel,
        grid=(m // tm, N_NAT + N_TR),
        in_specs=[
            pl.BlockSpec((tm, D_MODEL), lambda i, j: (i, 0)),
            pl.BlockSpec((1, D_MODEL), lambda i, j: (0, 0)),
            pl.BlockSpec((N_NAT, D_MODEL, IN_TN), lambda i, j: (0, 0, 0)),
            pl.BlockSpec((N_TR, IN_TN, D_MODEL), lambda i, j: (0, 0, 0)),
            pl.BlockSpec((D_MODEL, LANES), lambda i, j: (0, 0)),
            tab_spec, tab_spec, tab_spec,
        ],
        out_specs=[
            pl.BlockSpec((tm, IN_TN), lambda i, j: (i, nat_j(j))),
            pl.BlockSpec((IN_TN, tm), lambda i, j: (tr_j(j), i)),
            pl.BlockSpec((tm, LANES), lambda i, j: (i, 0)),
        ],
        out_shape=[
            jax.ShapeDtypeStruct((m, N_NAT * IN_TN), BF16),
            jax.ShapeDtypeStruct((N_TR * IN_TN, m), BF16),
            jax.ShapeDtypeStruct((m, LANES), F32),
        ],
        scratch_shapes=[pltpu.VMEM((tm, D_MODEL), BF16)],
        compiler_params=_cparams(("parallel", "arbitrary")),
        name="inproj",
    )(rows, norm_g, w_nat, w_tr_t, w_gate, cos_t, sa_t, sb_t)


_CONV_PAD = SUBLANES


def _conv_kernel(x_ref, m_ref, w_ref, b_ref, qt_ref, k_ref, km_ref, s_ref):
    seq = x_ref.shape[0]
    j = pl.program_id(1)
    zeros = jnp.zeros((_CONV_PAD, LANES), F32)
    s_ref[0:_CONV_PAD, :] = zeros
    s_ref[_CONV_PAD:_CONV_PAD + N_META, :] = m_ref[...].astype(F32)
    s_ref[_CONV_PAD + N_META:_CONV_PAD + N_META + seq, :] = x_ref[...].astype(F32)
    s_ref[_CONV_PAD + N_META + seq:, :] = zeros
    w = w_ref[...]
    bias = b_ref[...]

    def conv(pos, n):
        acc = jnp.zeros((n, LANES), F32) + bias
        for t in range(CONV_W):
            start = _CONV_PAD + pos + t - CONV_W // 2
            acc = acc + s_ref[start:start + n, :] * w[t:t + 1, :]
        return acc * _sigmoid(acc)

    @pl.when(j < ML_HEADS)
    def _():
        for c in range(seq // CONV_ROWS):
            y = conv(N_META + c * CONV_ROWS, CONV_ROWS) * (ML_QK_DIM ** -0.5)
            qt_ref[:, c * CONV_ROWS:(c + 1) * CONV_ROWS] = y.T.astype(BF16)

    @pl.when(j >= ML_HEADS)
    def _():
        km_ref[0] = conv(0, N_META).astype(BF16)
        for c in range(seq // CONV_ROWS):
            k_ref[c * CONV_ROWS:(c + 1) * CONV_ROWS, :] = conv(
                N_META + c * CONV_ROWS, CONV_ROWS).astype(BF16)


def _conv(nat_x, nat_m, conv_w, conv_b, bsz, seq):
    col0 = NAT_MQK * (IN_TN // LANES)
    q_j = lambda j: jnp.minimum(j, ML_HEADS - 1)
    k_j = lambda j: jnp.maximum(j - ML_HEADS, 0)
    return pl.pallas_call(
        _conv_kernel,
        grid=(bsz, 2 * ML_HEADS),
        in_specs=[
            pl.BlockSpec((seq, LANES), lambda b, j: (b, col0 + j)),
            pl.BlockSpec((N_META, LANES), lambda b, j: (0, col0 + j)),
            pl.BlockSpec((SUBLANES, LANES), lambda b, j: (0, j)),
            pl.BlockSpec((1, LANES), lambda b, j: (0, j)),
        ],
        out_specs=[
            pl.BlockSpec((ML_QK_DIM, seq), lambda b, j: (q_j(j), b)),
            pl.BlockSpec((seq, ML_QK_DIM), lambda b, j: (b, k_j(j))),
            pl.BlockSpec((1, N_META, ML_QK_DIM), lambda b, j: (b, 0, k_j(j))),
        ],
        out_shape=[
            jax.ShapeDtypeStruct((ML_QK_WIDTH, bsz * seq), BF16),
            jax.ShapeDtypeStruct((bsz * seq, ML_QK_WIDTH), BF16),
            jax.ShapeDtypeStruct((bsz, N_META, ML_QK_WIDTH), BF16),
        ],
        scratch_shapes=[pltpu.VMEM((seq + N_META + 2 * _CONV_PAD, LANES), F32)],
        compiler_params=_cparams(("parallel", "arbitrary")),
        name="mlstm_conv",
    )(nat_x, nat_m, conv_w, conv_b)


def _attn_kernel(lam_ref, q_ref, k_ref, vt_ref, km_ref, vmt_ref, gain_ref, o_ref,
                 s_ref, sm_ref, acc_ref, o_t_ref):
    tq = ATT_TQ
    nq = q_ref.shape[0] // tq
    nkb = k_ref.shape[0] // ATT_KB
    lp = lam_ref[...]
    lam = (jnp.exp(jnp.sum(lp[0:1] * lp[1:2], axis=1, keepdims=True))
           - jnp.exp(jnp.sum(lp[2:3] * lp[3:4], axis=1, keepdims=True)) + LAM_INIT)
    lane = lax.broadcasted_iota(jnp.int32, (tq, LANES), 1)
    comps = range(2)

    def q_rows(qt):
        return pl.ds(pl.multiple_of(qt * tq, tq), tq)

    def masked_q(qt):
        q = q_ref[q_rows(qt), :]
        zero = jnp.zeros_like(q)
        return (jnp.where(lane < DA_QK_DIM, q, zero), jnp.where(lane >= DA_QK_DIM, q, zero))

    def fold(a):
        return a.reshape(a.shape[0] // SUBLANES, SUBLANES, tq)

    def scores(qz, kb, slot):
        k = k_ref[kb * ATT_KB:(kb + 1) * ATT_KB, :]
        bmax = []
        for c in comps:
            s = _dot_nt(k, qz[c])
            s_ref[slot, c] = s
            bm = jnp.max(fold(s), axis=0)
            if kb == 0:
                sm = _dot_nt(km_ref[...], qz[c])
                sm_ref[c] = sm
                bm = jnp.maximum(bm, jnp.max(fold(sm), axis=0))
            bmax.append(bm)
        return tuple(bmax)

    def consume(kb, slot, bmax, m, l):
        vt = vt_ref[:, kb * ATT_KB:(kb + 1) * ATT_KB]
        m_out, l_out = [], []
        for c in comps:
            bm = jnp.max(bmax[c], axis=0, keepdims=True)
            if kb == 0:
                m_new = bm
                pm = jnp.exp2(sm_ref[c] - m_new)
                p = jnp.exp2(s_ref[slot, c] - m_new)
                l_out.append(jnp.sum(fold(p), axis=0) + jnp.sum(fold(pm), axis=0))
                acc_ref[c] = _dot(vt, p.astype(BF16)) + _dot(vmt_ref[...], pm.astype(BF16))
            else:
                m_new = jnp.maximum(m[c], bm)
                alpha = jnp.exp2(m[c] - m_new)
                p = jnp.exp2(s_ref[slot, c] - m_new)
                l_out.append(alpha * l[c] + jnp.sum(fold(p), axis=0))
                acc_ref[c] = alpha * acc_ref[c] + _dot(vt, p.astype(BF16))
            m_out.append(m_new)
        return tuple(m_out), tuple(l_out)

    def finalize(qt):
        o = o_t_ref[...].T
        y = (o * lax.rsqrt(jnp.mean(o * o, axis=1, keepdims=True) + EPS)
             * gain_ref[...] * (1.0 - LAM_INIT))
        o_ref[q_rows(qt), :] = y.astype(BF16)

    def tile(qt, bmax):
        finalize(jnp.maximum(qt - 1, 0))
        qz = masked_q(qt)
        m = l = None
        for kb in range(nkb):
            if kb + 1 < nkb:
                bmax_next = scores(qz, kb + 1, (kb + 1) % 2)
            else:
                bmax_next = scores(masked_q(jnp.minimum(qt + 1, nq - 1)), 0, 0)
            m, l = consume(kb, kb % 2, bmax, m, l)
            bmax = bmax_next
        l1 = jnp.sum(l[0], axis=0, keepdims=True)
        l2 = jnp.sum(l[1], axis=0, keepdims=True)
        o_t_ref[...] = acc_ref[0] / l1 - lam * (acc_ref[1] / l2)
        return bmax

    o_t_ref[...] = jnp.zeros_like(o_t_ref)
    lax.fori_loop(0, nq, tile, scores(masked_q(0), 0, 0))
    finalize(nq - 1)


def _attention(lam_params, nat_x, nat_m, tr_x, tr_m, gain, bsz, seq):
    hb = IN_TN // LANES
    return pl.pallas_call(
        _attn_kernel,
        grid=(bsz, DA_HEADS),
        in_specs=[
            pl.BlockSpec((SUBLANES, LANES), lambda b, h: (0, 0)),
            pl.BlockSpec((seq, LANES), lambda b, h: (b, NAT_AQ * hb + h)),
            pl.BlockSpec((seq, LANES), lambda b, h: (b, NAT_AK * hb + h)),
            pl.BlockSpec((DA_V_DIM, seq), lambda b, h: (TR_AV * DA_HEADS + h, b)),
            pl.BlockSpec((N_META, LANES), lambda b, h: (0, NAT_AK * hb + h)),
            pl.BlockSpec((DA_V_DIM, N_META), lambda b, h: (TR_AV * DA_HEADS + h, 0)),
            pl.BlockSpec((1, LANES), lambda b, h: (0, h)),
        ],
        out_specs=pl.BlockSpec((seq, LANES), lambda b, h: (b, h)),
        out_shape=jax.ShapeDtypeStruct((bsz * seq, DA_WIDTH), BF16),
        scratch_shapes=[
            pltpu.VMEM((2, 2, ATT_KB, ATT_TQ), F32),
            pltpu.VMEM((2, N_META, ATT_TQ), F32),
            pltpu.VMEM((2, DA_V_DIM, ATT_TQ), F32),
            pltpu.VMEM((DA_V_DIM, ATT_TQ), F32),
        ],
        compiler_params=_cparams(("parallel", "parallel")),
        name="diff_attention",
    )(lam_params, nat_x, nat_x, tr_x, nat_m, tr_m, gain)


def _ml_state_update(h, k, vt, a_col, b_end, g_row, ct_ref, n_ref, m_ref):
    m_prev = m_ref[h][0:1, 0:1]
    m_new = jnp.maximum(b_end + m_prev, jnp.max(g_row, axis=1, keepdims=True))
    decay = jnp.exp(b_end + m_prev - m_new)
    kw = (k.astype(F32) * jnp.exp(b_end + a_col - m_new)).astype(BF16)
    ct_ref[h] = decay * ct_ref[h] + _dot(vt, kw)
    n_ref[h] = decay * n_ref[h] + jnp.sum(kw.astype(F32), axis=0, keepdims=True)
    m_ref[h] = jnp.broadcast_to(m_new, (SUBLANES, LANES))


def _mlstm_kernel(reverse, *refs):
    if reverse:
        (qt_ref, k_ref, vt_ref, gt_ref, bc_ref,
         out_ref, ct_ref, n_ref, m_ref) = refs
    else:
        (qt_ref, k_ref, vt_ref, gt_ref, bc_ref,
         km_ref, vmt_ref, gm_ref, gmt_ref, br_ref, hb_ref, op_ref, gain_ref,
         out_ref, ct_ref, n_ref, m_ref) = refs
    c = pl.program_id(1)
    chunk = k_ref.shape[0]
    gate0 = 2 * ML_HEADS if reverse else 0

    @pl.when(c == 0)
    def _():
        ct_ref[...] = jnp.zeros_like(ct_ref)
        n_ref[...] = jnp.zeros_like(n_ref)
        m_ref[...] = jnp.zeros_like(m_ref)
        if not reverse:
            gm = gm_ref[...] + br_ref[...]
            gmt = gmt_ref[...] + bc_ref[...]
            r = lax.broadcasted_iota(jnp.int32, (N_META, N_META), 0)
            s = lax.broadcasted_iota(jnp.int32, (N_META, N_META), 1)
            bcm = _cumsum_cols(_log_sigmoid(gm), s <= r)
            brm = _cumsum_rows(_log_sigmoid(gmt), r <= s)
            for h in range(ML_HEADS):
                gi, gf = gate0 + h, gate0 + ML_HEADS + h
                b_end = brm[gf:gf + 1, N_META - 1:N_META]
                _ml_state_update(
                    h, km_ref[0][:, h * ML_QK_DIM:(h + 1) * ML_QK_DIM],
                    vmt_ref[h * ML_V_DIM:(h + 1) * ML_V_DIM, :],
                    gm[:, gi:gi + 1] - bcm[:, gf:gf + 1], b_end,
                    b_end - brm[gf:gf + 1, :] + gmt[gi:gi + 1, :],
                    ct_ref, n_ref, m_ref)

    grow = gt_ref[0] + bc_ref[...]
    row = lax.broadcasted_iota(jnp.int32, (chunk, chunk), 0)
    col = lax.broadcasted_iota(jnp.int32, (chunk, chunk), 1)
    mask = (row >= col) if reverse else (row <= col)
    brow_all = _cumsum_rows(_log_sigmoid(grow), mask)
    a_rows = grow - pltpu.roll(brow_all, N_GATE_COLS - ML_HEADS, 0)
    a_cols = jnp.concatenate(
        [a_rows, jnp.zeros((LANES - N_GATE_COLS, chunk), F32)], axis=0).T
    end = 0 if reverse else chunk - 1

    for h in range(ML_HEADS):
        gi, gf = gate0 + h, gate0 + ML_HEADS + h
        k = k_ref[:, h * ML_QK_DIM:(h + 1) * ML_QK_DIM]
        qt = qt_ref[h * ML_QK_DIM:(h + 1) * ML_QK_DIM, :]
        vt = vt_ref[h * ML_V_DIM:(h + 1) * ML_V_DIM, :]
        a_col = a_cols[:, gi:gi + 1]
        b_row = brow_all[gf:gf + 1, :]
        li_row = grow[gi:gi + 1, :]
        b_end = b_row[:, end:end + 1]
        m_prev = m_ref[h][0:1, 0:1]

        d = jnp.where(mask, b_row + a_col, NEG)
        m_inter = b_row + m_prev
        m_t = jnp.maximum(m_inter, jnp.max(d, axis=0, keepdims=True))
        w_inter = jnp.exp(m_inter - m_t)
        sg = _dot(k, qt) * jnp.exp(d - m_t)
        num = w_inter * _dot(ct_ref[h].astype(BF16), qt) + _dot(vt, sg.astype(BF16))
        n16 = jnp.broadcast_to(n_ref[h], (2 * SUBLANES, ML_QK_DIM)).astype(BF16)
        den = w_inter * _dot(n16, qt)[0:1, :] + jnp.sum(sg, axis=0, keepdims=True)
        hout = num / jnp.maximum(jnp.abs(den), jnp.exp(-m_t))

        _ml_state_update(h, k, vt, a_col, b_end, b_end - b_row + li_row, ct_ref, n_ref, m_ref)

        rows = slice(h * ML_V_DIM, (h + 1) * ML_V_DIM)
        if reverse:
            out_ref[rows, :] = hout
        else:
            ht = hout + hb_ref[rows, :]
            y = ht * lax.rsqrt(jnp.mean(ht * ht, axis=0, keepdims=True) + EPS) * gain_ref[rows, :]
            out_ref[rows, :] = (_sigmoid(op_ref[rows, :].astype(F32)) * y).astype(BF16)


def _mlstm(reverse, q_t, k_x, tr_x, gates_t, bias_col, bsz, seq, fwd_extra=None):
    nc = seq // ML_CHUNK
    if reverse:
        rowblk = lambda b, c: b * nc + (nc - 1 - c)
        tblk = lambda c: nc - 1 - c
    else:
        rowblk = lambda b, c: b * nc + c
        tblk = lambda c: c
    in_specs = [
        pl.BlockSpec((ML_QK_WIDTH, ML_CHUNK), lambda b, c: (0, rowblk(b, c))),
        pl.BlockSpec((ML_CHUNK, ML_QK_WIDTH), lambda b, c: (rowblk(b, c), 0)),
        pl.BlockSpec((ML_WIDTH, ML_CHUNK), lambda b, c: (TR_MV, rowblk(b, c))),
        pl.BlockSpec((1, N_GATE_COLS, ML_CHUNK), lambda b, c: (b, 0, tblk(c))),
        pl.BlockSpec((N_GATE_COLS, 1), lambda b, c: (0, 0)),
    ]
    args = [q_t, k_x, tr_x, gates_t, bias_col]
    if reverse:
        out_dtype = F32
    else:
        k_m, tr_m, gates_m, gates_m_t, bias_row, hb, gain_b = fwd_extra
        in_specs += [
            pl.BlockSpec((1, N_META, ML_QK_WIDTH), lambda b, c: (b, 0, 0)),
            pl.BlockSpec((ML_WIDTH, N_META), lambda b, c: (TR_MV, 0)),
            pl.BlockSpec((N_META, LANES), lambda b, c: (0, 0)),
            pl.BlockSpec((N_GATE_COLS, N_META), lambda b, c: (0, 0)),
            pl.BlockSpec((1, LANES), lambda b, c: (0, 0)),
            pl.BlockSpec((ML_WIDTH, ML_CHUNK), lambda b, c: (0, rowblk(b, c))),
            pl.BlockSpec((ML_WIDTH, ML_CHUNK), lambda b, c: (TR_MO, rowblk(b, c))),
            pl.BlockSpec((ML_WIDTH, ML_CHUNK), lambda b, c: (0, 0)),
        ]
        args += [k_m, tr_m, gates_m, gates_m_t, bias_row, hb, tr_x, gain_b]
        out_dtype = BF16
    return pl.pallas_call(
        functools.partial(_mlstm_kernel, reverse),
        grid=(bsz, nc),
        in_specs=in_specs,
        out_specs=pl.BlockSpec((ML_WIDTH, ML_CHUNK), lambda b, c: (0, rowblk(b, c))),
        out_shape=jax.ShapeDtypeStruct((ML_WIDTH, bsz * seq), out_dtype),
        scratch_shapes=[
            pltpu.VMEM((ML_HEADS, ML_V_DIM, ML_QK_DIM), F32),
            pltpu.VMEM((ML_HEADS, 1, ML_QK_DIM), F32),
            pltpu.VMEM((ML_HEADS, SUBLANES, LANES), F32),
        ],
        compiler_params=_cparams(("parallel", "arbitrary")),
        name="mlstm_bwd" if reverse else "mlstm_fwd",
    )(*args)


def _mix_ffn_kernel(a_ref, mt_ref, x_ref, wa_ref, wm_ref, g2_ref, wgu_ref, wd_ref, gf_ref,
                    o_ref, u_ref):
    j = pl.program_id(1)

    @pl.when(j == 0)
    def _():
        h = _dot(a_ref[...], wa_ref[...]) + _dot_tn(mt_ref[...], wm_ref[...]) + x_ref[...]
        o_ref[...] = h
        u = h * lax.rsqrt(jnp.mean(h * h, axis=-1, keepdims=True) + EPS) * g2_ref[...]
        u_ref[...] = u.astype(BF16)

    gu = _dot(u_ref[...], wgu_ref[0])
    g = gu[:, :FFN_TF]
    ff = (g * _sigmoid(g) * gu[:, FFN_TF:]).astype(BF16)
    o_ref[...] += _dot(ff, wd_ref[...])

    @pl.when(j == pl.num_programs(1) - 1)
    def _():
        y = o_ref[...]
        o_ref[...] = y * lax.rsqrt(jnp.mean(y * y, axis=-1, keepdims=True) + EPS) * gf_ref[...]


def _mix_ffn(attn, ml_t, x_rows, w_a, w_m, norm_ffn_g, w_gu, w_down, norm_final_g):
    m = x_rows.shape[0]
    return pl.pallas_call(
        _mix_ffn_kernel,
        grid=(m // FFN_TM, D_FF // FFN_TF),
        in_specs=[
            pl.BlockSpec((FFN_TM, DA_WIDTH), lambda i, j: (i, 0)),
            pl.BlockSpec((ML_WIDTH, FFN_TM), lambda i, j: (0, i)),
            pl.BlockSpec((FFN_TM, D_MODEL), lambda i, j: (i, 0)),
            pl.BlockSpec((DA_WIDTH, D_MODEL), lambda i, j: (0, 0)),
            pl.BlockSpec((ML_WIDTH, D_MODEL), lambda i, j: (0, 0)),
            pl.BlockSpec((1, D_MODEL), lambda i, j: (0, 0)),
            pl.BlockSpec((1, D_MODEL, 2 * FFN_TF), lambda i, j: (j, 0, 0)),
            pl.BlockSpec((FFN_TF, D_MODEL), lambda i, j: (j, 0)),
            pl.BlockSpec((1, D_MODEL), lambda i, j: (0, 0)),
        ],
        out_specs=pl.BlockSpec((FFN_TM, D_MODEL), lambda i, j: (i, 0)),
        out_shape=jax.ShapeDtypeStruct((m, D_MODEL), F32),
        scratch_shapes=[pltpu.VMEM((FFN_TM, D_MODEL), BF16)],
        compiler_params=_cparams(("parallel", "arbitrary")),
        name="mix_ffn",
    )(attn, ml_t, x_rows, w_a, w_m, norm_ffn_g, w_gu, w_down, norm_final_g)


def _rope_tables(pos0, n):
    pos = jnp.arange(pos0, pos0 + n, dtype=F32)
    inv = ROPE_THETA ** (-jnp.arange(0, DA_ROT_DIM, 2, dtype=F32) / DA_ROT_DIM)
    ang = pos[:, None] * inv[None, :]
    cos8, sin8 = jnp.cos(ang), jnp.sin(ang)
    half = DA_ROT_DIM // 2
    pad = jnp.zeros((n, DA_QK_DIM - DA_ROT_DIM), F32)
    zero8 = jnp.zeros((n, half), F32)
    cos_c = jnp.concatenate([cos8, cos8, pad + 1.0], axis=1)
    sa_c = jnp.concatenate([-sin8, zero8, pad], axis=1)
    sb_c = jnp.concatenate([zero8, sin8, pad], axis=1)
    two = lambda t: jnp.concatenate([t, t], axis=1)
    return two(cos_c), two(sa_c), two(sb_c)


def _pad_lanes(a, width=LANES):
    return jnp.pad(a, ((0, 0), (0, width - a.shape[1])))


def _split_w_in(w):
    aq, ak, av, mq, mk, mv, mo, mg = jnp.split(
        w, [1024, 2048, 3072, 3584, 4096, 5120, 6144], axis=1)
    w_nat = jnp.stack([aq, ak, jnp.concatenate([mq, mk], axis=1)]).astype(BF16)
    w_tr_t = jnp.stack([av.T, mv.T, mo.T]).astype(BF16)
    return w_nat, w_tr_t, _pad_lanes(mg).astype(BF16)


def kernel(x, meta_tokens, norm_mix, w_in, da_lambda_q1, da_lambda_k1, da_lambda_q2, da_lambda_k2,
           da_head_norm, ml_conv_w, ml_conv_b, ml_gate_bias, ml_head_norm, w_out, norm_ffn,
           w_gate, w_up, w_down, norm_final):
    bsz, seq, _ = x.shape
    x_rows = x.reshape(bsz * seq, D_MODEL)

    w_nat, w_tr_t, w_gates = _split_w_in(w_in[0])
    norm_mix0 = norm_mix[0][None, :]

    tabs_x = _rope_tables(N_META, seq)
    tabs_m = _rope_tables(0, N_META)
    nat_x, tr_x, gates_x = _inproj(x_rows, norm_mix0, w_nat, w_tr_t, w_gates, *tabs_x, tm=IN_TM)
    nat_m, tr_m, gates_m = _inproj(meta_tokens, norm_mix0, w_nat, w_tr_t, w_gates, *tabs_m,
                                   tm=N_META)

    conv_w = jnp.pad(ml_conv_w[0], ((0, SUBLANES - CONV_W), (0, 0)))
    q_t, k_x, k_m = _conv(nat_x, nat_m, conv_w, ml_conv_b[0][None, :], bsz, seq)

    lam_params = jnp.pad(
        jnp.stack([da_lambda_q1[0], da_lambda_k1[0], da_lambda_q2[0], da_lambda_k2[0]]),
        ((0, SUBLANES - 4), (0, LANES - DA_QK_DIM)))
    attn = _attention(lam_params, nat_x, nat_m, tr_x, tr_m,
                      da_head_norm[0].reshape(1, DA_WIDTH), bsz, seq)

    bias = ml_gate_bias[0].reshape(N_GATE_COLS)
    bias_row = _pad_lanes(bias[None, :])
    bias_col = bias[:, None]
    gates_t = jnp.transpose(gates_x[:, :N_GATE_COLS].reshape(bsz, seq, N_GATE_COLS), (0, 2, 1))
    gates_m_t = jnp.transpose(gates_m[:, :N_GATE_COLS])
    gain_b = jnp.broadcast_to(ml_head_norm[0].reshape(ML_WIDTH, 1), (ML_WIDTH, ML_CHUNK))
    h_bwd = _mlstm(True, q_t, k_x, tr_x, gates_t, bias_col, bsz, seq)
    ml_t = _mlstm(False, q_t, k_x, tr_x, gates_t, bias_col, bsz, seq,
                  fwd_extra=(k_m, tr_m, gates_m, gates_m_t, bias_row, h_bwd, gain_b))

    w_out0 = w_out[0].astype(BF16)
    nf = D_FF // FFN_TF
    w_gu = jnp.concatenate(
        [w_gate[0].reshape(D_MODEL, nf, FFN_TF), w_up[0].reshape(D_MODEL, nf, FFN_TF)],
        axis=2).astype(BF16).transpose(1, 0, 2)
    out = _mix_ffn(attn, ml_t, x_rows, w_out0[:DA_WIDTH], w_out0[DA_WIDTH:], norm_ffn[0][None, :],
                   w_gu, w_down[0].astype(BF16), norm_final[None, :])
    return out.reshape(bsz, seq, D_MODEL)
```

```python
import functools

import jax
import jax.numpy as jnp
from jax import lax
from jax.experimental import pallas as pl
from jax.experimental.pallas import tpu as pltpu

F32 = jnp.float32
BF16 = jnp.bfloat16

D_MODEL = 2048
N_META = 16
EPS = 1e-6
ROPE_THETA = 500000.0
NEG = -1e30

DA_QK_DIM = 64
DA_V_DIM = 128
DA_HEADS = 8
DA_ROT_DIM = 16
DA_WIDTH = DA_HEADS * DA_V_DIM
LAM_INIT = 0.8 - 0.6 * 1.0
LOG2_E = 1.4426950408889634

ML_HEADS = 4
ML_V_DIM = 256
ML_QK_DIM = 128
ML_WIDTH = ML_HEADS * ML_V_DIM
ML_QK_WIDTH = ML_HEADS * ML_QK_DIM
CONV_W = 5
N_GATES = 4
N_GATE_COLS = N_GATES * ML_HEADS

D_FF = 5632

IN_TN = 1024
NAT_AQ, NAT_AK, NAT_MQK = 0, 1, 2
TR_AV, TR_MV, TR_MO = 0, 1, 2
N_NAT = 3
N_TR = 3

LANES = 128
SUBLANES = 8
VMEM_LIMIT = 56 * 1024 * 1024

IN_TM = 512
ATT_TQ = 512
ATT_KB = 512
ML_CHUNK = 256
OUT_TM = 256
FFN_TM = 512
FFN_TF = 512
CONV_ROWS = 512


def _cparams(sem):
    return pltpu.CompilerParams(dimension_semantics=sem, vmem_limit_bytes=VMEM_LIMIT)


def _dot(a, b):
    return jnp.dot(a, b, preferred_element_type=F32)


def _dot_nt(a, b):
    return lax.dot_general(a, b, (((1,), (1,)), ((), ())), preferred_element_type=F32)


def _dot_tn(a, b):
    return lax.dot_general(a, b, (((0,), (0,)), ((), ())), preferred_element_type=F32)


def _bf16_pieces(x):
    hi = x.astype(BF16)
    rest = x - hi.astype(F32)
    mid = rest.astype(BF16)
    return hi, mid, (rest - mid.astype(F32)).astype(BF16)


def _cumsum_rows(x, visible):
    r = x.shape[0]
    ones = jnp.where(visible, 1.0, 0.0).astype(BF16)
    parts = _dot(jnp.concatenate(_bf16_pieces(x), axis=0), ones)
    return parts[0:r] + parts[r:2 * r] + parts[2 * r:3 * r]


def _cumsum_cols(x, visible):
    c = x.shape[1]
    ones = jnp.where(visible, 1.0, 0.0).astype(BF16)
    parts = _dot(ones, jnp.concatenate(_bf16_pieces(x), axis=1))
    return parts[:, 0:c] + parts[:, c:2 * c] + parts[:, 2 * c:3 * c]


def _sigmoid(x):
    return 1.0 / (1.0 + jnp.exp(-x))


def _log_sigmoid(x):
    return jnp.minimum(x, 0.0) - jnp.log(1.0 + jnp.exp(-jnp.abs(x)))


_NAT_COLS = (0, 1024, 3072)
_TR_COLS = (2048, 4096, 5120)


def _inproj_kernel(x_ref, g_ref, w_ref, wg_ref, cos_ref, sa_ref, sb_ref,
                   nat_ref, tr_ref, gate_ref, gate_t_ref, u_ref):
    j = pl.program_id(1)

    @pl.when(j == 0)
    def _():
        x = x_ref[...]
        ms = jnp.mean(x * x, axis=-1, keepdims=True)
        u = (x * lax.rsqrt(ms + EPS) * g_ref[...]).astype(BF16)
        u_ref[...] = u
        gates = _dot(u, wg_ref[...])
        gate_ref[...] = gates
        gate_t_ref[...] = gates.T[0:N_GATE_COLS, :]

    def tile(col0):
        return _dot(u_ref[...], w_ref[:, col0:col0 + IN_TN])

    for jj in (NAT_AQ, NAT_AK):
        @pl.when(j == jj)
        def _(jj=jj):
            acc = tile(_NAT_COLS[jj])
            scale = DA_QK_DIM ** -0.5 * LOG2_E if jj == NAT_AQ else 1.0
            cos = cos_ref[...] * scale
            sa = sa_ref[...] * scale
            sb = sb_ref[...] * scale
            for h in range(DA_HEADS):
                t = acc[:, h * LANES:(h + 1) * LANES]
                r = (t * cos + pltpu.roll(t, LANES - DA_ROT_DIM // 2, 1) * sa
                     + pltpu.roll(t, DA_ROT_DIM // 2, 1) * sb)
                nat_ref[:, h * LANES:(h + 1) * LANES] = r.astype(BF16)

    @pl.when(j == NAT_MQK)
    def _():
        nat_ref[...] = tile(_NAT_COLS[NAT_MQK]).astype(BF16)

    for jj in range(N_TR):
        @pl.when(j == N_NAT + jj)
        def _(jj=jj):
            tr_ref[...] = tile(_TR_COLS[jj]).T.astype(BF16)


def _inproj(rows, norm_g, w_main, w_gate, cos_t, sa_t, sb_t, tm):
    m = rows.shape[0]
    tiles_per_seq = cos_t.shape[0] // tm
    tab_spec = pl.BlockSpec((tm, LANES), lambda i, j: (i % tiles_per_seq, 0))
    nat_j = lambda j: jnp.minimum(j, N_NAT - 1)
    tr_j = lambda j: jnp.maximum(j - N_NAT, 0)
    return pl.pallas_call(
        _inproj_kernel,
        grid=(m // tm, N_NAT + N_TR),
        in_specs=[
            pl.BlockSpec((tm, D_MODEL), lambda i, j: (i, 0)),
            pl.BlockSpec((1, D_MODEL), lambda i, j: (0, 0)),
            pl.BlockSpec(w_main.shape, lambda i, j: (0, 0)),
            pl.BlockSpec((D_MODEL, LANES), lambda i, j: (0, 0)),
            tab_spec, tab_spec, tab_spec,
        ],
        out_specs=[
            pl.BlockSpec((tm, IN_TN), lambda i, j: (i, nat_j(j))),
            pl.BlockSpec((IN_TN, tm), lambda i, j: (tr_j(j), i)),
            pl.BlockSpec((tm, LANES), lambda i, j: (i, 0)),
            pl.BlockSpec((N_GATE_COLS, tm), lambda i, j: (0, i)),
        ],
        out_shape=[
            jax.ShapeDtypeStruct((m, N_NAT * IN_TN), BF16),
            jax.ShapeDtypeStruct((N_TR * IN_TN, m), BF16),
            jax.ShapeDtypeStruct((m, LANES), F32),
            jax.ShapeDtypeStruct((N_GATE_COLS, m), F32),
        ],
        scratch_shapes=[pltpu.VMEM((tm, D_MODEL), BF16)],
        compiler_params=_cparams(("parallel", "arbitrary")),
        name="inproj",
    )(rows, norm_g, w_main, w_gate, cos_t, sa_t, sb_t)


_CONV_PAD = SUBLANES


def _conv_kernel(x_ref, m_ref, w_ref, b_ref, qt_ref, k_ref, km_ref, s_ref):
    seq = x_ref.shape[0]
    j = pl.program_id(1)
    zeros = jnp.zeros((_CONV_PAD, LANES), F32)
    s_ref[0:_CONV_PAD, :] = zeros
    s_ref[_CONV_PAD:_CONV_PAD + N_META, :] = m_ref[...].astype(F32)
    s_ref[_CONV_PAD + N_META:_CONV_PAD + N_META + seq, :] = x_ref[...].astype(F32)
    s_ref[_CONV_PAD + N_META + seq:, :] = zeros
    w = w_ref[...]
    bias = b_ref[...]

    def conv(pos, n):
        acc = jnp.zeros((n, LANES), F32) + bias
        for t in range(CONV_W):
            start = _CONV_PAD + pos + t - CONV_W // 2
            acc = acc + s_ref[start:start + n, :] * w[t:t + 1, :]
        return acc * _sigmoid(acc)

    @pl.when(j < ML_HEADS)
    def _():
        for c in range(seq // CONV_ROWS):
            y = conv(N_META + c * CONV_ROWS, CONV_ROWS) * (ML_QK_DIM ** -0.5)
            qt_ref[:, c * CONV_ROWS:(c + 1) * CONV_ROWS] = y.T.astype(BF16)

    @pl.when(j >= ML_HEADS)
    def _():
        km_ref[0] = conv(0, N_META).astype(BF16)
        for c in range(seq // CONV_ROWS):
            k_ref[c * CONV_ROWS:(c + 1) * CONV_ROWS, :] = conv(
                N_META + c * CONV_ROWS, CONV_ROWS).astype(BF16)


def _conv(nat_x, nat_m, conv_w, conv_b, bsz, seq):
    col0 = NAT_MQK * (IN_TN // LANES)
    q_j = lambda j: jnp.minimum(j, ML_HEADS - 1)
    k_j = lambda j: jnp.maximum(j - ML_HEADS, 0)
    return pl.pallas_call(
        _conv_kernel,
        grid=(bsz, 2 * ML_HEADS),
        in_specs=[
            pl.BlockSpec((seq, LANES), lambda b, j: (b, col0 + j)),
            pl.BlockSpec((N_META, LANES), lambda b, j: (0, col0 + j)),
            pl.BlockSpec((SUBLANES, LANES), lambda b, j: (0, j)),
            pl.BlockSpec((1, LANES), lambda b, j: (0, j)),
        ],
        out_specs=[
            pl.BlockSpec((ML_QK_DIM, seq), lambda b, j: (q_j(j), b)),
            pl.BlockSpec((seq, ML_QK_DIM), lambda b, j: (b, k_j(j))),
            pl.BlockSpec((1, N_META, ML_QK_DIM), lambda b, j: (b, 0, k_j(j))),
        ],
        out_shape=[
            jax.ShapeDtypeStruct((ML_QK_WIDTH, bsz * seq), BF16),
            jax.ShapeDtypeStruct((bsz * seq, ML_QK_WIDTH), BF16),
            jax.ShapeDtypeStruct((bsz, N_META, ML_QK_WIDTH), BF16),
        ],
        scratch_shapes=[pltpu.VMEM((seq + N_META + 2 * _CONV_PAD, LANES), F32)],
        compiler_params=_cparams(("parallel", "arbitrary")),
        name="mlstm_conv",
    )(nat_x, nat_m, conv_w, conv_b)


def _attn_kernel(lam_ref, q_ref, k_ref, vt_ref, km_ref, vmt_ref, gain_ref, o_ref,
                 s_ref, sm_ref, acc_ref, o_t_ref):
    tq = ATT_TQ
    nq = q_ref.shape[0] // tq
    nkb = k_ref.shape[0] // ATT_KB
    lp = lam_ref[...]
    lam = (jnp.exp(jnp.sum(lp[0:1] * lp[1:2], axis=1, keepdims=True))
           - jnp.exp(jnp.sum(lp[2:3] * lp[3:4], axis=1, keepdims=True)) + LAM_INIT)
    lane = lax.broadcasted_iota(jnp.int32, (tq, LANES), 1)
    comps = range(2)

    def q_rows(qt):
        return pl.ds(pl.multiple_of(qt * tq, tq), tq)

    def masked_q(qt):
        q = q_ref[q_rows(qt), :]
        zero = jnp.zeros_like(q)
        return (jnp.where(lane < DA_QK_DIM, q, zero), jnp.where(lane >= DA_QK_DIM, q, zero))

    def fold(a):
        return a.reshape(a.shape[0] // SUBLANES, SUBLANES, tq)

    def scores(qz, kb, slot):
        k = k_ref[kb * ATT_KB:(kb + 1) * ATT_KB, :]
        bmax = []
        for c in comps:
            s = _dot_nt(k, qz[c])
            s_ref[slot, c] = s
            bm = jnp.max(fold(s), axis=0)
            if kb == 0:
                sm = _dot_nt(km_ref[...], qz[c])
                sm_ref[c] = sm
                bm = jnp.maximum(bm, jnp.max(fold(sm), axis=0))
            bmax.append(bm)
        return tuple(bmax)

    def consume(kb, slot, bmax, m, l):
        vt = vt_ref[:, kb * ATT_KB:(kb + 1) * ATT_KB]
        m_out, l_out = [], []
        for c in comps:
            bm = jnp.max(bmax[c], axis=0, keepdims=True)
            if kb == 0:
                m_new = bm
                pm = jnp.exp2(sm_ref[c] - m_new)
                p = jnp.exp2(s_ref[slot, c] - m_new)
                l_out.append(jnp.sum(fold(p), axis=0) + jnp.sum(fold(pm), axis=0))
                acc_ref[c] = _dot(vt, p.astype(BF16)) + _dot(vmt_ref[...], pm.astype(BF16))
            else:
                m_new = jnp.maximum(m[c], bm)
                alpha = jnp.exp2(m[c] - m_new)
                p = jnp.exp2(s_ref[slot, c] - m_new)
                l_out.append(alpha * l[c] + jnp.sum(fold(p), axis=0))
                acc_ref[c] = alpha * acc_ref[c] + _dot(vt, p.astype(BF16))
            m_out.append(m_new)
        return tuple(m_out), tuple(l_out)

    def finalize(qt):
        o = o_t_ref[...].T
        y = (o * lax.rsqrt(jnp.mean(o * o, axis=1, keepdims=True) + EPS)
             * gain_ref[...] * (1.0 - LAM_INIT))
        o_ref[q_rows(qt), :] = y.astype(BF16)

    def tile(qt, bmax):
        finalize(jnp.maximum(qt - 1, 0))
        qz = masked_q(qt)
        m = l = None
        for kb in range(nkb):
            if kb + 1 < nkb:
                bmax_next = scores(qz, kb + 1, (kb + 1) % 2)
            else:
                bmax_next = scores(masked_q(jnp.minimum(qt + 1, nq - 1)), 0, 0)
            m, l = consume(kb, kb % 2, bmax, m, l)
            bmax = bmax_next
        l1 = jnp.sum(l[0], axis=0, keepdims=True)
        l2 = jnp.sum(l[1], axis=0, keepdims=True)
        o_t_ref[...] = acc_ref[0] / l1 - lam * (acc_ref[1] / l2)
        return bmax

    o_t_ref[...] = jnp.zeros_like(o_t_ref)
    lax.fori_loop(0, nq, tile, scores(masked_q(0), 0, 0))
    finalize(nq - 1)


def _attention(lam_params, nat_x, nat_m, tr_x, tr_m, gain, bsz, seq):
    hb = IN_TN // LANES
    return pl.pallas_call(
        _attn_kernel,
        grid=(bsz, DA_HEADS),
        in_specs=[
            pl.BlockSpec((SUBLANES, LANES), lambda b, h: (0, 0)),
            pl.BlockSpec((seq, LANES), lambda b, h: (b, NAT_AQ * hb + h)),
            pl.BlockSpec((seq, LANES), lambda b, h: (b, NAT_AK * hb + h)),
            pl.BlockSpec((DA_V_DIM, seq), lambda b, h: (TR_AV * DA_HEADS + h, b)),
            pl.BlockSpec((N_META, LANES), lambda b, h: (0, NAT_AK * hb + h)),
            pl.BlockSpec((DA_V_DIM, N_META), lambda b, h: (TR_AV * DA_HEADS + h, 0)),
            pl.BlockSpec((1, LANES), lambda b, h: (0, h)),
        ],
        out_specs=pl.BlockSpec((seq, LANES), lambda b, h: (b, h)),
        out_shape=jax.ShapeDtypeStruct((bsz * seq, DA_WIDTH), BF16),
        scratch_shapes=[
            pltpu.VMEM((2, 2, ATT_KB, ATT_TQ), F32),
            pltpu.VMEM((2, N_META, ATT_TQ), F32),
            pltpu.VMEM((2, DA_V_DIM, ATT_TQ), F32),
            pltpu.VMEM((DA_V_DIM, ATT_TQ), F32),
        ],
        compiler_params=_cparams(("parallel", "parallel")),
        name="diff_attention",
    )(lam_params, nat_x, nat_x, tr_x, nat_m, tr_m, gain)


def _ml_state_update(h, k, vt, a_col, b_end, g_row, ct_ref, n_ref, m_ref):
    m_prev = m_ref[h][0:1, 0:1]
    m_new = jnp.maximum(b_end + m_prev, jnp.max(g_row, axis=1, keepdims=True))
    decay = jnp.exp(b_end + m_prev - m_new)
    kw = (k.astype(F32) * jnp.exp(b_end + a_col - m_new)).astype(BF16)
    ct_ref[h] = decay * ct_ref[h] + _dot(vt, kw)
    n_ref[h] = decay * n_ref[h] + jnp.sum(kw.astype(F32), axis=0, keepdims=True)
    m_ref[h] = jnp.broadcast_to(m_new, (SUBLANES, LANES))


def _mlstm_kernel(reverse, *refs):
    if reverse:
        (qt_ref, k_ref, vt_ref, gt_ref, bc_ref,
         out_ref, ct_ref, n_ref, m_ref) = refs
    else:
        (qt_ref, k_ref, vt_ref, gt_ref, bc_ref,
         km_ref, vmt_ref, gm_ref, gmt_ref, br_ref, hb_ref, op_ref, gain_ref,
         out_ref, ct_ref, n_ref, m_ref) = refs
    c = pl.program_id(1)
    chunk = k_ref.shape[0]
    gate0 = 2 * ML_HEADS if reverse else 0

    @pl.when(c == 0)
    def _():
        ct_ref[...] = jnp.zeros_like(ct_ref)
        n_ref[...] = jnp.zeros_like(n_ref)
        m_ref[...] = jnp.zeros_like(m_ref)
        if not reverse:
            gm = gm_ref[...] + br_ref[...]
            gmt = gmt_ref[...] + bc_ref[...]
            r = lax.broadcasted_iota(jnp.int32, (N_META, N_META), 0)
            s = lax.broadcasted_iota(jnp.int32, (N_META, N_META), 1)
            bcm = _cumsum_cols(_log_sigmoid(gm), s <= r)
            brm = _cumsum_rows(_log_sigmoid(gmt), r <= s)
            for h in range(ML_HEADS):
                gi, gf = gate0 + h, gate0 + ML_HEADS + h
                b_end = brm[gf:gf + 1, N_META - 1:N_META]
                _ml_state_update(
                    h, km_ref[0][:, h * ML_QK_DIM:(h + 1) * ML_QK_DIM],
                    vmt_ref[h * ML_V_DIM:(h + 1) * ML_V_DIM, :],
                    gm[:, gi:gi + 1] - bcm[:, gf:gf + 1], b_end,
                    b_end - brm[gf:gf + 1, :] + gmt[gi:gi + 1, :],
                    ct_ref, n_ref, m_ref)

    grow = gt_ref[...] + bc_ref[...]
    row = lax.broadcasted_iota(jnp.int32, (chunk, chunk), 0)
    col = lax.broadcasted_iota(jnp.int32, (chunk, chunk), 1)
    mask = (row >= col) if reverse else (row <= col)
    brow_all = _cumsum_rows(_log_sigmoid(grow), mask)
    a_rows = grow - pltpu.roll(brow_all, N_GATE_COLS - ML_HEADS, 0)
    a_cols = jnp.concatenate(
        [a_rows, jnp.zeros((LANES - N_GATE_COLS, chunk), F32)], axis=0).T
    end = 0 if reverse else chunk - 1

    for h in range(ML_HEADS):
        gi, gf = gate0 + h, gate0 + ML_HEADS + h
        k = k_ref[:, h * ML_QK_DIM:(h + 1) * ML_QK_DIM]
        qt = qt_ref[h * ML_QK_DIM:(h + 1) * ML_QK_DIM, :]
        vt = vt_ref[h * ML_V_DIM:(h + 1) * ML_V_DIM, :]
        a_col = a_cols[:, gi:gi + 1]
        b_row = brow_all[gf:gf + 1, :]
        li_row = grow[gi:gi + 1, :]
        b_end = b_row[:, end:end + 1]
        m_prev = m_ref[h][0:1, 0:1]

        d = jnp.where(mask, b_row + a_col, NEG)
        m_inter = b_row + m_prev
        m_t = jnp.maximum(m_inter, jnp.max(d, axis=0, keepdims=True))
        w_inter = jnp.exp(m_inter - m_t)
        sg = _dot(k, qt) * jnp.exp(d - m_t)
        num = w_inter * _dot(ct_ref[h].astype(BF16), qt) + _dot(vt, sg.astype(BF16))
        n16 = jnp.broadcast_to(n_ref[h], (2 * SUBLANES, ML_QK_DIM)).astype(BF16)
        den = w_inter * _dot(n16, qt)[0:1, :] + jnp.sum(sg, axis=0, keepdims=True)
        hout = num / jnp.maximum(jnp.abs(den), jnp.exp(-m_t))

        _ml_state_update(h, k, vt, a_col, b_end, b_end - b_row + li_row, ct_ref, n_ref, m_ref)

        rows = slice(h * ML_V_DIM, (h + 1) * ML_V_DIM)
        if reverse:
            out_ref[rows, :] = hout
        else:
            ht = hout + hb_ref[rows, :]
            y = ht * lax.rsqrt(jnp.mean(ht * ht, axis=0, keepdims=True) + EPS) * gain_ref[rows, :]
            out_ref[rows, :] = (_sigmoid(op_ref[rows, :].astype(F32)) * y).astype(BF16)


def _mlstm(reverse, q_t, k_x, tr_x, gates_t, bias_col, bsz, seq, fwd_extra=None):
    nc = seq // ML_CHUNK
    if reverse:
        rowblk = lambda b, c: b * nc + (nc - 1 - c)
    else:
        rowblk = lambda b, c: b * nc + c
    in_specs = [
        pl.BlockSpec((ML_QK_WIDTH, ML_CHUNK), lambda b, c: (0, rowblk(b, c))),
        pl.BlockSpec((ML_CHUNK, ML_QK_WIDTH), lambda b, c: (rowblk(b, c), 0)),
        pl.BlockSpec((ML_WIDTH, ML_CHUNK), lambda b, c: (TR_MV, rowblk(b, c))),
        pl.BlockSpec((N_GATE_COLS, ML_CHUNK), lambda b, c: (0, rowblk(b, c))),
        pl.BlockSpec((N_GATE_COLS, 1), lambda b, c: (0, 0)),
    ]
    args = [q_t, k_x, tr_x, gates_t, bias_col]
    if reverse:
        out_dtype = F32
    else:
        k_m, tr_m, gates_m, gates_m_t, bias_row, hb, gain_b = fwd_extra
        in_specs += [
            pl.BlockSpec((1, N_META, ML_QK_WIDTH), lambda b, c: (b, 0, 0)),
            pl.BlockSpec((ML_WIDTH, N_META), lambda b, c: (TR_MV, 0)),
            pl.BlockSpec((N_META, LANES), lambda b, c: (0, 0)),
            pl.BlockSpec((N_GATE_COLS, N_META), lambda b, c: (0, 0)),
            pl.BlockSpec((1, LANES), lambda b, c: (0, 0)),
            pl.BlockSpec((ML_WIDTH, ML_CHUNK), lambda b, c: (0, rowblk(b, c))),
            pl.BlockSpec((ML_WIDTH, ML_CHUNK), lambda b, c: (TR_MO, rowblk(b, c))),
            pl.BlockSpec((ML_WIDTH, ML_CHUNK), lambda b, c: (0, 0)),
        ]
        args += [k_m, tr_m, gates_m, gates_m_t, bias_row, hb, tr_x, gain_b]
        out_dtype = BF16
    return pl.pallas_call(
        functools.partial(_mlstm_kernel, reverse),
        grid=(bsz, nc),
        in_specs=in_specs,
        out_specs=pl.BlockSpec((ML_WIDTH, ML_CHUNK), lambda b, c: (0, rowblk(b, c))),
        out_shape=jax.ShapeDtypeStruct((ML_WIDTH, bsz * seq), out_dtype),
        scratch_shapes=[
            pltpu.VMEM((ML_HEADS, ML_V_DIM, ML_QK_DIM), F32),
            pltpu.VMEM((ML_HEADS, 1, ML_QK_DIM), F32),
            pltpu.VMEM((ML_HEADS, SUBLANES, LANES), F32),
        ],
        compiler_params=_cparams(("parallel", "arbitrary")),
        name="mlstm_bwd" if reverse else "mlstm_fwd",
    )(*args)


def _mix_ffn_kernel(a_ref, mt_ref, x_ref, wo_ref, g2_ref, wg_ref, wu_ref, wd_ref, gf_ref,
                    o_ref, u_ref):
    j = pl.program_id(1)

    @pl.when(j == 0)
    def _():
        h = (_dot(a_ref[...], wo_ref[0:DA_WIDTH, :])
             + _dot_tn(mt_ref[...], wo_ref[DA_WIDTH:DA_WIDTH + ML_WIDTH, :]) + x_ref[...])
        o_ref[...] = h
        u = h * lax.rsqrt(jnp.mean(h * h, axis=-1, keepdims=True) + EPS) * g2_ref[...]
        u_ref[...] = u.astype(BF16)

    u = u_ref[...]
    g = _dot(u, wg_ref[...])
    ff = (g * _sigmoid(g) * _dot(u, wu_ref[...])).astype(BF16)
    o_ref[...] += _dot(ff, wd_ref[...])

    @pl.when(j == pl.num_programs(1) - 1)
    def _():
        y = o_ref[...]
        o_ref[...] = y * lax.rsqrt(jnp.mean(y * y, axis=-1, keepdims=True) + EPS) * gf_ref[...]


def _mix_ffn(attn, ml_t, x_rows, w_out, norm_ffn_g, w_gate, w_up, w_down, norm_final_g):
    m = x_rows.shape[0]
    return pl.pallas_call(
        _mix_ffn_kernel,
        grid=(m // FFN_TM, D_FF // FFN_TF),
        in_specs=[
            pl.BlockSpec((FFN_TM, DA_WIDTH), lambda i, j: (i, 0)),
            pl.BlockSpec((ML_WIDTH, FFN_TM), lambda i, j: (0, i)),
            pl.BlockSpec((FFN_TM, D_MODEL), lambda i, j: (i, 0)),
            pl.BlockSpec((DA_WIDTH + ML_WIDTH, D_MODEL), lambda i, j: (0, 0)),
            pl.BlockSpec((1, D_MODEL), lambda i, j: (0, 0)),
            pl.BlockSpec((D_MODEL, FFN_TF), lambda i, j: (0, j)),
            pl.BlockSpec((D_MODEL, FFN_TF), lambda i, j: (0, j)),
            pl.BlockSpec((FFN_TF, D_MODEL), lambda i, j: (j, 0)),
            pl.BlockSpec((1, D_MODEL), lambda i, j: (0, 0)),
        ],
        out_specs=pl.BlockSpec((FFN_TM, D_MODEL), lambda i, j: (i, 0)),
        out_shape=jax.ShapeDtypeStruct((m, D_MODEL), F32),
        scratch_shapes=[pltpu.VMEM((FFN_TM, D_MODEL), BF16)],
        compiler_params=_cparams(("parallel", "arbitrary")),
        name="mix_ffn",
    )(attn, ml_t, x_rows, w_out, norm_ffn_g, w_gate, w_up, w_down, norm_final_g)


def _rope_tables(pos0, n):
    lane = jnp.arange(LANES) % DA_QK_DIM
    half = DA_ROT_DIM // 2
    inv = ROPE_THETA ** (-jnp.arange(0, DA_ROT_DIM, 2, dtype=F32) / DA_ROT_DIM)
    inv_lane = jnp.where(lane < DA_ROT_DIM, inv[lane % half], 0.0)
    pos = jnp.arange(pos0, pos0 + n, dtype=F32)
    ang = pos[:, None] * inv_lane[None, :]
    sin = jnp.sin(ang)
    sa = jnp.where(lane < half, -sin, 0.0)
    sb = jnp.where((lane >= half) & (lane < DA_ROT_DIM), sin, 0.0)
    return jnp.cos(ang), sa, sb


def _pad_lanes(a, width=LANES):
    return jnp.pad(a, ((0, 0), (0, width - a.shape[1])))


def kernel(x, meta_tokens, norm_mix, w_in, da_lambda_q1, da_lambda_k1, da_lambda_q2, da_lambda_k2,
           da_head_norm, ml_conv_w, ml_conv_b, ml_gate_bias, ml_head_norm, w_out, norm_ffn,
           w_gate, w_up, w_down, norm_final):
    bsz, seq, _ = x.shape
    x_rows = x.reshape(bsz * seq, D_MODEL)

    main_cols = N_NAT * IN_TN + N_TR * IN_TN
    w_main = w_in[0][:, :main_cols].astype(BF16)
    w_gates = _pad_lanes(w_in[0][:, main_cols:]).astype(BF16)
    norm_mix0 = norm_mix[0][None, :]

    tabs_x = _rope_tables(N_META, seq)
    tabs_m = _rope_tables(0, N_META)
    nat_x, tr_x, _, gates_t = _inproj(x_rows, norm_mix0, w_main, w_gates, *tabs_x, tm=IN_TM)
    nat_m, tr_m, gates_m, gates_m_t = _inproj(meta_tokens, norm_mix0, w_main, w_gates, *tabs_m,
                                              tm=N_META)

    conv_w = jnp.pad(ml_conv_w[0], ((0, SUBLANES - CONV_W), (0, 0)))
    q_t, k_x, k_m = _conv(nat_x, nat_m, conv_w, ml_conv_b[0][None, :], bsz, seq)

    lam_params = jnp.pad(
        jnp.stack([da_lambda_q1[0], da_lambda_k1[0], da_lambda_q2[0], da_lambda_k2[0]]),
        ((0, SUBLANES - 4), (0, LANES - DA_QK_DIM)))
    attn = _attention(lam_params, nat_x, nat_m, tr_x, tr_m,
                      da_head_norm[0].reshape(1, DA_WIDTH), bsz, seq)

    bias = ml_gate_bias[0].reshape(N_GATE_COLS)
    bias_row = _pad_lanes(bias[None, :])
    bias_col = bias[:, None]
    gain_b = jnp.broadcast_to(ml_head_norm[0].reshape(ML_WIDTH, 1), (ML_WIDTH, ML_CHUNK))
    h_bwd = _mlstm(True, q_t, k_x, tr_x, gates_t, bias_col, bsz, seq)
    ml_t = _mlstm(False, q_t, k_x, tr_x, gates_t, bias_col, bsz, seq,
                  fwd_extra=(k_m, tr_m, gates_m, gates_m_t, bias_row, h_bwd, gain_b))

    out = _mix_ffn(attn, ml_t, x_rows, w_out[0].astype(BF16), norm_ffn[0][None, :],
                   w_gate[0].astype(BF16), w_up[0].astype(BF16), w_down[0].astype(BF16),
                   norm_final[None, :])
    return out.reshape(bsz, seq, D_MODEL)
```

```python
import functools

import jax
import jax.numpy as jnp
from jax import lax
from jax.experimental import pallas as pl
from jax.experimental.pallas import tpu as pltpu

F32 = jnp.float32
BF16 = jnp.bfloat16

D_MODEL = 2048
N_META = 16
EPS = 1e-6
ROPE_THETA = 500000.0
NEG = -1e30

DA_QK_DIM = 64
DA_V_DIM = 128
DA_HEADS = 8
DA_ROT_DIM = 16
DA_WIDTH = DA_HEADS * DA_V_DIM
LAM_INIT = 0.8 - 0.6 * 1.0
LOG2_E = 1.4426950408889634

ML_HEADS = 4
ML_V_DIM = 256
ML_QK_DIM = 128
ML_WIDTH = ML_HEADS * ML_V_DIM
ML_QK_WIDTH = ML_HEADS * ML_QK_DIM
CONV_W = 5
N_GATES = 4
N_GATE_COLS = N_GATES * ML_HEADS

D_FF = 5632

IN_TN = 1024
NAT_AQ, NAT_AK, NAT_MQK = 0, 1, 2
TR_AV, TR_MV, TR_MO = 0, 1, 2
N_NAT = 3
N_TR = 3

LANES = 128
SUBLANES = 8
VMEM_LIMIT = 56 * 1024 * 1024

IN_TM = 512
ATT_TQ = 512
ATT_KB = 512
ML_CHUNK = 256
OUT_TM = 256
FFN_TM = 512
FFN_TF = 512
CONV_ROWS = 512


def _cparams(sem):
    return pltpu.CompilerParams(dimension_semantics=sem, vmem_limit_bytes=VMEM_LIMIT)


def _dot(a, b):
    return jnp.dot(a, b, preferred_element_type=F32)


def _dot_nt(a, b):
    return lax.dot_general(a, b, (((1,), (1,)), ((), ())), preferred_element_type=F32)


def _dot_tn(a, b):
    return lax.dot_general(a, b, (((0,), (0,)), ((), ())), preferred_element_type=F32)


def _bf16_pieces(x):
    hi = x.astype(BF16)
    rest = x - hi.astype(F32)
    mid = rest.astype(BF16)
    return hi, mid, (rest - mid.astype(F32)).astype(BF16)


def _cumsum_rows(x, visible):
    r = x.shape[0]
    ones = jnp.where(visible, 1.0, 0.0).astype(BF16)
    parts = _dot(jnp.concatenate(_bf16_pieces(x), axis=0), ones)
    return parts[0:r] + parts[r:2 * r] + parts[2 * r:3 * r]


def _cumsum_cols(x, visible):
    c = x.shape[1]
    ones = jnp.where(visible, 1.0, 0.0).astype(BF16)
    parts = _dot(ones, jnp.concatenate(_bf16_pieces(x), axis=1))
    return parts[:, 0:c] + parts[:, c:2 * c] + parts[:, 2 * c:3 * c]


def _sigmoid(x):
    return 1.0 / (1.0 + jnp.exp(-x))


def _log_sigmoid(x):
    return jnp.minimum(x, 0.0) - jnp.log(1.0 + jnp.exp(-jnp.abs(x)))


_NAT_COLS = (0, 1024, 3072)
_TR_COLS = (2048, 4096, 5120)


def _inproj_kernel(x_ref, g_ref, w_ref, wg_ref, cos_ref, sa_ref, sb_ref,
                   nat_ref, tr_ref, gate_ref, gate_t_ref, u_ref):
    j = pl.program_id(1)

    @pl.when(j == 0)
    def _():
        x = x_ref[...]
        ms = jnp.mean(x * x, axis=-1, keepdims=True)
        u = (x * lax.rsqrt(ms + EPS) * g_ref[...]).astype(BF16)
        u_ref[...] = u
        gates = _dot(u, wg_ref[...])
        gate_ref[...] = gates
        gate_t_ref[...] = gates.T[0:N_GATE_COLS, :]

    def tile(col0):
        return _dot(u_ref[...], w_ref[:, col0:col0 + IN_TN])

    for jj in (NAT_AQ, NAT_AK):
        @pl.when(j == jj)
        def _(jj=jj):
            acc = tile(_NAT_COLS[jj])
            scale = DA_QK_DIM ** -0.5 * LOG2_E if jj == NAT_AQ else 1.0
            cos = cos_ref[...] * scale
            sa = sa_ref[...] * scale
            sb = sb_ref[...] * scale
            for h in range(DA_HEADS):
                t = acc[:, h * LANES:(h + 1) * LANES]
                r = (t * cos + pltpu.roll(t, LANES - DA_ROT_DIM // 2, 1) * sa
                     + pltpu.roll(t, DA_ROT_DIM // 2, 1) * sb)
                nat_ref[:, h * LANES:(h + 1) * LANES] = r.astype(BF16)

    @pl.when(j == NAT_MQK)
    def _():
        nat_ref[...] = tile(_NAT_COLS[NAT_MQK]).astype(BF16)

    for jj in range(N_TR):
        @pl.when(j == N_NAT + jj)
        def _(jj=jj):
            tr_ref[...] = tile(_TR_COLS[jj]).T.astype(BF16)


def _inproj(rows, norm_g, w_main, w_gate, cos_t, sa_t, sb_t, tm):
    m = rows.shape[0]
    tiles_per_seq = cos_t.shape[0] // tm
    tab_spec = pl.BlockSpec((tm, LANES), lambda i, j: (i % tiles_per_seq, 0))
    nat_j = lambda j: jnp.minimum(j, N_NAT - 1)
    tr_j = lambda j: jnp.maximum(j - N_NAT, 0)
    return pl.pallas_call(
        _inproj_kernel,
        grid=(m // tm, N_NAT + N_TR),
        in_specs=[
            pl.BlockSpec((tm, D_MODEL), lambda i, j: (i, 0)),
            pl.BlockSpec((1, D_MODEL), lambda i, j: (0, 0)),
            pl.BlockSpec(w_main.shape, lambda i, j: (0, 0)),
            pl.BlockSpec((D_MODEL, LANES), lambda i, j: (0, 0)),
            tab_spec, tab_spec, tab_spec,
        ],
        out_specs=[
            pl.BlockSpec((tm, IN_TN), lambda i, j: (i, nat_j(j))),
            pl.BlockSpec((IN_TN, tm), lambda i, j: (tr_j(j), i)),
            pl.BlockSpec((tm, LANES), lambda i, j: (i, 0)),
            pl.BlockSpec((N_GATE_COLS, tm), lambda i, j: (0, i)),
        ],
        out_shape=[
            jax.ShapeDtypeStruct((m, N_NAT * IN_TN), BF16),
            jax.ShapeDtypeStruct((N_TR * IN_TN, m), BF16),
            jax.ShapeDtypeStruct((m, LANES), F32),
            jax.ShapeDtypeStruct((N_GATE_COLS, m), F32),
        ],
        scratch_shapes=[pltpu.VMEM((tm, D_MODEL), BF16)],
        compiler_params=_cparams(("parallel", "arbitrary")),
        name="inproj",
    )(rows, norm_g, w_main, w_gate, cos_t, sa_t, sb_t)


_CONV_PAD = SUBLANES


def _conv_kernel(x_ref, m_ref, w_ref, b_ref, qt_ref, k_ref, km_ref, s_ref):
    seq = x_ref.shape[0]
    j = pl.program_id(1)
    zeros = jnp.zeros((_CONV_PAD, LANES), F32)
    s_ref[0:_CONV_PAD, :] = zeros
    s_ref[_CONV_PAD:_CONV_PAD + N_META, :] = m_ref[...].astype(F32)
    s_ref[_CONV_PAD + N_META:_CONV_PAD + N_META + seq, :] = x_ref[...].astype(F32)
    s_ref[_CONV_PAD + N_META + seq:, :] = zeros
    w = w_ref[...]
    bias = b_ref[...]

    def conv(pos, n):
        acc = jnp.zeros((n, LANES), F32) + bias
        for t in range(CONV_W):
            start = _CONV_PAD + pos + t - CONV_W // 2
            acc = acc + s_ref[start:start + n, :] * w[t:t + 1, :]
        return acc * _sigmoid(acc)

    @pl.when(j < ML_HEADS)
    def _():
        for c in range(seq // CONV_ROWS):
            y = conv(N_META + c * CONV_ROWS, CONV_ROWS) * (ML_QK_DIM ** -0.5)
            qt_ref[:, c * CONV_ROWS:(c + 1) * CONV_ROWS] = y.T.astype(BF16)

    @pl.when(j >= ML_HEADS)
    def _():
        km_ref[0] = conv(0, N_META).astype(BF16)
        for c in range(seq // CONV_ROWS):
            k_ref[c * CONV_ROWS:(c + 1) * CONV_ROWS, :] = conv(
                N_META + c * CONV_ROWS, CONV_ROWS).astype(BF16)


def _conv(nat_x, nat_m, conv_w, conv_b, bsz, seq):
    col0 = NAT_MQK * (IN_TN // LANES)
    q_j = lambda j: jnp.minimum(j, ML_HEADS - 1)
    k_j = lambda j: jnp.maximum(j - ML_HEADS, 0)
    return pl.pallas_call(
        _conv_kernel,
        grid=(bsz, 2 * ML_HEADS),
        in_specs=[
            pl.BlockSpec((seq, LANES), lambda b, j: (b, col0 + j)),
            pl.BlockSpec((N_META, LANES), lambda b, j: (0, col0 + j)),
            pl.BlockSpec((SUBLANES, LANES), lambda b, j: (0, j)),
            pl.BlockSpec((1, LANES), lambda b, j: (0, j)),
        ],
        out_specs=[
            pl.BlockSpec((ML_QK_DIM, seq), lambda b, j: (q_j(j), b)),
            pl.BlockSpec((seq, ML_QK_DIM), lambda b, j: (b, k_j(j))),
            pl.BlockSpec((1, N_META, ML_QK_DIM), lambda b, j: (b, 0, k_j(j))),
        ],
        out_shape=[
            jax.ShapeDtypeStruct((ML_QK_WIDTH, bsz * seq), BF16),
            jax.ShapeDtypeStruct((bsz * seq, ML_QK_WIDTH), BF16),
            jax.ShapeDtypeStruct((bsz, N_META, ML_QK_WIDTH), BF16),
        ],
        scratch_shapes=[pltpu.VMEM((seq + N_META + 2 * _CONV_PAD, LANES), F32)],
        compiler_params=_cparams(("parallel", "arbitrary")),
        name="mlstm_conv",
    )(nat_x, nat_m, conv_w, conv_b)


_ATT_ACC_ROWS = DA_V_DIM + 16


def _attn_kernel(lam_ref, q_ref, k_ref, vt_ref, km_ref, vmt_ref, gain_ref, o_ref,
                 s_ref, sm_ref, acc_ref, o_t_ref, vte_ref, vmte_ref):
    tq = ATT_TQ
    nq = q_ref.shape[0] // tq
    nkb = k_ref.shape[0] // ATT_KB
    lp = lam_ref[...]
    lam = (jnp.exp(jnp.sum(lp[0:1] * lp[1:2], axis=1, keepdims=True))
           - jnp.exp(jnp.sum(lp[2:3] * lp[3:4], axis=1, keepdims=True)) + LAM_INIT)
    lane = lax.broadcasted_iota(jnp.int32, (tq, LANES), 1)
    comps = range(2)

    def q_rows(qt):
        return pl.ds(pl.multiple_of(qt * tq, tq), tq)

    def masked_q(qt):
        q = q_ref[q_rows(qt), :]
        zero = jnp.zeros_like(q)
        return (jnp.where(lane < DA_QK_DIM, q, zero), jnp.where(lane >= DA_QK_DIM, q, zero))

    def fold(a):
        return a.reshape(a.shape[0] // SUBLANES, SUBLANES, tq)

    def scores(qz, kb, slot):
        k = k_ref[kb * ATT_KB:(kb + 1) * ATT_KB, :]
        bmax = []
        for c in comps:
            s = _dot_nt(k, qz[c])
            s_ref[slot, c] = s
            bm = jnp.max(fold(s), axis=0)
            if kb == 0:
                sm = _dot_nt(km_ref[...], qz[c])
                sm_ref[c] = sm
                bm = jnp.maximum(bm, jnp.max(fold(sm), axis=0))
            bmax.append(bm)
        return tuple(bmax)

    def probs(s, m):
        return jnp.exp2(s - m).astype(BF16)

    def consume(kb, slot, bmax, m):
        vte = vte_ref[:, kb * ATT_KB:(kb + 1) * ATT_KB]
        m_out = []
        for c in comps:
            bm = jnp.max(bmax[c], axis=0, keepdims=True)
            if kb == 0:
                m_new = bm
                acc_ref[c] = (_dot(vte, probs(s_ref[slot, c], m_new))
                              + _dot(vmte_ref[...], probs(sm_ref[c], m_new)))
            else:
                m_new = jnp.maximum(m[c], bm)
                alpha = jnp.exp2(m[c] - m_new)
                acc_ref[c] = alpha * acc_ref[c] + _dot(vte, probs(s_ref[slot, c], m_new))
            m_out.append(m_new)
        return tuple(m_out)

    def finalize(qt):
        o = o_t_ref[...].T
        y = (o * lax.rsqrt(jnp.mean(o * o, axis=1, keepdims=True) + EPS)
             * gain_ref[...] * (1.0 - LAM_INIT))
        o_ref[q_rows(qt), :] = y.astype(BF16)

    def tile(qt, bmax):
        finalize(jnp.maximum(qt - 1, 0))
        qz = masked_q(qt)
        m = None
        for kb in range(nkb):
            if kb + 1 < nkb:
                bmax_next = scores(qz, kb + 1, (kb + 1) % 2)
            else:
                bmax_next = scores(masked_q(jnp.minimum(qt + 1, nq - 1)), 0, 0)
            m = consume(kb, kb % 2, bmax, m)
            bmax = bmax_next
        l1 = acc_ref[0, DA_V_DIM:DA_V_DIM + 1, :]
        l2 = acc_ref[1, DA_V_DIM:DA_V_DIM + 1, :]
        o_t_ref[...] = (acc_ref[0, 0:DA_V_DIM, :] / l1
                        - lam * (acc_ref[1, 0:DA_V_DIM, :] / l2))
        return bmax

    ones_row = jnp.where(
        lax.broadcasted_iota(jnp.int32, (_ATT_ACC_ROWS - DA_V_DIM, 1), 0) == 0, 1.0, 0.0)
    vte_ref[0:DA_V_DIM, :] = vt_ref[...]
    vte_ref[DA_V_DIM:, :] = jnp.broadcast_to(ones_row, (_ATT_ACC_ROWS - DA_V_DIM, vte_ref.shape[1])
                                             ).astype(BF16)
    vmte_ref[0:DA_V_DIM, :] = vmt_ref[...]
    vmte_ref[DA_V_DIM:, :] = jnp.broadcast_to(ones_row, (_ATT_ACC_ROWS - DA_V_DIM, N_META)
                                              ).astype(BF16)
    o_t_ref[...] = jnp.zeros_like(o_t_ref)
    lax.fori_loop(0, nq, tile, scores(masked_q(0), 0, 0))
    finalize(nq - 1)


def _attention(lam_params, nat_x, nat_m, tr_x, tr_m, gain, bsz, seq):
    hb = IN_TN // LANES
    return pl.pallas_call(
        _attn_kernel,
        grid=(bsz, DA_HEADS),
        in_specs=[
            pl.BlockSpec((SUBLANES, LANES), lambda b, h: (0, 0)),
            pl.BlockSpec((seq, LANES), lambda b, h: (b, NAT_AQ * hb + h)),
            pl.BlockSpec((seq, LANES), lambda b, h: (b, NAT_AK * hb + h)),
            pl.BlockSpec((DA_V_DIM, seq), lambda b, h: (TR_AV * DA_HEADS + h, b)),
            pl.BlockSpec((N_META, LANES), lambda b, h: (0, NAT_AK * hb + h)),
            pl.BlockSpec((DA_V_DIM, N_META), lambda b, h: (TR_AV * DA_HEADS + h, 0)),
            pl.BlockSpec((1, LANES), lambda b, h: (0, h)),
        ],
        out_specs=pl.BlockSpec((seq, LANES), lambda b, h: (b, h)),
        out_shape=jax.ShapeDtypeStruct((bsz * seq, DA_WIDTH), BF16),
        scratch_shapes=[
            pltpu.VMEM((2, 2, ATT_KB, ATT_TQ), F32),
            pltpu.VMEM((2, N_META, ATT_TQ), F32),
            pltpu.VMEM((2, _ATT_ACC_ROWS, ATT_TQ), F32),
            pltpu.VMEM((DA_V_DIM, ATT_TQ), F32),
            pltpu.VMEM((_ATT_ACC_ROWS, seq), BF16),
            pltpu.VMEM((_ATT_ACC_ROWS, N_META), BF16),
        ],
        compiler_params=_cparams(("parallel", "parallel")),
        name="diff_attention",
    )(lam_params, nat_x, nat_x, tr_x, nat_m, tr_m, gain)


def _ml_state_update(h, k, vt, a_col, b_end, g_row, ct_ref, n_ref, m_ref):
    m_prev = m_ref[h][0:1, 0:1]
    m_new = jnp.maximum(b_end + m_prev, jnp.max(g_row, axis=1, keepdims=True))
    decay = jnp.exp(b_end + m_prev - m_new)
    kw = (k.astype(F32) * jnp.exp(b_end + a_col - m_new)).astype(BF16)
    ct_ref[h] = decay * ct_ref[h] + _dot(vt, kw)
    n_ref[h] = decay * n_ref[h] + jnp.sum(kw.astype(F32), axis=0, keepdims=True)
    m_ref[h] = jnp.broadcast_to(m_new, (SUBLANES, LANES))


def _mlstm_kernel(reverse, *refs):
    if reverse:
        (qt_ref, k_ref, vt_ref, gt_ref, bc_ref,
         out_ref, ct_ref, n_ref, m_ref) = refs
    else:
        (qt_ref, k_ref, vt_ref, gt_ref, bc_ref,
         km_ref, vmt_ref, gm_ref, gmt_ref, br_ref, hb_ref, op_ref, gain_ref,
         out_ref, ct_ref, n_ref, m_ref) = refs
    c = pl.program_id(1)
    chunk = k_ref.shape[0]
    gate0 = 2 * ML_HEADS if reverse else 0

    @pl.when(c == 0)
    def _():
        ct_ref[...] = jnp.zeros_like(ct_ref)
        n_ref[...] = jnp.zeros_like(n_ref)
        m_ref[...] = jnp.zeros_like(m_ref)
        if not reverse:
            gm = gm_ref[...] + br_ref[...]
            gmt = gmt_ref[...] + bc_ref[...]
            r = lax.broadcasted_iota(jnp.int32, (N_META, N_META), 0)
            s = lax.broadcasted_iota(jnp.int32, (N_META, N_META), 1)
            bcm = _cumsum_cols(_log_sigmoid(gm), s <= r)
            brm = _cumsum_rows(_log_sigmoid(gmt), r <= s)
            for h in range(ML_HEADS):
                gi, gf = gate0 + h, gate0 + ML_HEADS + h
                b_end = brm[gf:gf + 1, N_META - 1:N_META]
                _ml_state_update(
                    h, km_ref[0][:, h * ML_QK_DIM:(h + 1) * ML_QK_DIM],
                    vmt_ref[h * ML_V_DIM:(h + 1) * ML_V_DIM, :],
                    gm[:, gi:gi + 1] - bcm[:, gf:gf + 1], b_end,
                    b_end - brm[gf:gf + 1, :] + gmt[gi:gi + 1, :],
                    ct_ref, n_ref, m_ref)

    grow = gt_ref[...] + bc_ref[...]
    row = lax.broadcasted_iota(jnp.int32, (chunk, chunk), 0)
    col = lax.broadcasted_iota(jnp.int32, (chunk, chunk), 1)
    mask = (row >= col) if reverse else (row <= col)
    brow_all = _cumsum_rows(_log_sigmoid(grow), mask)
    a_rows = grow - pltpu.roll(brow_all, N_GATE_COLS - ML_HEADS, 0)
    a_cols = jnp.concatenate(
        [a_rows, jnp.zeros((LANES - N_GATE_COLS, chunk), F32)], axis=0).T
    end = 0 if reverse else chunk - 1

    for h in range(ML_HEADS):
        gi, gf = gate0 + h, gate0 + ML_HEADS + h
        k = k_ref[:, h * ML_QK_DIM:(h + 1) * ML_QK_DIM]
        qt = qt_ref[h * ML_QK_DIM:(h + 1) * ML_QK_DIM, :]
        vt = vt_ref[h * ML_V_DIM:(h + 1) * ML_V_DIM, :]
        a_col = a_cols[:, gi:gi + 1]
        b_row = brow_all[gf:gf + 1, :]
        li_row = grow[gi:gi + 1, :]
        b_end = b_row[:, end:end + 1]
        m_prev = m_ref[h][0:1, 0:1]

        d = jnp.where(mask, b_row + a_col, NEG)
        m_inter = b_row + m_prev
        m_t = jnp.maximum(m_inter, jnp.max(d, axis=0, keepdims=True))
        w_inter = jnp.exp(m_inter - m_t)
        sg = _dot(k, qt) * jnp.exp(d - m_t)
        num = w_inter * _dot(ct_ref[h].astype(BF16), qt) + _dot(vt, sg.astype(BF16))
        n16 = jnp.broadcast_to(n_ref[h], (2 * SUBLANES, ML_QK_DIM)).astype(BF16)
        den = w_inter * _dot(n16, qt)[0:1, :] + jnp.sum(sg, axis=0, keepdims=True)
        hout = num / jnp.maximum(jnp.abs(den), jnp.exp(-m_t))

        _ml_state_update(h, k, vt, a_col, b_end, b_end - b_row + li_row, ct_ref, n_ref, m_ref)

        rows = slice(h * ML_V_DIM, (h + 1) * ML_V_DIM)
        if reverse:
            out_ref[rows, :] = hout
        else:
            ht = hout + hb_ref[rows, :]
            y = ht * lax.rsqrt(jnp.mean(ht * ht, axis=0, keepdims=True) + EPS) * gain_ref[rows, :]
            out_ref[rows, :] = (_sigmoid(op_ref[rows, :].astype(F32)) * y).astype(BF16)


def _mlstm(reverse, q_t, k_x, tr_x, gates_t, bias_col, bsz, seq, fwd_extra=None):
    nc = seq // ML_CHUNK
    if reverse:
        rowblk = lambda b, c: b * nc + (nc - 1 - c)
    else:
        rowblk = lambda b, c: b * nc + c
    in_specs = [
        pl.BlockSpec((ML_QK_WIDTH, ML_CHUNK), lambda b, c: (0, rowblk(b, c))),
        pl.BlockSpec((ML_CHUNK, ML_QK_WIDTH), lambda b, c: (rowblk(b, c), 0)),
        pl.BlockSpec((ML_WIDTH, ML_CHUNK), lambda b, c: (TR_MV, rowblk(b, c))),
        pl.BlockSpec((N_GATE_COLS, ML_CHUNK), lambda b, c: (0, rowblk(b, c))),
        pl.BlockSpec((N_GATE_COLS, 1), lambda b, c: (0, 0)),
    ]
    args = [q_t, k_x, tr_x, gates_t, bias_col]
    if reverse:
        out_dtype = F32
    else:
        k_m, tr_m, gates_m, gates_m_t, bias_row, hb, gain_b = fwd_extra
        in_specs += [
            pl.BlockSpec((1, N_META, ML_QK_WIDTH), lambda b, c: (b, 0, 0)),
            pl.BlockSpec((ML_WIDTH, N_META), lambda b, c: (TR_MV, 0)),
            pl.BlockSpec((N_META, LANES), lambda b, c: (0, 0)),
            pl.BlockSpec((N_GATE_COLS, N_META), lambda b, c: (0, 0)),
            pl.BlockSpec((1, LANES), lambda b, c: (0, 0)),
            pl.BlockSpec((ML_WIDTH, ML_CHUNK), lambda b, c: (0, rowblk(b, c))),
            pl.BlockSpec((ML_WIDTH, ML_CHUNK), lambda b, c: (TR_MO, rowblk(b, c))),
            pl.BlockSpec((ML_WIDTH, ML_CHUNK), lambda b, c: (0, 0)),
        ]
        args += [k_m, tr_m, gates_m, gates_m_t, bias_row, hb, tr_x, gain_b]
        out_dtype = BF16
    return pl.pallas_call(
        functools.partial(_mlstm_kernel, reverse),
        grid=(bsz, nc),
        in_specs=in_specs,
        out_specs=pl.BlockSpec((ML_WIDTH, ML_CHUNK), lambda b, c: (0, rowblk(b, c))),
        out_shape=jax.ShapeDtypeStruct((ML_WIDTH, bsz * seq), out_dtype),
        scratch_shapes=[
            pltpu.VMEM((ML_HEADS, ML_V_DIM, ML_QK_DIM), F32),
            pltpu.VMEM((ML_HEADS, 1, ML_QK_DIM), F32),
            pltpu.VMEM((ML_HEADS, SUBLANES, LANES), F32),
        ],
        compiler_params=_cparams(("parallel", "arbitrary")),
        name="mlstm_bwd" if reverse else "mlstm_fwd",
    )(*args)


def _mix_ffn_kernel(a_ref, mt_ref, x_ref, wo_ref, g2_ref, wg_ref, wu_ref, wd_ref, gf_ref,
                    o_ref, u_ref):
    j = pl.program_id(1)

    @pl.when(j == 0)
    def _():
        h = (_dot(a_ref[...], wo_ref[0:DA_WIDTH, :])
             + _dot_tn(mt_ref[...], wo_ref[DA_WIDTH:DA_WIDTH + ML_WIDTH, :]) + x_ref[...])
        o_ref[...] = h
        u = h * lax.rsqrt(jnp.mean(h * h, axis=-1, keepdims=True) + EPS) * g2_ref[...]
        u_ref[...] = u.astype(BF16)

    u = u_ref[...]
    g = _dot(u, wg_ref[...])
    ff = (g * _sigmoid(g) * _dot(u, wu_ref[...])).astype(BF16)
    o_ref[...] += _dot(ff, wd_ref[...])

    @pl.when(j == pl.num_programs(1) - 1)
    def _():
        y = o_ref[...]
        o_ref[...] = y * lax.rsqrt(jnp.mean(y * y, axis=-1, keepdims=True) + EPS) * gf_ref[...]


def _mix_ffn(attn, ml_t, x_rows, w_out, norm_ffn_g, w_gate, w_up, w_down, norm_final_g):
    m = x_rows.shape[0]
    return pl.pallas_call(
        _mix_ffn_kernel,
        grid=(m // FFN_TM, D_FF // FFN_TF),
        in_specs=[
            pl.BlockSpec((FFN_TM, DA_WIDTH), lambda i, j: (i, 0)),
            pl.BlockSpec((ML_WIDTH, FFN_TM), lambda i, j: (0, i)),
            pl.BlockSpec((FFN_TM, D_MODEL), lambda i, j: (i, 0)),
            pl.BlockSpec((DA_WIDTH + ML_WIDTH, D_MODEL), lambda i, j: (0, 0)),
            pl.BlockSpec((1, D_MODEL), lambda i, j: (0, 0)),
            pl.BlockSpec((D_MODEL, FFN_TF), lambda i, j: (0, j)),
            pl.BlockSpec((D_MODEL, FFN_TF), lambda i, j: (0, j)),
            pl.BlockSpec((FFN_TF, D_MODEL), lambda i, j: (j, 0)),
            pl.BlockSpec((1, D_MODEL), lambda i, j: (0, 0)),
        ],
        out_specs=pl.BlockSpec((FFN_TM, D_MODEL), lambda i, j: (i, 0)),
        out_shape=jax.ShapeDtypeStruct((m, D_MODEL), F32),
        scratch_shapes=[pltpu.VMEM((FFN_TM, D_MODEL), BF16)],
        compiler_params=_cparams(("parallel", "arbitrary")),
        name="mix_ffn",
    )(attn, ml_t, x_rows, w_out, norm_ffn_g, w_gate, w_up, w_down, norm_final_g)


def _rope_tables(pos0, n):
    lane = jnp.arange(LANES) % DA_QK_DIM
    half = DA_ROT_DIM // 2
    inv = ROPE_THETA ** (-jnp.arange(0, DA_ROT_DIM, 2, dtype=F32) / DA_ROT_DIM)
    inv_lane = jnp.where(lane < DA_ROT_DIM, inv[lane % half], 0.0)
    pos = jnp.arange(pos0, pos0 + n, dtype=F32)
    ang = pos[:, None] * inv_lane[None, :]
    sin = jnp.sin(ang)
    sa = jnp.where(lane < half, -sin, 0.0)
    sb = jnp.where((lane >= half) & (lane < DA_ROT_DIM), sin, 0.0)
    return jnp.cos(ang), sa, sb


def _pad_lanes(a, width=LANES):
    return jnp.pad(a, ((0, 0), (0, width - a.shape[1])))


def kernel(x, meta_tokens, norm_mix, w_in, da_lambda_q1, da_lambda_k1, da_lambda_q2, da_lambda_k2,
           da_head_norm, ml_conv_w, ml_conv_b, ml_gate_bias, ml_head_norm, w_out, norm_ffn,
           w_gate, w_up, w_down, norm_final):
    bsz, seq, _ = x.shape
    x_rows = x.reshape(bsz * seq, D_MODEL)

    main_cols = N_NAT * IN_TN + N_TR * IN_TN
    w_main = w_in[0].astype(BF16)
    w_gates = _pad_lanes(w_in[0][:, main_cols:]).astype(BF16)
    norm_mix0 = norm_mix[0][None, :]

    tabs_x = _rope_tables(N_META, seq)
    tabs_m = _rope_tables(0, N_META)
    nat_x, tr_x, _, gates_t = _inproj(x_rows, norm_mix0, w_main, w_gates, *tabs_x, tm=IN_TM)
    nat_m, tr_m, gates_m, gates_m_t = _inproj(meta_tokens, norm_mix0, w_main, w_gates, *tabs_m,
                                              tm=N_META)

    conv_w = jnp.pad(ml_conv_w[0], ((0, SUBLANES - CONV_W), (0, 0)))
    q_t, k_x, k_m = _conv(nat_x, nat_m, conv_w, ml_conv_b[0][None, :], bsz, seq)

    lam_params = jnp.pad(
        jnp.stack([da_lambda_q1[0], da_lambda_k1[0], da_lambda_q2[0], da_lambda_k2[0]]),
        ((0, SUBLANES - 4), (0, LANES - DA_QK_DIM)))
    attn = _attention(lam_params, nat_x, nat_m, tr_x, tr_m,
                      da_head_norm[0].reshape(1, DA_WIDTH), bsz, seq)

    bias = ml_gate_bias[0].reshape(N_GATE_COLS)
    bias_row = _pad_lanes(bias[None, :])
    bias_col = bias[:, None]
    gain_b = jnp.broadcast_to(ml_head_norm[0].reshape(ML_WIDTH, 1), (ML_WIDTH, ML_CHUNK))
    h_bwd = _mlstm(True, q_t, k_x, tr_x, gates_t, bias_col, bsz, seq)
    ml_t = _mlstm(False, q_t, k_x, tr_x, gates_t, bias_col, bsz, seq,
                  fwd_extra=(k_m, tr_m, gates_m, gates_m_t, bias_row, h_bwd, gain_b))

    out = _mix_ffn(attn, ml_t, x_rows, w_out[0].astype(BF16), norm_ffn[0][None, :],
                   w_gate[0].astype(BF16), w_up[0].astype(BF16), w_down[0].astype(BF16),
                   norm_final[None, :])
    return out.reshape(bsz, seq, D_MODEL)
```

```python
import jax
import jax.numpy as jnp
from jax import lax
from jax.experimental import pallas as pl
from jax.experimental.pallas import tpu as pltpu

F32 = jnp.float32
BF16 = jnp.bfloat16

D_MODEL = 2048
N_META = 16
EPS = 1e-6
ROPE_THETA = 500000.0
NEG = -1e30

DA_QK_DIM = 64
DA_V_DIM = 128
DA_HEADS = 8
DA_ROT_DIM = 16
DA_WIDTH = DA_HEADS * DA_V_DIM
LAM_INIT = 0.8 - 0.6 * 1.0
LOG2_E = 1.4426950408889634

ML_HEADS = 4
ML_V_DIM = 256
ML_QK_DIM = 128
ML_WIDTH = ML_HEADS * ML_V_DIM
ML_QK_WIDTH = ML_HEADS * ML_QK_DIM
CONV_W = 5
N_GATES = 4
N_GATE_COLS = N_GATES * ML_HEADS

D_FF = 5632

IN_TN = 1024
NAT_AQ, NAT_AK, NAT_MQK = 0, 1, 2
TR_AV, TR_MV, TR_MO = 0, 1, 2
N_NAT = 3
N_TR = 3

LANES = 128
SUBLANES = 8
VMEM_LIMIT = 56 * 1024 * 1024

IN_TM = 512
ATT_TQ = 512
ATT_KB = 512
ML_CHUNK = 256
OUT_TM = 256
FFN_TM = 512
FFN_TF = 512
CONV_ROWS = 512


def _cparams(sem):
    return pltpu.CompilerParams(dimension_semantics=sem, vmem_limit_bytes=VMEM_LIMIT)


def _dot(a, b):
    return jnp.dot(a, b, preferred_element_type=F32)


def _dot_nt(a, b):
    return lax.dot_general(a, b, (((1,), (1,)), ((), ())), preferred_element_type=F32)


def _dot_tn(a, b):
    return lax.dot_general(a, b, (((0,), (0,)), ((), ())), preferred_element_type=F32)


def _bf16_pieces(x):
    hi = x.astype(BF16)
    rest = x - hi.astype(F32)
    mid = rest.astype(BF16)
    return hi, mid, (rest - mid.astype(F32)).astype(BF16)


def _cumsum_rows(x, visible):
    r = x.shape[0]
    ones = jnp.where(visible, 1.0, 0.0).astype(BF16)
    parts = _dot(jnp.concatenate(_bf16_pieces(x), axis=0), ones)
    return parts[0:r] + parts[r:2 * r] + parts[2 * r:3 * r]


def _cumsum_cols(x, visible):
    c = x.shape[1]
    ones = jnp.where(visible, 1.0, 0.0).astype(BF16)
    parts = _dot(ones, jnp.concatenate(_bf16_pieces(x), axis=1))
    return parts[:, 0:c] + parts[:, c:2 * c] + parts[:, 2 * c:3 * c]


def _sigmoid(x):
    return 1.0 / (1.0 + jnp.exp(-x))


def _log_sigmoid(x):
    return jnp.minimum(x, 0.0) - jnp.log(1.0 + jnp.exp(-jnp.abs(x)))


_NAT_COLS = (0, 1024, 3072)
_TR_COLS = (2048, 4096, 5120)


def _inproj_kernel(x_ref, g_ref, w_ref, wg_ref, cos_ref, sa_ref, sb_ref,
                   nat_ref, tr_ref, gate_ref, gate_t_ref):
    x = x_ref[...]
    ms = jnp.mean(x * x, axis=-1, keepdims=True)
    u = (x * lax.rsqrt(ms + EPS) * g_ref[...]).astype(BF16)
    gates = _dot(u, wg_ref[...])
    gate_ref[...] = gates
    gate_t_ref[...] = gates.T[0:N_GATE_COLS, :]

    def tile(col0):
        return _dot(u, w_ref[0, :, col0:col0 + IN_TN])

    for jj in (NAT_AQ, NAT_AK):
        acc = tile(_NAT_COLS[jj])
        scale = DA_QK_DIM ** -0.5 * LOG2_E if jj == NAT_AQ else 1.0
        cos = cos_ref[...] * scale
        sa = sa_ref[...] * scale
        sb = sb_ref[...] * scale
        for h in range(DA_HEADS):
            t = acc[:, h * LANES:(h + 1) * LANES]
            r = (t * cos + pltpu.roll(t, LANES - DA_ROT_DIM // 2, 1) * sa
                 + pltpu.roll(t, DA_ROT_DIM // 2, 1) * sb)
            nat_ref[:, jj * IN_TN + h * LANES:jj * IN_TN + (h + 1) * LANES] = r.astype(BF16)

    nat_ref[:, NAT_MQK * IN_TN:(NAT_MQK + 1) * IN_TN] = tile(_NAT_COLS[NAT_MQK]).astype(BF16)

    for jj in range(N_TR):
        tr_ref[jj * IN_TN:(jj + 1) * IN_TN, :] = tile(_TR_COLS[jj]).T.astype(BF16)


def _inproj(rows, norm_g, w_main, w_gate, cos_t, sa_t, sb_t, tm):
    m = rows.shape[0]
    tiles_per_seq = cos_t.shape[0] // tm
    tab_spec = pl.BlockSpec((tm, LANES), lambda i: (i % tiles_per_seq, 0))
    return pl.pallas_call(
        _inproj_kernel,
        grid=(m // tm,),
        in_specs=[
            pl.BlockSpec((tm, D_MODEL), lambda i: (i, 0)),
            pl.BlockSpec((1, D_MODEL), lambda i: (0, 0)),
            pl.BlockSpec(w_main.shape, lambda i: (0, 0, 0)),
            pl.BlockSpec((D_MODEL, LANES), lambda i: (0, 0)),
            tab_spec, tab_spec, tab_spec,
        ],
        out_specs=[
            pl.BlockSpec((tm, N_NAT * IN_TN), lambda i: (i, 0)),
            pl.BlockSpec((N_TR * IN_TN, tm), lambda i: (0, i)),
            pl.BlockSpec((tm, LANES), lambda i: (i, 0)),
            pl.BlockSpec((N_GATE_COLS, tm), lambda i: (0, i)),
        ],
        out_shape=[
            jax.ShapeDtypeStruct((m, N_NAT * IN_TN), BF16),
            jax.ShapeDtypeStruct((N_TR * IN_TN, m), BF16),
            jax.ShapeDtypeStruct((m, LANES), F32),
            jax.ShapeDtypeStruct((N_GATE_COLS, m), F32),
        ],
        compiler_params=_cparams(("parallel",)),
        name="inproj",
    )(rows, norm_g, w_main, w_gate, cos_t, sa_t, sb_t)


_CONV_PAD = SUBLANES


def _conv_kernel(x_ref, m_ref, w_ref, b_ref, qt_ref, k_ref, km_ref, s_ref):
    seq = x_ref.shape[0]
    j = pl.program_id(1)
    zeros = jnp.zeros((_CONV_PAD, LANES), F32)
    s_ref[0:_CONV_PAD, :] = zeros
    s_ref[_CONV_PAD:_CONV_PAD + N_META, :] = m_ref[...].astype(F32)
    s_ref[_CONV_PAD + N_META:_CONV_PAD + N_META + seq, :] = x_ref[...].astype(F32)
    s_ref[_CONV_PAD + N_META + seq:, :] = zeros
    w = w_ref[...]
    bias = b_ref[...]

    def conv(pos, n):
        acc = jnp.zeros((n, LANES), F32) + bias
        for t in range(CONV_W):
            start = _CONV_PAD + pos + t - CONV_W // 2
            acc = acc + s_ref[start:start + n, :] * w[t:t + 1, :]
        return acc * _sigmoid(acc)

    @pl.when(j < ML_HEADS)
    def _():
        for c in range(seq // CONV_ROWS):
            y = conv(N_META + c * CONV_ROWS, CONV_ROWS) * (ML_QK_DIM ** -0.5)
            qt_ref[:, c * CONV_ROWS:(c + 1) * CONV_ROWS] = y.T.astype(BF16)

    @pl.when(j >= ML_HEADS)
    def _():
        km_ref[0] = conv(0, N_META).astype(BF16)
        for c in range(seq // CONV_ROWS):
            k_ref[c * CONV_ROWS:(c + 1) * CONV_ROWS, :] = conv(
                N_META + c * CONV_ROWS, CONV_ROWS).astype(BF16)


def _conv(nat_x, nat_m, conv_w, conv_b, bsz, seq):
    col0 = NAT_MQK * (IN_TN // LANES)
    q_j = lambda j: jnp.minimum(j, ML_HEADS - 1)
    k_j = lambda j: jnp.maximum(j - ML_HEADS, 0)
    return pl.pallas_call(
        _conv_kernel,
        grid=(bsz, 2 * ML_HEADS),
        in_specs=[
            pl.BlockSpec((seq, LANES), lambda b, j: (b, col0 + j)),
            pl.BlockSpec((N_META, LANES), lambda b, j: (0, col0 + j)),
            pl.BlockSpec((SUBLANES, LANES), lambda b, j: (0, j)),
            pl.BlockSpec((1, LANES), lambda b, j: (0, j)),
        ],
        out_specs=[
            pl.BlockSpec((ML_QK_DIM, seq), lambda b, j: (q_j(j), b)),
            pl.BlockSpec((seq, ML_QK_DIM), lambda b, j: (b, k_j(j))),
            pl.BlockSpec((1, N_META, ML_QK_DIM), lambda b, j: (b, 0, k_j(j))),
        ],
        out_shape=[
            jax.ShapeDtypeStruct((ML_QK_WIDTH, bsz * seq), BF16),
            jax.ShapeDtypeStruct((bsz * seq, ML_QK_WIDTH), BF16),
            jax.ShapeDtypeStruct((bsz, N_META, ML_QK_WIDTH), BF16),
        ],
        scratch_shapes=[pltpu.VMEM((seq + N_META + 2 * _CONV_PAD, LANES), F32)],
        compiler_params=_cparams(("parallel", "arbitrary")),
        name="mlstm_conv",
    )(nat_x, nat_m, conv_w, conv_b)


_ATT_ACC_ROWS = DA_V_DIM + 16


def _attn_kernel(lam_ref, q_ref, k_ref, vt_ref, km_ref, vmt_ref, gain_ref, o_ref,
                 s_ref, sm_ref, acc_ref, o_t_ref, vte_ref, vmte_ref):
    tq = ATT_TQ
    nq = q_ref.shape[0] // tq
    nkb = k_ref.shape[0] // ATT_KB
    lp = lam_ref[...]
    lam = (jnp.exp(jnp.sum(lp[0:1] * lp[1:2], axis=1, keepdims=True))
           - jnp.exp(jnp.sum(lp[2:3] * lp[3:4], axis=1, keepdims=True)) + LAM_INIT)
    lane = lax.broadcasted_iota(jnp.int32, (tq, LANES), 1)
    comps = range(2)

    def q_rows(qt):
        return pl.ds(pl.multiple_of(qt * tq, tq), tq)

    def masked_q(qt):
        q = q_ref[q_rows(qt), :]
        zero = jnp.zeros_like(q)
        return (jnp.where(lane < DA_QK_DIM, q, zero), jnp.where(lane >= DA_QK_DIM, q, zero))

    def fold(a):
        return a.reshape(a.shape[0] // SUBLANES, SUBLANES, tq)

    def scores(qz, kb, slot):
        k = k_ref[kb * ATT_KB:(kb + 1) * ATT_KB, :]
        bmax = []
        for c in comps:
            s = _dot_nt(k, qz[c])
            s_ref[slot, c] = s
            bm = jnp.max(fold(s), axis=0)
            if kb == 0:
                sm = _dot_nt(km_ref[...], qz[c])
                sm_ref[c] = sm
                bm = jnp.maximum(bm, jnp.max(fold(sm), axis=0))
            bmax.append(bm)
        return tuple(bmax)

    def probs(s, m):
        return jnp.exp2(s - m).astype(BF16)

    def consume(kb, slot, bmax, m):
        vte = vte_ref[:, kb * ATT_KB:(kb + 1) * ATT_KB]
        m_out = []
        for c in comps:
            bm = jnp.max(bmax[c], axis=0, keepdims=True)
            if kb == 0:
                m_new = bm
                acc_ref[c] = (_dot(vte, probs(s_ref[slot, c], m_new))
                              + _dot(vmte_ref[...], probs(sm_ref[c], m_new)))
            else:
                m_new = jnp.maximum(m[c], bm)
                alpha = jnp.exp2(m[c] - m_new)
                acc_ref[c] = alpha * acc_ref[c] + _dot(vte, probs(s_ref[slot, c], m_new))
            m_out.append(m_new)
        return tuple(m_out)

    def finalize(qt):
        o = o_t_ref[...].T
        y = (o * lax.rsqrt(jnp.mean(o * o, axis=1, keepdims=True) + EPS)
             * gain_ref[...] * (1.0 - LAM_INIT))
        o_ref[q_rows(qt), :] = y.astype(BF16)

    def tile(qt, bmax):
        finalize(jnp.maximum(qt - 1, 0))
        qz = masked_q(qt)
        m = None
        for kb in range(nkb):
            if kb + 1 < nkb:
                bmax_next = scores(qz, kb + 1, (kb + 1) % 2)
            else:
                bmax_next = scores(masked_q(jnp.minimum(qt + 1, nq - 1)), 0, 0)
            m = consume(kb, kb % 2, bmax, m)
            bmax = bmax_next
        l1 = acc_ref[0, DA_V_DIM:DA_V_DIM + 1, :]
        l2 = acc_ref[1, DA_V_DIM:DA_V_DIM + 1, :]
        o_t_ref[...] = (acc_ref[0, 0:DA_V_DIM, :] / l1
                        - lam * (acc_ref[1, 0:DA_V_DIM, :] / l2))
        return bmax

    ones_row = jnp.where(
        lax.broadcasted_iota(jnp.int32, (_ATT_ACC_ROWS - DA_V_DIM, 1), 0) == 0, 1.0, 0.0)
    vte_ref[0:DA_V_DIM, :] = vt_ref[...]
    vte_ref[DA_V_DIM:, :] = jnp.broadcast_to(ones_row, (_ATT_ACC_ROWS - DA_V_DIM, vte_ref.shape[1])
                                             ).astype(BF16)
    vmte_ref[0:DA_V_DIM, :] = vmt_ref[...]
    vmte_ref[DA_V_DIM:, :] = jnp.broadcast_to(ones_row, (_ATT_ACC_ROWS - DA_V_DIM, N_META)
                                              ).astype(BF16)
    o_t_ref[...] = jnp.zeros_like(o_t_ref)
    lax.fori_loop(0, nq, tile, scores(masked_q(0), 0, 0))
    finalize(nq - 1)


def _attention(lam_params, nat_x, nat_m, tr_x, tr_m, gain, bsz, seq):
    hb = IN_TN // LANES
    return pl.pallas_call(
        _attn_kernel,
        grid=(bsz, DA_HEADS),
        in_specs=[
            pl.BlockSpec((SUBLANES, LANES), lambda b, h: (0, 0)),
            pl.BlockSpec((seq, LANES), lambda b, h: (b, NAT_AQ * hb + h)),
            pl.BlockSpec((seq, LANES), lambda b, h: (b, NAT_AK * hb + h)),
            pl.BlockSpec((DA_V_DIM, seq), lambda b, h: (TR_AV * DA_HEADS + h, b)),
            pl.BlockSpec((N_META, LANES), lambda b, h: (0, NAT_AK * hb + h)),
            pl.BlockSpec((DA_V_DIM, N_META), lambda b, h: (TR_AV * DA_HEADS + h, 0)),
            pl.BlockSpec((1, LANES), lambda b, h: (0, h)),
        ],
        out_specs=pl.BlockSpec((seq, LANES), lambda b, h: (b, h)),
        out_shape=jax.ShapeDtypeStruct((bsz * seq, DA_WIDTH), BF16),
        scratch_shapes=[
            pltpu.VMEM((2, 2, ATT_KB, ATT_TQ), F32),
            pltpu.VMEM((2, N_META, ATT_TQ), F32),
            pltpu.VMEM((2, _ATT_ACC_ROWS, ATT_TQ), F32),
            pltpu.VMEM((DA_V_DIM, ATT_TQ), F32),
            pltpu.VMEM((_ATT_ACC_ROWS, seq), BF16),
            pltpu.VMEM((_ATT_ACC_ROWS, N_META), BF16),
        ],
        compiler_params=_cparams(("parallel", "parallel")),
        name="diff_attention",
    )(lam_params, nat_x, nat_x, tr_x, nat_m, tr_m, gain)


def _ml_state_update(k, vt, a_col, b_end, g_row, ct_ref, n_ref, m_ref):
    m_prev = m_ref[0:1, 0:1]
    m_new = jnp.maximum(b_end + m_prev, jnp.max(g_row, axis=1, keepdims=True))
    decay = jnp.exp(b_end + m_prev - m_new)
    kw = (k.astype(F32) * jnp.exp(b_end + a_col - m_new)).astype(BF16)
    ct_ref[...] = decay * ct_ref[...] + _dot(vt, kw)
    n_ref[...] = decay * n_ref[...] + jnp.sum(kw.astype(F32), axis=0, keepdims=True)
    m_ref[...] = jnp.broadcast_to(m_new, (SUBLANES, LANES))


def _ml_chunk(reverse, qt_ref, k_ref, vt_ref, gt_ref, bc_ref, ct_ref, n_ref, m_ref, dst_ref):
    chunk = k_ref.shape[0]
    gate0 = 2 * ML_HEADS if reverse else 0
    grow = gt_ref[...] + bc_ref[...]
    row = lax.broadcasted_iota(jnp.int32, (chunk, chunk), 0)
    col = lax.broadcasted_iota(jnp.int32, (chunk, chunk), 1)
    mask = (row >= col) if reverse else (row <= col)
    brow_all = _cumsum_rows(_log_sigmoid(grow), mask)
    a_rows = grow - pltpu.roll(brow_all, N_GATE_COLS - ML_HEADS, 0)
    a_cols = jnp.concatenate(
        [a_rows, jnp.zeros((LANES - N_GATE_COLS, chunk), F32)], axis=0).T
    end = 0 if reverse else chunk - 1

    for h in range(ML_HEADS):
        gi, gf = gate0 + h, gate0 + ML_HEADS + h
        k = k_ref[:, h * ML_QK_DIM:(h + 1) * ML_QK_DIM]
        qt = qt_ref[h * ML_QK_DIM:(h + 1) * ML_QK_DIM, :]
        vt = vt_ref[h * ML_V_DIM:(h + 1) * ML_V_DIM, :]
        a_col = a_cols[:, gi:gi + 1]
        b_row = brow_all[gf:gf + 1, :]
        li_row = grow[gi:gi + 1, :]
        b_end = b_row[:, end:end + 1]
        m_prev = m_ref[h][0:1, 0:1]

        d = jnp.where(mask, b_row + a_col, NEG)
        m_inter = b_row + m_prev
        m_t = jnp.maximum(m_inter, jnp.max(d, axis=0, keepdims=True))
        w_inter = jnp.exp(m_inter - m_t)
        sg = _dot(k, qt) * jnp.exp(d - m_t)
        num = w_inter * _dot(ct_ref[h].astype(BF16), qt) + _dot(vt, sg.astype(BF16))
        n16 = jnp.broadcast_to(n_ref[h], (2 * SUBLANES, ML_QK_DIM)).astype(BF16)
        den = w_inter * _dot(n16, qt)[0:1, :] + jnp.sum(sg, axis=0, keepdims=True)
        dst_ref[h * ML_V_DIM:(h + 1) * ML_V_DIM, :] = (
            num / jnp.maximum(jnp.abs(den), jnp.exp(-m_t)))

        _ml_state_update(k, vt, a_col, b_end, b_end - b_row + li_row,
                         ct_ref.at[h], n_ref.at[h], m_ref.at[h])


def _mlstm_kernel(qtf_ref, kf_ref, vtf_ref, gtf_ref, opf_ref,
                  qtb_ref, kb_ref, vtb_ref, gtb_ref, opb_ref,
                  bc_ref, km_ref, vmt_ref, gm_ref, gmt_ref, br_ref, gain_ref,
                  out_ref, ct_ref, n_ref, m_ref, hf_ref, hb_ref):
    c = pl.program_id(1)
    nc = pl.num_programs(1)
    half = hf_ref.shape[0] - 1

    @pl.when(c == 0)
    def _():
        ct_ref[...] = jnp.zeros_like(ct_ref)
        n_ref[...] = jnp.zeros_like(n_ref)
        m_ref[...] = jnp.zeros_like(m_ref)
        gm = gm_ref[...] + br_ref[...]
        gmt = gmt_ref[...] + bc_ref[...]
        r = lax.broadcasted_iota(jnp.int32, (N_META, N_META), 0)
        s = lax.broadcasted_iota(jnp.int32, (N_META, N_META), 1)
        bcm = _cumsum_cols(_log_sigmoid(gm), s <= r)
        brm = _cumsum_rows(_log_sigmoid(gmt), r <= s)
        for h in range(ML_HEADS):
            gi, gf = h, ML_HEADS + h
            b_end = brm[gf:gf + 1, N_META - 1:N_META]
            _ml_state_update(
                km_ref[0][:, h * ML_QK_DIM:(h + 1) * ML_QK_DIM],
                vmt_ref[h * ML_V_DIM:(h + 1) * ML_V_DIM, :],
                gm[:, gi:gi + 1] - bcm[:, gf:gf + 1], b_end,
                b_end - brm[gf:gf + 1, :] + gmt[gi:gi + 1, :],
                ct_ref.at[0, h], n_ref.at[0, h], m_ref.at[0, h])

    slot = jnp.minimum(c, half)
    _ml_chunk(False, qtf_ref, kf_ref, vtf_ref, gtf_ref, bc_ref,
              ct_ref.at[0], n_ref.at[0], m_ref.at[0], hf_ref.at[slot])
    _ml_chunk(True, qtb_ref, kb_ref, vtb_ref, gtb_ref, bc_ref,
              ct_ref.at[1], n_ref.at[1], m_ref.at[1], hb_ref.at[slot])

    def finish(ht, op_ref, chunk_idx):
        for h in range(ML_HEADS):
            rows = slice(h * ML_V_DIM, (h + 1) * ML_V_DIM)
            hh = ht[rows, :]
            y = hh * lax.rsqrt(jnp.mean(hh * hh, axis=0, keepdims=True) + EPS) * gain_ref[rows, :]
            out_ref[chunk_idx, rows, :] = (_sigmoid(op_ref[rows, :].astype(F32)) * y).astype(BF16)

    @pl.when(c >= nc - half)
    def _():
        partner = nc - 1 - c
        finish(hf_ref[slot] + hb_ref[partner], opf_ref, c)
        finish(hb_ref[slot] + hf_ref[partner], opb_ref, partner)


def _mlstm(q_t, k_x, tr_x, gates_t, bias_col, k_m, tr_m, gates_m, gates_m_t, bias_row, gain_b,
           bsz, seq):
    nc = seq // ML_CHUNK
    half = nc // 2
    fwd = lambda b, c: b * nc + c
    bwd = lambda b, c: b * nc + (nc - 1 - c)

    def chunk_specs(blk):
        return [
            pl.BlockSpec((ML_QK_WIDTH, ML_CHUNK), lambda b, c: (0, blk(b, c))),
            pl.BlockSpec((ML_CHUNK, ML_QK_WIDTH), lambda b, c: (blk(b, c), 0)),
            pl.BlockSpec((ML_WIDTH, ML_CHUNK), lambda b, c: (TR_MV, blk(b, c))),
            pl.BlockSpec((N_GATE_COLS, ML_CHUNK), lambda b, c: (0, blk(b, c))),
            pl.BlockSpec((ML_WIDTH, ML_CHUNK), lambda b, c: (TR_MO, blk(b, c))),
        ]

    chunk_args = [q_t, k_x, tr_x, gates_t, tr_x]
    in_specs = chunk_specs(fwd) + chunk_specs(bwd) + [
        pl.BlockSpec((N_GATE_COLS, 1), lambda b, c: (0, 0)),
        pl.BlockSpec((1, N_META, ML_QK_WIDTH), lambda b, c: (b, 0, 0)),
        pl.BlockSpec((ML_WIDTH, N_META), lambda b, c: (TR_MV, 0)),
        pl.BlockSpec((N_META, LANES), lambda b, c: (0, 0)),
        pl.BlockSpec((N_GATE_COLS, N_META), lambda b, c: (0, 0)),
        pl.BlockSpec((1, LANES), lambda b, c: (0, 0)),
        pl.BlockSpec((ML_WIDTH, ML_CHUNK), lambda b, c: (0, 0)),
    ]
    return pl.pallas_call(
        _mlstm_kernel,
        grid=(bsz, nc),
        in_specs=in_specs,
        out_specs=pl.BlockSpec((nc, ML_WIDTH, ML_CHUNK), lambda b, c: (b, 0, 0)),
        out_shape=jax.ShapeDtypeStruct((bsz * nc, ML_WIDTH, ML_CHUNK), BF16),
        scratch_shapes=[
            pltpu.VMEM((2, ML_HEADS, ML_V_DIM, ML_QK_DIM), F32),
            pltpu.VMEM((2, ML_HEADS, 1, ML_QK_DIM), F32),
            pltpu.VMEM((2, ML_HEADS, SUBLANES, LANES), F32),
            pltpu.VMEM((half + 1, ML_WIDTH, ML_CHUNK), F32),
            pltpu.VMEM((half + 1, ML_WIDTH, ML_CHUNK), F32),
        ],
        compiler_params=_cparams(("parallel", "arbitrary")),
        name="mlstm",
    )(*chunk_args, *chunk_args, bias_col, k_m, tr_m, gates_m, gates_m_t, bias_row, gain_b)


def _mix_ffn_kernel(a_ref, mt_ref, x_ref, wo_ref, g2_ref, wg_ref, wu_ref, wd_ref, gf_ref,
                    o_ref, u_ref):
    j = pl.program_id(1)

    @pl.when(j == 0)
    def _():
        w_m = wo_ref[DA_WIDTH:DA_WIDTH + ML_WIDTH, :]
        ml = jnp.concatenate([_dot_tn(mt_ref[s], w_m) for s in range(mt_ref.shape[0])], axis=0)
        h = _dot(a_ref[...], wo_ref[0:DA_WIDTH, :]) + ml + x_ref[...]
        o_ref[...] = h
        u = h * lax.rsqrt(jnp.mean(h * h, axis=-1, keepdims=True) + EPS) * g2_ref[...]
        u_ref[...] = u.astype(BF16)

    u = u_ref[...]
    g = _dot(u, wg_ref[...])
    ff = (g * _sigmoid(g) * _dot(u, wu_ref[...])).astype(BF16)
    o_ref[...] += _dot(ff, wd_ref[...])

    @pl.when(j == pl.num_programs(1) - 1)
    def _():
        y = o_ref[...]
        o_ref[...] = y * lax.rsqrt(jnp.mean(y * y, axis=-1, keepdims=True) + EPS) * gf_ref[...]


def _mix_ffn(attn, ml_t, x_rows, w_out, norm_ffn_g, w_gate, w_up, w_down, norm_final_g):
    m = x_rows.shape[0]
    return pl.pallas_call(
        _mix_ffn_kernel,
        grid=(m // FFN_TM, D_FF // FFN_TF),
        in_specs=[
            pl.BlockSpec((FFN_TM, DA_WIDTH), lambda i, j: (i, 0)),
            pl.BlockSpec((FFN_TM // ML_CHUNK, ML_WIDTH, ML_CHUNK), lambda i, j: (i, 0, 0)),
            pl.BlockSpec((FFN_TM, D_MODEL), lambda i, j: (i, 0)),
            pl.BlockSpec((DA_WIDTH + ML_WIDTH, D_MODEL), lambda i, j: (0, 0)),
            pl.BlockSpec((1, D_MODEL), lambda i, j: (0, 0)),
            pl.BlockSpec((D_MODEL, FFN_TF), lambda i, j: (0, j)),
            pl.BlockSpec((D_MODEL, FFN_TF), lambda i, j: (0, j)),
            pl.BlockSpec((FFN_TF, D_MODEL), lambda i, j: (j, 0)),
            pl.BlockSpec((1, D_MODEL), lambda i, j: (0, 0)),
        ],
        out_specs=pl.BlockSpec((FFN_TM, D_MODEL), lambda i, j: (i, 0)),
        out_shape=jax.ShapeDtypeStruct((m, D_MODEL), F32),
        scratch_shapes=[pltpu.VMEM((FFN_TM, D_MODEL), BF16)],
        compiler_params=_cparams(("parallel", "arbitrary")),
        name="mix_ffn",
    )(attn, ml_t, x_rows, w_out, norm_ffn_g, w_gate, w_up, w_down, norm_final_g)


def _rope_tables(pos0, n):
    lane = jnp.arange(LANES) % DA_QK_DIM
    half = DA_ROT_DIM // 2
    inv = ROPE_THETA ** (-jnp.arange(0, DA_ROT_DIM, 2, dtype=F32) / DA_ROT_DIM)
    inv_lane = jnp.where(lane < DA_ROT_DIM, inv[lane % half], 0.0)
    pos = jnp.arange(pos0, pos0 + n, dtype=F32)
    ang = pos[:, None] * inv_lane[None, :]
    sin = jnp.sin(ang)
    sa = jnp.where(lane < half, -sin, 0.0)
    sb = jnp.where((lane >= half) & (lane < DA_ROT_DIM), sin, 0.0)
    return jnp.cos(ang), sa, sb


def _pad_lanes(a, width=LANES):
    return jnp.pad(a, ((0, 0), (0, width - a.shape[1])))


def kernel(x, meta_tokens, norm_mix, w_in, da_lambda_q1, da_lambda_k1, da_lambda_q2, da_lambda_k2,
           da_head_norm, ml_conv_w, ml_conv_b, ml_gate_bias, ml_head_norm, w_out, norm_ffn,
           w_gate, w_up, w_down, norm_final):
    bsz, seq, _ = x.shape
    x_rows = x.reshape(bsz * seq, D_MODEL)

    main_cols = N_NAT * IN_TN + N_TR * IN_TN
    w_main = w_in.astype(BF16)
    w_gates = _pad_lanes(w_in[0][:, main_cols:]).astype(BF16)
    norm_mix0 = norm_mix[0][None, :]

    tabs_x = _rope_tables(N_META, seq)
    tabs_m = _rope_tables(0, N_META)
    nat_x, tr_x, _, gates_t = _inproj(x_rows, norm_mix0, w_main, w_gates, *tabs_x, tm=IN_TM)
    nat_m, tr_m, gates_m, gates_m_t = _inproj(meta_tokens, norm_mix0, w_main, w_gates, *tabs_m,
                                              tm=N_META)

    conv_w = jnp.pad(ml_conv_w[0], ((0, SUBLANES - CONV_W), (0, 0)))
    q_t, k_x, k_m = _conv(nat_x, nat_m, conv_w, ml_conv_b[0][None, :], bsz, seq)

    lam_params = jnp.pad(
        jnp.stack([da_lambda_q1[0], da_lambda_k1[0], da_lambda_q2[0], da_lambda_k2[0]]),
        ((0, SUBLANES - 4), (0, LANES - DA_QK_DIM)))
    attn = _attention(lam_params, nat_x, nat_m, tr_x, tr_m,
                      da_head_norm[0].reshape(1, DA_WIDTH), bsz, seq)

    bias = ml_gate_bias[0].reshape(N_GATE_COLS)
    bias_row = _pad_lanes(bias[None, :])
    bias_col = bias[:, None]
    gain_b = jnp.broadcast_to(ml_head_norm[0].reshape(ML_WIDTH, 1), (ML_WIDTH, ML_CHUNK))
    ml_t = _mlstm(q_t, k_x, tr_x, gates_t, bias_col, k_m, tr_m, gates_m, gates_m_t, bias_row,
                  gain_b, bsz, seq)

    out = _mix_ffn(attn, ml_t, x_rows, w_out[0].astype(BF16), norm_ffn[0][None, :],
                   w_gate[0].astype(BF16), w_up[0].astype(BF16), w_down[0].astype(BF16),
                   norm_final[None, :])
    return out.reshape(bsz, seq, D_MODEL)
```

```python
import jax
import jax.numpy as jnp
from jax import lax
from jax.experimental import pallas as pl
from jax.experimental.pallas import tpu as pltpu

F32 = jnp.float32
BF16 = jnp.bfloat16

D_MODEL = 2048
N_META = 16
EPS = 1e-6
ROPE_THETA = 500000.0
NEG = -1e30

DA_QK_DIM = 64
DA_V_DIM = 128
DA_HEADS = 8
DA_ROT_DIM = 16
DA_WIDTH = DA_HEADS * DA_V_DIM
LAM_INIT = 0.8 - 0.6 * 1.0
LOG2_E = 1.4426950408889634

ML_HEADS = 4
ML_V_DIM = 256
ML_QK_DIM = 128
ML_WIDTH = ML_HEADS * ML_V_DIM
ML_QK_WIDTH = ML_HEADS * ML_QK_DIM
CONV_W = 5
N_GATES = 4
N_GATE_COLS = N_GATES * ML_HEADS

D_FF = 5632

IN_TN = 1024
NAT_AQ, NAT_AK, NAT_MQK = 0, 1, 2
TR_AV, TR_MV, TR_MO = 0, 1, 2
N_NAT = 3
N_TR = 3

LANES = 128
SUBLANES = 8
VMEM_LIMIT = 56 * 1024 * 1024

IN_TM = 512
ATT_TQ = 512
ATT_KB = 512
ML_CHUNK = 256
OUT_TM = 256
FFN_TM = 512
FFN_TF = 512
CONV_ROWS = 512


def _cparams(sem):
    return pltpu.CompilerParams(dimension_semantics=sem, vmem_limit_bytes=VMEM_LIMIT)


def _dot(a, b):
    return jnp.dot(a, b, preferred_element_type=F32)


def _dot_nt(a, b):
    return lax.dot_general(a, b, (((1,), (1,)), ((), ())), preferred_element_type=F32)


def _dot_tn(a, b):
    return lax.dot_general(a, b, (((0,), (0,)), ((), ())), preferred_element_type=F32)


def _bf16_pieces(x):
    hi = x.astype(BF16)
    rest = x - hi.astype(F32)
    mid = rest.astype(BF16)
    return hi, mid, (rest - mid.astype(F32)).astype(BF16)


def _cumsum_rows(x, visible):
    r = x.shape[0]
    ones = jnp.where(visible, 1.0, 0.0).astype(BF16)
    parts = _dot(jnp.concatenate(_bf16_pieces(x), axis=0), ones)
    return parts[0:r] + parts[r:2 * r] + parts[2 * r:3 * r]


def _cumsum_cols(x, visible):
    c = x.shape[1]
    ones = jnp.where(visible, 1.0, 0.0).astype(BF16)
    parts = _dot(ones, jnp.concatenate(_bf16_pieces(x), axis=1))
    return parts[:, 0:c] + parts[:, c:2 * c] + parts[:, 2 * c:3 * c]


def _sigmoid(x):
    return 1.0 / (1.0 + jnp.exp(-x))


def _log_sigmoid(x):
    return jnp.minimum(x, 0.0) - jnp.log(1.0 + jnp.exp(-jnp.abs(x)))


_NAT_COLS = (0, 1024, 3072)
_TR_COLS = (2048, 4096, 5120)


def _inproj_kernel(x_ref, g_ref, w_ref, wg_ref, cos_ref, sa_ref, sb_ref,
                   nat_ref, tr_ref, gate_ref, gate_t_ref):
    x = x_ref[...]
    ms = jnp.mean(x * x, axis=-1, keepdims=True)
    u = (x * lax.rsqrt(ms + EPS) * g_ref[...]).astype(BF16)
    gates = _dot(u, wg_ref[...])
    gate_ref[...] = gates
    gate_t_ref[...] = gates.T[0:N_GATE_COLS, :]

    def tile(col0):
        return _dot(u, w_ref[0, :, col0:col0 + IN_TN])

    for jj in (NAT_AQ, NAT_AK):
        acc = tile(_NAT_COLS[jj])
        scale = DA_QK_DIM ** -0.5 * LOG2_E if jj == NAT_AQ else 1.0
        cos = cos_ref[...] * scale
        sa = sa_ref[...] * scale
        sb = sb_ref[...] * scale
        for h in range(DA_HEADS):
            t = acc[:, h * LANES:(h + 1) * LANES]
            r = (t * cos + pltpu.roll(t, LANES - DA_ROT_DIM // 2, 1) * sa
                 + pltpu.roll(t, DA_ROT_DIM // 2, 1) * sb)
            nat_ref[:, jj * IN_TN + h * LANES:jj * IN_TN + (h + 1) * LANES] = r.astype(BF16)

    nat_ref[:, NAT_MQK * IN_TN:(NAT_MQK + 1) * IN_TN] = tile(_NAT_COLS[NAT_MQK]).astype(BF16)

    for jj in range(N_TR):
        tr_ref[jj * IN_TN:(jj + 1) * IN_TN, :] = tile(_TR_COLS[jj]).T.astype(BF16)


def _inproj(rows, norm_g, w_main, w_gate, cos_t, sa_t, sb_t, tm):
    m = rows.shape[0]
    tiles_per_seq = cos_t.shape[0] // tm
    tab_spec = pl.BlockSpec((tm, LANES), lambda i: (i % tiles_per_seq, 0))
    return pl.pallas_call(
        _inproj_kernel,
        grid=(m // tm,),
        in_specs=[
            pl.BlockSpec((tm, D_MODEL), lambda i: (i, 0)),
            pl.BlockSpec((1, D_MODEL), lambda i: (0, 0)),
            pl.BlockSpec(w_main.shape, lambda i: (0, 0, 0)),
            pl.BlockSpec((D_MODEL, LANES), lambda i: (0, 0)),
            tab_spec, tab_spec, tab_spec,
        ],
        out_specs=[
            pl.BlockSpec((tm, N_NAT * IN_TN), lambda i: (i, 0)),
            pl.BlockSpec((N_TR * IN_TN, tm), lambda i: (0, i)),
            pl.BlockSpec((tm, LANES), lambda i: (i, 0)),
            pl.BlockSpec((N_GATE_COLS, tm), lambda i: (0, i)),
        ],
        out_shape=[
            jax.ShapeDtypeStruct((m, N_NAT * IN_TN), BF16),
            jax.ShapeDtypeStruct((N_TR * IN_TN, m), BF16),
            jax.ShapeDtypeStruct((m, LANES), F32),
            jax.ShapeDtypeStruct((N_GATE_COLS, m), F32),
        ],
        compiler_params=_cparams(("parallel",)),
        name="inproj",
    )(rows, norm_g, w_main, w_gate, cos_t, sa_t, sb_t)


_CONV_PAD = SUBLANES


def _conv_kernel(x_ref, m_ref, w_ref, b_ref, qt_ref, k_ref, km_ref, s_ref):
    seq = x_ref.shape[0]
    j = pl.program_id(1)
    zeros = jnp.zeros((_CONV_PAD, LANES), F32)
    s_ref[0:_CONV_PAD, :] = zeros
    s_ref[_CONV_PAD:_CONV_PAD + N_META, :] = m_ref[...].astype(F32)
    s_ref[_CONV_PAD + N_META:_CONV_PAD + N_META + seq, :] = x_ref[...].astype(F32)
    s_ref[_CONV_PAD + N_META + seq:, :] = zeros
    w = w_ref[...]
    bias = b_ref[...]

    def conv(pos, n):
        acc = jnp.zeros((n, LANES), F32) + bias
        for t in range(CONV_W):
            start = _CONV_PAD + pos + t - CONV_W // 2
            acc = acc + s_ref[start:start + n, :] * w[t:t + 1, :]
        return acc * _sigmoid(acc)

    @pl.when(j < ML_HEADS)
    def _():
        for c in range(seq // CONV_ROWS):
            y = conv(N_META + c * CONV_ROWS, CONV_ROWS) * (ML_QK_DIM ** -0.5)
            qt_ref[:, c * CONV_ROWS:(c + 1) * CONV_ROWS] = y.T.astype(BF16)

    @pl.when(j >= ML_HEADS)
    def _():
        km_ref[0] = conv(0, N_META).astype(BF16)
        for c in range(seq // CONV_ROWS):
            k_ref[c * CONV_ROWS:(c + 1) * CONV_ROWS, :] = conv(
                N_META + c * CONV_ROWS, CONV_ROWS).astype(BF16)


def _conv(nat_x, nat_m, conv_w, conv_b, bsz, seq):
    col0 = NAT_MQK * (IN_TN // LANES)
    q_j = lambda j: jnp.minimum(j, ML_HEADS - 1)
    k_j = lambda j: jnp.maximum(j - ML_HEADS, 0)
    return pl.pallas_call(
        _conv_kernel,
        grid=(bsz, 2 * ML_HEADS),
        in_specs=[
            pl.BlockSpec((seq, LANES), lambda b, j: (b, col0 + j)),
            pl.BlockSpec((N_META, LANES), lambda b, j: (0, col0 + j)),
            pl.BlockSpec((SUBLANES, LANES), lambda b, j: (0, j)),
            pl.BlockSpec((1, LANES), lambda b, j: (0, j)),
        ],
        out_specs=[
            pl.BlockSpec((ML_QK_DIM, seq), lambda b, j: (q_j(j), b)),
            pl.BlockSpec((seq, ML_QK_DIM), lambda b, j: (b, k_j(j))),
            pl.BlockSpec((1, N_META, ML_QK_DIM), lambda b, j: (b, 0, k_j(j))),
        ],
        out_shape=[
            jax.ShapeDtypeStruct((ML_QK_WIDTH, bsz * seq), BF16),
            jax.ShapeDtypeStruct((bsz * seq, ML_QK_WIDTH), BF16),
            jax.ShapeDtypeStruct((bsz, N_META, ML_QK_WIDTH), BF16),
        ],
        scratch_shapes=[pltpu.VMEM((seq + N_META + 2 * _CONV_PAD, LANES), F32)],
        compiler_params=_cparams(("parallel", "arbitrary")),
        name="mlstm_conv",
    )(nat_x, nat_m, conv_w, conv_b)


_ATT_ACC_ROWS = DA_V_DIM + 16


def _attn_kernel(lam_ref, q_ref, k_ref, vt_ref, km_ref, vmt_ref, gain_ref, o_ref,
                 s_ref, sm_ref, acc_ref, o_t_ref, vte_ref, vmte_ref):
    tq = ATT_TQ
    nq = q_ref.shape[0] // tq
    nkb = k_ref.shape[0] // ATT_KB
    lp = lam_ref[...]
    lam = (jnp.exp(jnp.sum(lp[0:1] * lp[1:2], axis=1, keepdims=True))
           - jnp.exp(jnp.sum(lp[2:3] * lp[3:4], axis=1, keepdims=True)) + LAM_INIT)
    lane = lax.broadcasted_iota(jnp.int32, (tq, LANES), 1)
    comps = range(2)

    def q_rows(qt):
        return pl.ds(pl.multiple_of(qt * tq, tq), tq)

    def masked_q(qt):
        q = q_ref[q_rows(qt), :]
        zero = jnp.zeros_like(q)
        return (jnp.where(lane < DA_QK_DIM, q, zero), jnp.where(lane >= DA_QK_DIM, q, zero))

    def fold(a):
        return a.reshape(a.shape[0] // SUBLANES, SUBLANES, tq)

    def scores(qz, kb, slot):
        k = k_ref[kb * ATT_KB:(kb + 1) * ATT_KB, :]
        bmax = []
        for c in comps:
            s = _dot_nt(k, qz[c])
            s_ref[slot, c] = s
            bm = jnp.max(fold(s), axis=0)
            if kb == 0:
                sm = _dot_nt(km_ref[...], qz[c])
                sm_ref[c] = sm
                bm = jnp.maximum(bm, jnp.max(fold(sm), axis=0))
            bmax.append(bm)
        return tuple(bmax)

    def probs(s, m):
        return jnp.exp2(s - m).astype(BF16)

    def consume(kb, slot, bmax, m):
        vte = vte_ref[:, kb * ATT_KB:(kb + 1) * ATT_KB]
        m_out = []
        for c in comps:
            bm = jnp.max(bmax[c], axis=0, keepdims=True)
            if kb == 0:
                m_new = bm
                acc_ref[c] = (_dot(vte, probs(s_ref[slot, c], m_new))
                              + _dot(vmte_ref[...], probs(sm_ref[c], m_new)))
            else:
                m_new = jnp.maximum(m[c], bm)
                alpha = jnp.exp2(m[c] - m_new)
                acc_ref[c] = alpha * acc_ref[c] + _dot(vte, probs(s_ref[slot, c], m_new))
            m_out.append(m_new)
        return tuple(m_out)

    def finalize(qt):
        o = o_t_ref[...].T
        y = (o * lax.rsqrt(jnp.mean(o * o, axis=1, keepdims=True) + EPS)
             * gain_ref[...] * (1.0 - LAM_INIT))
        o_ref[q_rows(qt), :] = y.astype(BF16)

    def tile(qt, bmax):
        finalize(jnp.maximum(qt - 1, 0))
        qz = masked_q(qt)
        m = None
        for kb in range(nkb):
            if kb + 1 < nkb:
                bmax_next = scores(qz, kb + 1, (kb + 1) % 2)
            else:
                bmax_next = scores(masked_q(jnp.minimum(qt + 1, nq - 1)), 0, 0)
            m = consume(kb, kb % 2, bmax, m)
            bmax = bmax_next
        l1 = acc_ref[0, DA_V_DIM:DA_V_DIM + 1, :]
        l2 = acc_ref[1, DA_V_DIM:DA_V_DIM + 1, :]
        o_t_ref[...] = (acc_ref[0, 0:DA_V_DIM, :] / l1
                        - lam * (acc_ref[1, 0:DA_V_DIM, :] / l2))
        return bmax

    ones_row = jnp.where(
        lax.broadcasted_iota(jnp.int32, (_ATT_ACC_ROWS - DA_V_DIM, 1), 0) == 0, 1.0, 0.0)
    vte_ref[0:DA_V_DIM, :] = vt_ref[...]
    vte_ref[DA_V_DIM:, :] = jnp.broadcast_to(ones_row, (_ATT_ACC_ROWS - DA_V_DIM, vte_ref.shape[1])
                                             ).astype(BF16)
    vmte_ref[0:DA_V_DIM, :] = vmt_ref[...]
    vmte_ref[DA_V_DIM:, :] = jnp.broadcast_to(ones_row, (_ATT_ACC_ROWS - DA_V_DIM, N_META)
                                              ).astype(BF16)
    o_t_ref[...] = jnp.zeros_like(o_t_ref)
    lax.fori_loop(0, nq, tile, scores(masked_q(0), 0, 0))
    finalize(nq - 1)


def _attention(lam_params, nat_x, nat_m, tr_x, tr_m, gain, bsz, seq):
    hb = IN_TN // LANES
    return pl.pallas_call(
        _attn_kernel,
        grid=(bsz, DA_HEADS),
        in_specs=[
            pl.BlockSpec((SUBLANES, LANES), lambda b, h: (0, 0)),
            pl.BlockSpec((seq, LANES), lambda b, h: (b, NAT_AQ * hb + h)),
            pl.BlockSpec((seq, LANES), lambda b, h: (b, NAT_AK * hb + h)),
            pl.BlockSpec((DA_V_DIM, seq), lambda b, h: (TR_AV * DA_HEADS + h, b)),
            pl.BlockSpec((N_META, LANES), lambda b, h: (0, NAT_AK * hb + h)),
            pl.BlockSpec((DA_V_DIM, N_META), lambda b, h: (TR_AV * DA_HEADS + h, 0)),
            pl.BlockSpec((1, LANES), lambda b, h: (0, h)),
        ],
        out_specs=pl.BlockSpec((seq, LANES), lambda b, h: (b, h)),
        out_shape=jax.ShapeDtypeStruct((bsz * seq, DA_WIDTH), BF16),
        scratch_shapes=[
            pltpu.VMEM((2, 2, ATT_KB, ATT_TQ), F32),
            pltpu.VMEM((2, N_META, ATT_TQ), F32),
            pltpu.VMEM((2, _ATT_ACC_ROWS, ATT_TQ), F32),
            pltpu.VMEM((DA_V_DIM, ATT_TQ), F32),
            pltpu.VMEM((_ATT_ACC_ROWS, seq), BF16),
            pltpu.VMEM((_ATT_ACC_ROWS, N_META), BF16),
        ],
        compiler_params=_cparams(("parallel", "parallel")),
        name="diff_attention",
    )(lam_params, nat_x, nat_x, tr_x, nat_m, tr_m, gain)


def _ml_state_update(k, vt, a_col, b_end, g_row, ct_ref, n_ref, m_ref):
    m_prev = m_ref[0:1, 0:1]
    m_new = jnp.maximum(b_end + m_prev, jnp.max(g_row, axis=1, keepdims=True))
    decay = jnp.exp(b_end + m_prev - m_new)
    kw = (k.astype(F32) * jnp.exp(b_end + a_col - m_new)).astype(BF16)
    ct_ref[...] = decay * ct_ref[...] + _dot(vt, kw)
    n_ref[...] = decay * n_ref[...] + jnp.sum(kw.astype(F32), axis=0, keepdims=True)
    m_ref[...] = jnp.broadcast_to(m_new, (SUBLANES, LANES))


def _ml_chunk(reverse, qt_ref, k_ref, vt_ref, gt_ref, bc_ref, ct_ref, n_ref, m_ref, dst_ref):
    chunk = k_ref.shape[0]
    gate0 = 2 * ML_HEADS if reverse else 0
    grow = gt_ref[...] + bc_ref[...]
    row = lax.broadcasted_iota(jnp.int32, (chunk, chunk), 0)
    col = lax.broadcasted_iota(jnp.int32, (chunk, chunk), 1)
    mask = (row >= col) if reverse else (row <= col)
    brow_all = _cumsum_rows(_log_sigmoid(grow), mask)
    a_rows = grow - pltpu.roll(brow_all, N_GATE_COLS - ML_HEADS, 0)
    a_cols = jnp.concatenate(
        [a_rows, jnp.zeros((LANES - N_GATE_COLS, chunk), F32)], axis=0).T
    end = 0 if reverse else chunk - 1

    for h in range(ML_HEADS):
        gi, gf = gate0 + h, gate0 + ML_HEADS + h
        k = k_ref[:, h * ML_QK_DIM:(h + 1) * ML_QK_DIM]
        qt = qt_ref[h * ML_QK_DIM:(h + 1) * ML_QK_DIM, :]
        vt = vt_ref[h * ML_V_DIM:(h + 1) * ML_V_DIM, :]
        a_col = a_cols[:, gi:gi + 1]
        b_row = brow_all[gf:gf + 1, :]
        li_row = grow[gi:gi + 1, :]
        b_end = b_row[:, end:end + 1]
        m_prev = m_ref[h][0:1, 0:1]

        d = jnp.where(mask, b_row + a_col, NEG)
        m_inter = b_row + m_prev
        m_t = jnp.maximum(m_inter, jnp.max(d, axis=0, keepdims=True))
        w_inter = jnp.exp(m_inter - m_t)
        sg = _dot(k, qt) * jnp.exp(d - m_t)
        num = w_inter * _dot(ct_ref[h].astype(BF16), qt) + _dot(vt, sg.astype(BF16))
        n16 = jnp.broadcast_to(n_ref[h], (2 * SUBLANES, ML_QK_DIM)).astype(BF16)
        den = w_inter * _dot(n16, qt)[0:1, :] + jnp.sum(sg, axis=0, keepdims=True)
        dst_ref[h * ML_V_DIM:(h + 1) * ML_V_DIM, :] = (
            num / jnp.maximum(jnp.abs(den), jnp.exp(-m_t)))

        _ml_state_update(k, vt, a_col, b_end, b_end - b_row + li_row,
                         ct_ref.at[h], n_ref.at[h], m_ref.at[h])


def _mlstm_kernel(qtf_ref, kf_ref, vtf_ref, gtf_ref, opf_ref,
                  qtb_ref, kb_ref, vtb_ref, gtb_ref, opb_ref,
                  bc_ref, km_ref, vmt_ref, gm_ref, gmt_ref, br_ref, gain_ref,
                  out_ref, ct_ref, n_ref, m_ref, hf_ref, hb_ref):
    c = pl.program_id(1)
    nc = pl.num_programs(1)
    half = hf_ref.shape[0] - 1

    @pl.when(c == 0)
    def _():
        ct_ref[...] = jnp.zeros_like(ct_ref)
        n_ref[...] = jnp.zeros_like(n_ref)
        m_ref[...] = jnp.zeros_like(m_ref)
        gm = gm_ref[...] + br_ref[...]
        gmt = gmt_ref[...] + bc_ref[...]
        r = lax.broadcasted_iota(jnp.int32, (N_META, N_META), 0)
        s = lax.broadcasted_iota(jnp.int32, (N_META, N_META), 1)
        bcm = _cumsum_cols(_log_sigmoid(gm), s <= r)
        brm = _cumsum_rows(_log_sigmoid(gmt), r <= s)
        for h in range(ML_HEADS):
            gi, gf = h, ML_HEADS + h
            b_end = brm[gf:gf + 1, N_META - 1:N_META]
            _ml_state_update(
                km_ref[0][:, h * ML_QK_DIM:(h + 1) * ML_QK_DIM],
                vmt_ref[h * ML_V_DIM:(h + 1) * ML_V_DIM, :],
                gm[:, gi:gi + 1] - bcm[:, gf:gf + 1], b_end,
                b_end - brm[gf:gf + 1, :] + gmt[gi:gi + 1, :],
                ct_ref.at[0, h], n_ref.at[0, h], m_ref.at[0, h])

    slot = jnp.minimum(c, half)
    _ml_chunk(False, qtf_ref, kf_ref, vtf_ref, gtf_ref, bc_ref,
              ct_ref.at[0], n_ref.at[0], m_ref.at[0], hf_ref.at[slot])
    _ml_chunk(True, qtb_ref, kb_ref, vtb_ref, gtb_ref, bc_ref,
              ct_ref.at[1], n_ref.at[1], m_ref.at[1], hb_ref.at[slot])

    def finish(ht, op_ref, chunk_idx):
        for h in range(ML_HEADS):
            rows = slice(h * ML_V_DIM, (h + 1) * ML_V_DIM)
            hh = ht[rows, :]
            y = hh * lax.rsqrt(jnp.mean(hh * hh, axis=0, keepdims=True) + EPS) * gain_ref[rows, :]
            out_ref[chunk_idx, rows, :] = (_sigmoid(op_ref[rows, :].astype(F32)) * y).astype(BF16)

    @pl.when(c >= nc - half)
    def _():
        partner = nc - 1 - c
        finish(hf_ref[slot] + hb_ref[partner], opf_ref, c)
        finish(hb_ref[slot] + hf_ref[partner], opb_ref, partner)


def _mlstm(q_t, k_x, tr_x, gates_t, bias_col, k_m, tr_m, gates_m, gates_m_t, bias_row, gain_b,
           bsz, seq):
    nc = seq // ML_CHUNK
    half = nc // 2
    fwd = lambda b, c: b * nc + c
    bwd = lambda b, c: b * nc + (nc - 1 - c)

    def chunk_specs(blk):
        return [
            pl.BlockSpec((ML_QK_WIDTH, ML_CHUNK), lambda b, c: (0, blk(b, c))),
            pl.BlockSpec((ML_CHUNK, ML_QK_WIDTH), lambda b, c: (blk(b, c), 0)),
            pl.BlockSpec((ML_WIDTH, ML_CHUNK), lambda b, c: (TR_MV, blk(b, c))),
            pl.BlockSpec((N_GATE_COLS, ML_CHUNK), lambda b, c: (0, blk(b, c))),
            pl.BlockSpec((ML_WIDTH, ML_CHUNK), lambda b, c: (TR_MO, blk(b, c))),
        ]

    chunk_args = [q_t, k_x, tr_x, gates_t, tr_x]
    in_specs = chunk_specs(fwd) + chunk_specs(bwd) + [
        pl.BlockSpec((N_GATE_COLS, 1), lambda b, c: (0, 0)),
        pl.BlockSpec((1, N_META, ML_QK_WIDTH), lambda b, c: (b, 0, 0)),
        pl.BlockSpec((ML_WIDTH, N_META), lambda b, c: (TR_MV, 0)),
        pl.BlockSpec((N_META, LANES), lambda b, c: (0, 0)),
        pl.BlockSpec((N_GATE_COLS, N_META), lambda b, c: (0, 0)),
        pl.BlockSpec((1, LANES), lambda b, c: (0, 0)),
        pl.BlockSpec((ML_WIDTH, ML_CHUNK), lambda b, c: (0, 0)),
    ]
    return pl.pallas_call(
        _mlstm_kernel,
        grid=(bsz, nc),
        in_specs=in_specs,
        out_specs=pl.BlockSpec((nc, ML_WIDTH, ML_CHUNK), lambda b, c: (b, 0, 0)),
        out_shape=jax.ShapeDtypeStruct((bsz * nc, ML_WIDTH, ML_CHUNK), BF16),
        scratch_shapes=[
            pltpu.VMEM((2, ML_HEADS, ML_V_DIM, ML_QK_DIM), F32),
            pltpu.VMEM((2, ML_HEADS, 1, ML_QK_DIM), F32),
            pltpu.VMEM((2, ML_HEADS, SUBLANES, LANES), F32),
            pltpu.VMEM((half + 1, ML_WIDTH, ML_CHUNK), F32),
            pltpu.VMEM((half + 1, ML_WIDTH, ML_CHUNK), F32),
        ],
        compiler_params=_cparams(("parallel", "arbitrary")),
        name="mlstm",
    )(*chunk_args, *chunk_args, bias_col, k_m, tr_m, gates_m, gates_m_t, bias_row, gain_b)


_FFN_TILES = D_FF // FFN_TF


def _mix_ffn_kernel(a_ref, mt_ref, x_ref, wo_ref, g2_ref, gf_ref, wg_hbm, wu_hbm, wd_hbm,
                    o_ref, u_ref, wg_buf, wu_buf, wd_buf, sem):
    def weight_copies(j, slot):
        cols = pl.ds(pl.multiple_of(j * FFN_TF, FFN_TF), FFN_TF)
        return (pltpu.make_async_copy(wg_hbm.at[:, cols], wg_buf.at[slot], sem.at[0, slot]),
                pltpu.make_async_copy(wu_hbm.at[:, cols], wu_buf.at[slot], sem.at[1, slot]),
                pltpu.make_async_copy(wd_hbm.at[cols, :], wd_buf.at[slot], sem.at[2, slot]))

    for cp in weight_copies(0, 0):
        cp.start()

    w_m = wo_ref[DA_WIDTH:DA_WIDTH + ML_WIDTH, :]
    ml = jnp.concatenate([_dot_tn(mt_ref[s], w_m) for s in range(mt_ref.shape[0])], axis=0)
    h = _dot(a_ref[...], wo_ref[0:DA_WIDTH, :]) + ml + x_ref[...]
    o_ref[...] = h
    u = h * lax.rsqrt(jnp.mean(h * h, axis=-1, keepdims=True) + EPS) * g2_ref[...]
    u_ref[...] = u.astype(BF16)

    def step(j, slot, prefetch):
        if prefetch:
            for cp in weight_copies(j + 1, 1 - slot):
                cp.start()
        for cp in weight_copies(j, slot):
            cp.wait()
        u = u_ref[...]
        g = _dot(u, wg_buf[slot])
        ff = (g * _sigmoid(g) * _dot(u, wu_buf[slot])).astype(BF16)
        o_ref[...] += _dot(ff, wd_buf[slot])

    def pair(p, carry):
        step(2 * p, 0, True)
        step(2 * p + 1, 1, True)
        return carry

    lax.fori_loop(0, (_FFN_TILES - 1) // 2, pair, 0)
    step(_FFN_TILES - 1, 0, False)

    y = o_ref[...]
    o_ref[...] = y * lax.rsqrt(jnp.mean(y * y, axis=-1, keepdims=True) + EPS) * gf_ref[...]


def _mix_ffn(attn, ml_t, x_rows, w_out, norm_ffn_g, w_gate, w_up, w_down, norm_final_g):
    assert _FFN_TILES % 2 == 1
    m = x_rows.shape[0]
    return pl.pallas_call(
        _mix_ffn_kernel,
        grid=(m // FFN_TM,),
        in_specs=[
            pl.BlockSpec((FFN_TM, DA_WIDTH), lambda i: (i, 0)),
            pl.BlockSpec((FFN_TM // ML_CHUNK, ML_WIDTH, ML_CHUNK), lambda i: (i, 0, 0)),
            pl.BlockSpec((FFN_TM, D_MODEL), lambda i: (i, 0)),
            pl.BlockSpec((DA_WIDTH + ML_WIDTH, D_MODEL), lambda i: (0, 0)),
            pl.BlockSpec((1, D_MODEL), lambda i: (0, 0)),
            pl.BlockSpec((1, D_MODEL), lambda i: (0, 0)),
            pl.BlockSpec(memory_space=pl.ANY),
            pl.BlockSpec(memory_space=pl.ANY),
            pl.BlockSpec(memory_space=pl.ANY),
        ],
        out_specs=pl.BlockSpec((FFN_TM, D_MODEL), lambda i: (i, 0)),
        out_shape=jax.ShapeDtypeStruct((m, D_MODEL), F32),
        scratch_shapes=[
            pltpu.VMEM((FFN_TM, D_MODEL), BF16),
            pltpu.VMEM((2, D_MODEL, FFN_TF), BF16),
            pltpu.VMEM((2, D_MODEL, FFN_TF), BF16),
            pltpu.VMEM((2, FFN_TF, D_MODEL), BF16),
            pltpu.SemaphoreType.DMA((3, 2)),
        ],
        compiler_params=_cparams(("parallel",)),
        name="mix_ffn",
    )(attn, ml_t, x_rows, w_out, norm_ffn_g, norm_final_g, w_gate, w_up, w_down)


def _rope_tables(pos0, n):
    lane = jnp.arange(LANES) % DA_QK_DIM
    half = DA_ROT_DIM // 2
    inv = ROPE_THETA ** (-jnp.arange(0, DA_ROT_DIM, 2, dtype=F32) / DA_ROT_DIM)
    inv_lane = jnp.where(lane < DA_ROT_DIM, inv[lane % half], 0.0)
    pos = jnp.arange(pos0, pos0 + n, dtype=F32)
    ang = pos[:, None] * inv_lane[None, :]
    sin = jnp.sin(ang)
    sa = jnp.where(lane < half, -sin, 0.0)
    sb = jnp.where((lane >= half) & (lane < DA_ROT_DIM), sin, 0.0)
    return jnp.cos(ang), sa, sb


def _pad_lanes(a, width=LANES):
    return jnp.pad(a, ((0, 0), (0, width - a.shape[1])))


def kernel(x, meta_tokens, norm_mix, w_in, da_lambda_q1, da_lambda_k1, da_lambda_q2, da_lambda_k2,
           da_head_norm, ml_conv_w, ml_conv_b, ml_gate_bias, ml_head_norm, w_out, norm_ffn,
           w_gate, w_up, w_down, norm_final):
    bsz, seq, _ = x.shape
    x_rows = x.reshape(bsz * seq, D_MODEL)

    main_cols = N_NAT * IN_TN + N_TR * IN_TN
    w_main = w_in.astype(BF16)
    w_gates = _pad_lanes(w_in[0][:, main_cols:]).astype(BF16)
    norm_mix0 = norm_mix[0][None, :]

    tabs_x = _rope_tables(N_META, seq)
    tabs_m = _rope_tables(0, N_META)
    nat_x, tr_x, _, gates_t = _inproj(x_rows, norm_mix0, w_main, w_gates, *tabs_x, tm=IN_TM)
    nat_m, tr_m, gates_m, gates_m_t = _inproj(meta_tokens, norm_mix0, w_main, w_gates, *tabs_m,
                                              tm=N_META)

    conv_w = jnp.pad(ml_conv_w[0], ((0, SUBLANES - CONV_W), (0, 0)))
    q_t, k_x, k_m = _conv(nat_x, nat_m, conv_w, ml_conv_b[0][None, :], bsz, seq)

    lam_params = jnp.pad(
        jnp.stack([da_lambda_q1[0], da_lambda_k1[0], da_lambda_q2[0], da_lambda_k2[0]]),
        ((0, SUBLANES - 4), (0, LANES - DA_QK_DIM)))
    attn = _attention(lam_params, nat_x, nat_m, tr_x, tr_m,
                      da_head_norm[0].reshape(1, DA_WIDTH), bsz, seq)

    bias = ml_gate_bias[0].reshape(N_GATE_COLS)
    bias_row = _pad_lanes(bias[None, :])
    bias_col = bias[:, None]
    gain_b = jnp.broadcast_to(ml_head_norm[0].reshape(ML_WIDTH, 1), (ML_WIDTH, ML_CHUNK))
    ml_t = _mlstm(q_t, k_x, tr_x, gates_t, bias_col, k_m, tr_m, gates_m, gates_m_t, bias_row,
                  gain_b, bsz, seq)

    out = _mix_ffn(attn, ml_t, x_rows, w_out[0].astype(BF16), norm_ffn[0][None, :],
                   w_gate[0].astype(BF16), w_up[0].astype(BF16), w_down[0].astype(BF16),
                   norm_final[None, :])
    return out.reshape(bsz, seq, D_MODEL)
```

```python
import jax
import jax.numpy as jnp
from jax import lax
from jax.experimental import pallas as pl
from jax.experimental.pallas import tpu as pltpu

F32 = jnp.float32
BF16 = jnp.bfloat16

D_MODEL = 2048
N_META = 16
EPS = 1e-6
ROPE_THETA = 500000.0
NEG = -1e30

DA_QK_DIM = 64
DA_V_DIM = 128
DA_HEADS = 8
DA_ROT_DIM = 16
DA_WIDTH = DA_HEADS * DA_V_DIM
LAM_INIT = 0.8 - 0.6 * 1.0
LOG2_E = 1.4426950408889634

ML_HEADS = 4
ML_V_DIM = 256
ML_QK_DIM = 128
ML_WIDTH = ML_HEADS * ML_V_DIM
ML_QK_WIDTH = ML_HEADS * ML_QK_DIM
CONV_W = 5
N_GATES = 4
N_GATE_COLS = N_GATES * ML_HEADS

D_FF = 5632

IN_TN = 1024
NAT_AQ, NAT_AK, NAT_MQK = 0, 1, 2
TR_AV, TR_MV, TR_MO = 0, 1, 2
N_NAT = 3
N_TR = 3

LANES = 128
SUBLANES = 8
VMEM_LIMIT = 56 * 1024 * 1024

IN_TM = 512
ATT_TQ = 256
ATT_KB = 512
ML_CHUNK = 256
OUT_TM = 256
FFN_TM = 512
FFN_TF = 512
CONV_ROWS = 512


def _cparams(sem):
    return pltpu.CompilerParams(dimension_semantics=sem, vmem_limit_bytes=VMEM_LIMIT)


def _dot(a, b):
    return jnp.dot(a, b, preferred_element_type=F32)


def _dot_nt(a, b):
    return lax.dot_general(a, b, (((1,), (1,)), ((), ())), preferred_element_type=F32)


def _dot_tn(a, b):
    return lax.dot_general(a, b, (((0,), (0,)), ((), ())), preferred_element_type=F32)


def _bf16_pieces(x):
    hi = x.astype(BF16)
    rest = x - hi.astype(F32)
    mid = rest.astype(BF16)
    return hi, mid, (rest - mid.astype(F32)).astype(BF16)


def _cumsum_rows(x, visible):
    r = x.shape[0]
    ones = jnp.where(visible, 1.0, 0.0).astype(BF16)
    parts = _dot(jnp.concatenate(_bf16_pieces(x), axis=0), ones)
    return parts[0:r] + parts[r:2 * r] + parts[2 * r:3 * r]


def _cumsum_cols(x, visible):
    c = x.shape[1]
    ones = jnp.where(visible, 1.0, 0.0).astype(BF16)
    parts = _dot(ones, jnp.concatenate(_bf16_pieces(x), axis=1))
    return parts[:, 0:c] + parts[:, c:2 * c] + parts[:, 2 * c:3 * c]


def _sigmoid(x):
    return 1.0 / (1.0 + jnp.exp(-x))


def _log_sigmoid(x):
    return jnp.minimum(x, 0.0) - jnp.log(1.0 + jnp.exp(-jnp.abs(x)))


_NAT_COLS = (0, 1024, 3072)
_TR_COLS = (2048, 4096, 5120)


def _inproj_kernel(x_ref, g_ref, w_ref, wg_ref, cos_ref, sa_ref, sb_ref,
                   nat_ref, tr_ref, gate_ref, gate_t_ref):
    x = x_ref[...]
    ms = jnp.mean(x * x, axis=-1, keepdims=True)
    u = (x * lax.rsqrt(ms + EPS) * g_ref[...]).astype(BF16)
    gates = _dot(u, wg_ref[...])
    gate_ref[...] = gates
    gate_t_ref[...] = gates.T[0:N_GATE_COLS, :]

    def tile(col0):
        return _dot(u, w_ref[0, :, col0:col0 + IN_TN])

    for jj in (NAT_AQ, NAT_AK):
        acc = tile(_NAT_COLS[jj])
        scale = DA_QK_DIM ** -0.5 * LOG2_E if jj == NAT_AQ else 1.0
        cos = cos_ref[...] * scale
        sa = sa_ref[...] * scale
        sb = sb_ref[...] * scale
        for h in range(DA_HEADS):
            t = acc[:, h * LANES:(h + 1) * LANES]
            r = (t * cos + pltpu.roll(t, LANES - DA_ROT_DIM // 2, 1) * sa
                 + pltpu.roll(t, DA_ROT_DIM // 2, 1) * sb)
            nat_ref[:, jj * IN_TN + h * LANES:jj * IN_TN + (h + 1) * LANES] = r.astype(BF16)

    nat_ref[:, NAT_MQK * IN_TN:(NAT_MQK + 1) * IN_TN] = tile(_NAT_COLS[NAT_MQK]).astype(BF16)

    for jj in range(N_TR):
        tr_ref[jj * IN_TN:(jj + 1) * IN_TN, :] = tile(_TR_COLS[jj]).T.astype(BF16)


def _inproj(rows, norm_g, w_main, w_gate, cos_t, sa_t, sb_t, tm):
    m = rows.shape[0]
    tiles_per_seq = cos_t.shape[0] // tm
    tab_spec = pl.BlockSpec((tm, LANES), lambda i: (i % tiles_per_seq, 0))
    return pl.pallas_call(
        _inproj_kernel,
        grid=(m // tm,),
        in_specs=[
            pl.BlockSpec((tm, D_MODEL), lambda i: (i, 0)),
            pl.BlockSpec((1, D_MODEL), lambda i: (0, 0)),
            pl.BlockSpec(w_main.shape, lambda i: (0, 0, 0)),
            pl.BlockSpec((D_MODEL, LANES), lambda i: (0, 0)),
            tab_spec, tab_spec, tab_spec,
        ],
        out_specs=[
            pl.BlockSpec((tm, N_NAT * IN_TN), lambda i: (i, 0)),
            pl.BlockSpec((N_TR * IN_TN, tm), lambda i: (0, i)),
            pl.BlockSpec((tm, LANES), lambda i: (i, 0)),
            pl.BlockSpec((N_GATE_COLS, tm), lambda i: (0, i)),
        ],
        out_shape=[
            jax.ShapeDtypeStruct((m, N_NAT * IN_TN), BF16),
            jax.ShapeDtypeStruct((N_TR * IN_TN, m), BF16),
            jax.ShapeDtypeStruct((m, LANES), F32),
            jax.ShapeDtypeStruct((N_GATE_COLS, m), F32),
        ],
        compiler_params=_cparams(("parallel",)),
        name="inproj",
    )(rows, norm_g, w_main, w_gate, cos_t, sa_t, sb_t)


_CONV_PAD = SUBLANES


def _conv_kernel(x_ref, m_ref, w_ref, b_ref, qt_ref, k_ref, km_ref, s_ref):
    seq = x_ref.shape[0]
    j = pl.program_id(1)
    zeros = jnp.zeros((_CONV_PAD, LANES), F32)
    s_ref[0:_CONV_PAD, :] = zeros
    s_ref[_CONV_PAD:_CONV_PAD + N_META, :] = m_ref[...].astype(F32)
    s_ref[_CONV_PAD + N_META:_CONV_PAD + N_META + seq, :] = x_ref[...].astype(F32)
    s_ref[_CONV_PAD + N_META + seq:, :] = zeros
    w = w_ref[...]
    bias = b_ref[...]

    def conv(pos, n):
        acc = jnp.zeros((n, LANES), F32) + bias
        for t in range(CONV_W):
            start = _CONV_PAD + pos + t - CONV_W // 2
            acc = acc + s_ref[start:start + n, :] * w[t:t + 1, :]
        return acc * _sigmoid(acc)

    @pl.when(j < ML_HEADS)
    def _():
        for c in range(seq // CONV_ROWS):
            y = conv(N_META + c * CONV_ROWS, CONV_ROWS) * (ML_QK_DIM ** -0.5)
            qt_ref[:, c * CONV_ROWS:(c + 1) * CONV_ROWS] = y.T.astype(BF16)

    @pl.when(j >= ML_HEADS)
    def _():
        km_ref[0] = conv(0, N_META).astype(BF16)
        for c in range(seq // CONV_ROWS):
            k_ref[c * CONV_ROWS:(c + 1) * CONV_ROWS, :] = conv(
                N_META + c * CONV_ROWS, CONV_ROWS).astype(BF16)


def _conv(nat_x, nat_m, conv_w, conv_b, bsz, seq):
    col0 = NAT_MQK * (IN_TN // LANES)
    q_j = lambda j: jnp.minimum(j, ML_HEADS - 1)
    k_j = lambda j: jnp.maximum(j - ML_HEADS, 0)
    return pl.pallas_call(
        _conv_kernel,
        grid=(bsz, 2 * ML_HEADS),
        in_specs=[
            pl.BlockSpec((seq, LANES), lambda b, j: (b, col0 + j)),
            pl.BlockSpec((N_META, LANES), lambda b, j: (0, col0 + j)),
            pl.BlockSpec((SUBLANES, LANES), lambda b, j: (0, j)),
            pl.BlockSpec((1, LANES), lambda b, j: (0, j)),
        ],
        out_specs=[
            pl.BlockSpec((ML_QK_DIM, seq), lambda b, j: (q_j(j), b)),
            pl.BlockSpec((seq, ML_QK_DIM), lambda b, j: (b, k_j(j))),
            pl.BlockSpec((1, N_META, ML_QK_DIM), lambda b, j: (b, 0, k_j(j))),
        ],
        out_shape=[
            jax.ShapeDtypeStruct((ML_QK_WIDTH, bsz * seq), BF16),
            jax.ShapeDtypeStruct((bsz * seq, ML_QK_WIDTH), BF16),
            jax.ShapeDtypeStruct((bsz, N_META, ML_QK_WIDTH), BF16),
        ],
        scratch_shapes=[pltpu.VMEM((seq + N_META + 2 * _CONV_PAD, LANES), F32)],
        compiler_params=_cparams(("parallel", "arbitrary")),
        name="mlstm_conv",
    )(nat_x, nat_m, conv_w, conv_b)


_ATT_ACC_ROWS = DA_V_DIM + 16


def _attn_kernel(lam_ref, q_ref, k_ref, vt_ref, km_ref, vmt_ref, gain_ref, o_ref,
                 s_ref, sm_ref, acc_ref, o_t_ref, vte_ref, vmte_ref):
    tq = ATT_TQ
    nq = q_ref.shape[0] // tq
    nkb = k_ref.shape[0] // ATT_KB
    lp = lam_ref[...]
    lam = (jnp.exp(jnp.sum(lp[0:1] * lp[1:2], axis=1, keepdims=True))
           - jnp.exp(jnp.sum(lp[2:3] * lp[3:4], axis=1, keepdims=True)) + LAM_INIT)
    lane = lax.broadcasted_iota(jnp.int32, (tq, LANES), 1)
    comps = range(2)

    def q_rows(qt):
        return pl.ds(pl.multiple_of(qt * tq, tq), tq)

    def masked_q(qt):
        q = q_ref[q_rows(qt), :]
        zero = jnp.zeros_like(q)
        return (jnp.where(lane < DA_QK_DIM, q, zero), jnp.where(lane >= DA_QK_DIM, q, zero))

    def fold(a):
        return a.reshape(a.shape[0] // SUBLANES, SUBLANES, tq)

    def scores(qz, kb, slot):
        k = k_ref[kb * ATT_KB:(kb + 1) * ATT_KB, :]
        bmax = []
        for c in comps:
            s = _dot_nt(k, qz[c])
            s_ref[slot, c] = s
            bm = jnp.max(fold(s), axis=0)
            if kb == 0:
                sm = _dot_nt(km_ref[...], qz[c])
                sm_ref[c] = sm
                bm = jnp.maximum(bm, jnp.max(fold(sm), axis=0))
            bmax.append(bm)
        return tuple(bmax)

    def probs(s, m):
        return jnp.exp2(s - m).astype(BF16)

    def consume(kb, slot, bmax, m):
        vte = vte_ref[:, kb * ATT_KB:(kb + 1) * ATT_KB]
        m_out = []
        for c in comps:
            bm = jnp.max(bmax[c], axis=0, keepdims=True)
            if kb == 0:
                m_new = bm
                acc_ref[c] = (_dot(vte, probs(s_ref[slot, c], m_new))
                              + _dot(vmte_ref[...], probs(sm_ref[c], m_new)))
            else:
                m_new = jnp.maximum(m[c], bm)
                alpha = jnp.exp2(m[c] - m_new)
                acc_ref[c] = alpha * acc_ref[c] + _dot(vte, probs(s_ref[slot, c], m_new))
            m_out.append(m_new)
        return tuple(m_out)

    def finalize(qt):
        o = o_t_ref[...].T
        y = (o * lax.rsqrt(jnp.mean(o * o, axis=1, keepdims=True) + EPS)
             * gain_ref[...] * (1.0 - LAM_INIT))
        o_ref[q_rows(qt), :] = y.astype(BF16)

    def tile(qt, bmax):
        finalize(jnp.maximum(qt - 1, 0))
        qz = masked_q(qt)
        m = None
        for kb in range(nkb):
            if kb + 1 < nkb:
                bmax_next = scores(qz, kb + 1, (kb + 1) % 2)
            else:
                bmax_next = scores(masked_q(jnp.minimum(qt + 1, nq - 1)), 0, 0)
            m = consume(kb, kb % 2, bmax, m)
            bmax = bmax_next
        l1 = acc_ref[0, DA_V_DIM:DA_V_DIM + 1, :]
        l2 = acc_ref[1, DA_V_DIM:DA_V_DIM + 1, :]
        o_t_ref[...] = (acc_ref[0, 0:DA_V_DIM, :] / l1
                        - lam * (acc_ref[1, 0:DA_V_DIM, :] / l2))
        return bmax

    ones_row = jnp.where(
        lax.broadcasted_iota(jnp.int32, (_ATT_ACC_ROWS - DA_V_DIM, 1), 0) == 0, 1.0, 0.0)
    vte_ref[0:DA_V_DIM, :] = vt_ref[...]
    vte_ref[DA_V_DIM:, :] = jnp.broadcast_to(ones_row, (_ATT_ACC_ROWS - DA_V_DIM, vte_ref.shape[1])
                                             ).astype(BF16)
    vmte_ref[0:DA_V_DIM, :] = vmt_ref[...]
    vmte_ref[DA_V_DIM:, :] = jnp.broadcast_to(ones_row, (_ATT_ACC_ROWS - DA_V_DIM, N_META)
                                              ).astype(BF16)
    o_t_ref[...] = jnp.zeros_like(o_t_ref)
    lax.fori_loop(0, nq, tile, scores(masked_q(0), 0, 0), unroll=4)
    finalize(nq - 1)


def _attention(lam_params, nat_x, nat_m, tr_x, tr_m, gain, bsz, seq):
    hb = IN_TN // LANES
    return pl.pallas_call(
        _attn_kernel,
        grid=(bsz, DA_HEADS),
        in_specs=[
            pl.BlockSpec((SUBLANES, LANES), lambda b, h: (0, 0)),
            pl.BlockSpec((seq, LANES), lambda b, h: (b, NAT_AQ * hb + h)),
            pl.BlockSpec((seq, LANES), lambda b, h: (b, NAT_AK * hb + h)),
            pl.BlockSpec((DA_V_DIM, seq), lambda b, h: (TR_AV * DA_HEADS + h, b)),
            pl.BlockSpec((N_META, LANES), lambda b, h: (0, NAT_AK * hb + h)),
            pl.BlockSpec((DA_V_DIM, N_META), lambda b, h: (TR_AV * DA_HEADS + h, 0)),
            pl.BlockSpec((1, LANES), lambda b, h: (0, h)),
        ],
        out_specs=pl.BlockSpec((seq, LANES), lambda b, h: (b, h)),
        out_shape=jax.ShapeDtypeStruct((bsz * seq, DA_WIDTH), BF16),
        scratch_shapes=[
            pltpu.VMEM((2, 2, ATT_KB, ATT_TQ), F32),
            pltpu.VMEM((2, N_META, ATT_TQ), F32),
            pltpu.VMEM((2, _ATT_ACC_ROWS, ATT_TQ), F32),
            pltpu.VMEM((DA_V_DIM, ATT_TQ), F32),
            pltpu.VMEM((_ATT_ACC_ROWS, seq), BF16),
            pltpu.VMEM((_ATT_ACC_ROWS, N_META), BF16),
        ],
        compiler_params=_cparams(("parallel", "parallel")),
        name="diff_attention",
    )(lam_params, nat_x, nat_x, tr_x, nat_m, tr_m, gain)


def _ml_state_update(k, vt, a_col, b_end, g_row, ct_ref, n_ref, m_ref):
    m_prev = m_ref[0:1, 0:1]
    m_new = jnp.maximum(b_end + m_prev, jnp.max(g_row, axis=1, keepdims=True))
    decay = jnp.exp(b_end + m_prev - m_new)
    kw = (k.astype(F32) * jnp.exp(b_end + a_col - m_new)).astype(BF16)
    ct_ref[...] = decay * ct_ref[...] + _dot(vt, kw)
    n_ref[...] = decay * n_ref[...] + jnp.sum(kw.astype(F32), axis=0, keepdims=True)
    m_ref[...] = jnp.broadcast_to(m_new, (SUBLANES, LANES))


def _ml_chunk(reverse, qt_ref, k_ref, vt_ref, gt_ref, bc_ref, ct_ref, n_ref, m_ref, dst_ref):
    chunk = k_ref.shape[0]
    gate0 = 2 * ML_HEADS if reverse else 0
    grow = gt_ref[...] + bc_ref[...]
    row = lax.broadcasted_iota(jnp.int32, (chunk, chunk), 0)
    col = lax.broadcasted_iota(jnp.int32, (chunk, chunk), 1)
    mask = (row >= col) if reverse else (row <= col)
    brow_all = _cumsum_rows(_log_sigmoid(grow), mask)
    a_rows = grow - pltpu.roll(brow_all, N_GATE_COLS - ML_HEADS, 0)
    a_cols = jnp.concatenate(
        [a_rows, jnp.zeros((LANES - N_GATE_COLS, chunk), F32)], axis=0).T
    end = 0 if reverse else chunk - 1

    for h in range(ML_HEADS):
        gi, gf = gate0 + h, gate0 + ML_HEADS + h
        k = k_ref[:, h * ML_QK_DIM:(h + 1) * ML_QK_DIM]
        qt = qt_ref[h * ML_QK_DIM:(h + 1) * ML_QK_DIM, :]
        vt = vt_ref[h * ML_V_DIM:(h + 1) * ML_V_DIM, :]
        a_col = a_cols[:, gi:gi + 1]
        b_row = brow_all[gf:gf + 1, :]
        li_row = grow[gi:gi + 1, :]
        b_end = b_row[:, end:end + 1]
        m_prev = m_ref[h][0:1, 0:1]

        d = jnp.where(mask, b_row + a_col, NEG)
        m_inter = b_row + m_prev
        m_t = jnp.maximum(m_inter, jnp.max(d, axis=0, keepdims=True))
        w_inter = jnp.exp(m_inter - m_t)
        sg = _dot(k, qt) * jnp.exp(d - m_t)
        num = w_inter * _dot(ct_ref[h].astype(BF16), qt) + _dot(vt, sg.astype(BF16))
        n16 = jnp.broadcast_to(n_ref[h], (2 * SUBLANES, ML_QK_DIM)).astype(BF16)
        den = w_inter * _dot(n16, qt)[0:1, :] + jnp.sum(sg, axis=0, keepdims=True)
        dst_ref[h * ML_V_DIM:(h + 1) * ML_V_DIM, :] = (
            num / jnp.maximum(jnp.abs(den), jnp.exp(-m_t)))

        _ml_state_update(k, vt, a_col, b_end, b_end - b_row + li_row,
                         ct_ref.at[h], n_ref.at[h], m_ref.at[h])


def _mlstm_kernel(qtf_ref, kf_ref, vtf_ref, gtf_ref, opf_ref,
                  qtb_ref, kb_ref, vtb_ref, gtb_ref, opb_ref,
                  bc_ref, km_ref, vmt_ref, gm_ref, gmt_ref, br_ref, gain_ref,
                  out_ref, ct_ref, n_ref, m_ref, hf_ref, hb_ref):
    c = pl.program_id(1)
    nc = pl.num_programs(1)
    half = hf_ref.shape[0] - 1

    @pl.when(c == 0)
    def _():
        ct_ref[...] = jnp.zeros_like(ct_ref)
        n_ref[...] = jnp.zeros_like(n_ref)
        m_ref[...] = jnp.zeros_like(m_ref)
        gm = gm_ref[...] + br_ref[...]
        gmt = gmt_ref[...] + bc_ref[...]
        r = lax.broadcasted_iota(jnp.int32, (N_META, N_META), 0)
        s = lax.broadcasted_iota(jnp.int32, (N_META, N_META), 1)
        bcm = _cumsum_cols(_log_sigmoid(gm), s <= r)
        brm = _cumsum_rows(_log_sigmoid(gmt), r <= s)
        for h in range(ML_HEADS):
            gi, gf = h, ML_HEADS + h
            b_end = brm[gf:gf + 1, N_META - 1:N_META]
            _ml_state_update(
                km_ref[0][:, h * ML_QK_DIM:(h + 1) * ML_QK_DIM],
                vmt_ref[h * ML_V_DIM:(h + 1) * ML_V_DIM, :],
                gm[:, gi:gi + 1] - bcm[:, gf:gf + 1], b_end,
                b_end - brm[gf:gf + 1, :] + gmt[gi:gi + 1, :],
                ct_ref.at[0, h], n_ref.at[0, h], m_ref.at[0, h])

    slot = jnp.minimum(c, half)
    _ml_chunk(False, qtf_ref, kf_ref, vtf_ref, gtf_ref, bc_ref,
              ct_ref.at[0], n_ref.at[0], m_ref.at[0], hf_ref.at[slot])
    _ml_chunk(True, qtb_ref, kb_ref, vtb_ref, gtb_ref, bc_ref,
              ct_ref.at[1], n_ref.at[1], m_ref.at[1], hb_ref.at[slot])

    def finish(ht, op_ref, chunk_idx):
        for h in range(ML_HEADS):
            rows = slice(h * ML_V_DIM, (h + 1) * ML_V_DIM)
            hh = ht[rows, :]
            y = hh * lax.rsqrt(jnp.mean(hh * hh, axis=0, keepdims=True) + EPS) * gain_ref[rows, :]
            out_ref[chunk_idx, rows, :] = (_sigmoid(op_ref[rows, :].astype(F32)) * y).astype(BF16)

    @pl.when(c >= nc - half)
    def _():
        partner = nc - 1 - c
        finish(hf_ref[slot] + hb_ref[partner], opf_ref, c)
        finish(hb_ref[slot] + hf_ref[partner], opb_ref, partner)


def _mlstm(q_t, k_x, tr_x, gates_t, bias_col, k_m, tr_m, gates_m, gates_m_t, bias_row, gain_b,
           bsz, seq):
    nc = seq // ML_CHUNK
    half = nc // 2
    fwd = lambda b, c: b * nc + c
    bwd = lambda b, c: b * nc + (nc - 1 - c)

    def chunk_specs(blk):
        return [
            pl.BlockSpec((ML_QK_WIDTH, ML_CHUNK), lambda b, c: (0, blk(b, c))),
            pl.BlockSpec((ML_CHUNK, ML_QK_WIDTH), lambda b, c: (blk(b, c), 0)),
            pl.BlockSpec((ML_WIDTH, ML_CHUNK), lambda b, c: (TR_MV, blk(b, c))),
            pl.BlockSpec((N_GATE_COLS, ML_CHUNK), lambda b, c: (0, blk(b, c))),
            pl.BlockSpec((ML_WIDTH, ML_CHUNK), lambda b, c: (TR_MO, blk(b, c))),
        ]

    chunk_args = [q_t, k_x, tr_x, gates_t, tr_x]
    in_specs = chunk_specs(fwd) + chunk_specs(bwd) + [
        pl.BlockSpec((N_GATE_COLS, 1), lambda b, c: (0, 0)),
        pl.BlockSpec((1, N_META, ML_QK_WIDTH), lambda b, c: (b, 0, 0)),
        pl.BlockSpec((ML_WIDTH, N_META), lambda b, c: (TR_MV, 0)),
        pl.BlockSpec((N_META, LANES), lambda b, c: (0, 0)),
        pl.BlockSpec((N_GATE_COLS, N_META), lambda b, c: (0, 0)),
        pl.BlockSpec((1, LANES), lambda b, c: (0, 0)),
        pl.BlockSpec((ML_WIDTH, ML_CHUNK), lambda b, c: (0, 0)),
    ]
    return pl.pallas_call(
        _mlstm_kernel,
        grid=(bsz, nc),
        in_specs=in_specs,
        out_specs=pl.BlockSpec((nc, ML_WIDTH, ML_CHUNK), lambda b, c: (b, 0, 0)),
        out_shape=jax.ShapeDtypeStruct((bsz * nc, ML_WIDTH, ML_CHUNK), BF16),
        scratch_shapes=[
            pltpu.VMEM((2, ML_HEADS, ML_V_DIM, ML_QK_DIM), F32),
            pltpu.VMEM((2, ML_HEADS, 1, ML_QK_DIM), F32),
            pltpu.VMEM((2, ML_HEADS, SUBLANES, LANES), F32),
            pltpu.VMEM((half + 1, ML_WIDTH, ML_CHUNK), F32),
            pltpu.VMEM((half + 1, ML_WIDTH, ML_CHUNK), F32),
        ],
        compiler_params=_cparams(("parallel", "arbitrary")),
        name="mlstm",
    )(*chunk_args, *chunk_args, bias_col, k_m, tr_m, gates_m, gates_m_t, bias_row, gain_b)


_FFN_TILES = D_FF // FFN_TF


def _mix_ffn_kernel(a_ref, mt_ref, x_ref, wo_ref, g2_ref, gf_ref, wg_hbm, wu_hbm, wd_hbm,
                    o_ref, u_ref, wg_buf, wu_buf, wd_buf, sem):
    def weight_copies(j, slot):
        cols = pl.ds(pl.multiple_of(j * FFN_TF, FFN_TF), FFN_TF)
        return (pltpu.make_async_copy(wg_hbm.at[:, cols], wg_buf.at[slot], sem.at[0, slot]),
                pltpu.make_async_copy(wu_hbm.at[:, cols], wu_buf.at[slot], sem.at[1, slot]),
                pltpu.make_async_copy(wd_hbm.at[cols, :], wd_buf.at[slot], sem.at[2, slot]))

    for cp in weight_copies(0, 0):
        cp.start()

    w_m = wo_ref[DA_WIDTH:DA_WIDTH + ML_WIDTH, :]
    ml = jnp.concatenate([_dot_tn(mt_ref[s], w_m) for s in range(mt_ref.shape[0])], axis=0)
    h = _dot(a_ref[...], wo_ref[0:DA_WIDTH, :]) + ml + x_ref[...]
    o_ref[...] = h
    u = h * lax.rsqrt(jnp.mean(h * h, axis=-1, keepdims=True) + EPS) * g2_ref[...]
    u_ref[...] = u.astype(BF16)

    def step(j, slot, prefetch):
        if prefetch:
            for cp in weight_copies(j + 1, 1 - slot):
                cp.start()
        for cp in weight_copies(j, slot):
            cp.wait()
        u = u_ref[...]
        g = _dot(u, wg_buf[slot])
        ff = (g * _sigmoid(g) * _dot(u, wu_buf[slot])).astype(BF16)
        o_ref[...] += _dot(ff, wd_buf[slot])

    def pair(p, carry):
        step(2 * p, 0, True)
        step(2 * p + 1, 1, True)
        return carry

    lax.fori_loop(0, (_FFN_TILES - 1) // 2, pair, 0)
    step(_FFN_TILES - 1, 0, False)

    y = o_ref[...]
    o_ref[...] = y * lax.rsqrt(jnp.mean(y * y, axis=-1, keepdims=True) + EPS) * gf_ref[...]


def _mix_ffn(attn, ml_t, x_rows, w_out, norm_ffn_g, w_gate, w_up, w_down, norm_final_g):
    assert _FFN_TILES % 2 == 1
    m = x_rows.shape[0]
    return pl.pallas_call(
        _mix_ffn_kernel,
        grid=(m // FFN_TM,),
        in_specs=[
            pl.BlockSpec((FFN_TM, DA_WIDTH), lambda i: (i, 0)),
            pl.BlockSpec((FFN_TM // ML_CHUNK, ML_WIDTH, ML_CHUNK), lambda i: (i, 0, 0)),
            pl.BlockSpec((FFN_TM, D_MODEL), lambda i: (i, 0)),
            pl.BlockSpec((DA_WIDTH + ML_WIDTH, D_MODEL), lambda i: (0, 0)),
            pl.BlockSpec((1, D_MODEL), lambda i: (0, 0)),
            pl.BlockSpec((1, D_MODEL), lambda i: (0, 0)),
            pl.BlockSpec(memory_space=pl.ANY),
            pl.BlockSpec(memory_space=pl.ANY),
            pl.BlockSpec(memory_space=pl.ANY),
        ],
        out_specs=pl.BlockSpec((FFN_TM, D_MODEL), lambda i: (i, 0)),
        out_shape=jax.ShapeDtypeStruct((m, D_MODEL), F32),
        scratch_shapes=[
            pltpu.VMEM((FFN_TM, D_MODEL), BF16),
            pltpu.VMEM((2, D_MODEL, FFN_TF), BF16),
            pltpu.VMEM((2, D_MODEL, FFN_TF), BF16),
            pltpu.VMEM((2, FFN_TF, D_MODEL), BF16),
            pltpu.SemaphoreType.DMA((3, 2)),
        ],
        compiler_params=_cparams(("parallel",)),
        name="mix_ffn",
    )(attn, ml_t, x_rows, w_out, norm_ffn_g, norm_final_g, w_gate, w_up, w_down)


def _rope_tables(pos0, n):
    lane = jnp.arange(LANES) % DA_QK_DIM
    half = DA_ROT_DIM // 2
    inv = ROPE_THETA ** (-jnp.arange(0, DA_ROT_DIM, 2, dtype=F32) / DA_ROT_DIM)
    inv_lane = jnp.where(lane < DA_ROT_DIM, inv[lane % half], 0.0)
    pos = jnp.arange(pos0, pos0 + n, dtype=F32)
    ang = pos[:, None] * inv_lane[None, :]
    sin = jnp.sin(ang)
    sa = jnp.where(lane < half, -sin, 0.0)
    sb = jnp.where((lane >= half) & (lane < DA_ROT_DIM), sin, 0.0)
    return jnp.cos(ang), sa, sb


def _pad_lanes(a, width=LANES):
    return jnp.pad(a, ((0, 0), (0, width - a.shape[1])))


def kernel(x, meta_tokens, norm_mix, w_in, da_lambda_q1, da_lambda_k1, da_lambda_q2, da_lambda_k2,
           da_head_norm, ml_conv_w, ml_conv_b, ml_gate_bias, ml_head_norm, w_out, norm_ffn,
           w_gate, w_up, w_down, norm_final):
    bsz, seq, _ = x.shape
    x_rows = x.reshape(bsz * seq, D_MODEL)

    main_cols = N_NAT * IN_TN + N_TR * IN_TN
    w_main = w_in.astype(BF16)
    w_gates = _pad_lanes(w_in[0][:, main_cols:]).astype(BF16)
    norm_mix0 = norm_mix[0][None, :]

    tabs_x = _rope_tables(N_META, seq)
    tabs_m = _rope_tables(0, N_META)
    nat_x, tr_x, _, gates_t = _inproj(x_rows, norm_mix0, w_main, w_gates, *tabs_x, tm=IN_TM)
    nat_m, tr_m, gates_m, gates_m_t = _inproj(meta_tokens, norm_mix0, w_main, w_gates, *tabs_m,
                                              tm=N_META)

    conv_w = jnp.pad(ml_conv_w[0], ((0, SUBLANES - CONV_W), (0, 0)))
    q_t, k_x, k_m = _conv(nat_x, nat_m, conv_w, ml_conv_b[0][None, :], bsz, seq)

    lam_params = jnp.pad(
        jnp.stack([da_lambda_q1[0], da_lambda_k1[0], da_lambda_q2[0], da_lambda_k2[0]]),
        ((0, SUBLANES - 4), (0, LANES - DA_QK_DIM)))
    attn = _attention(lam_params, nat_x, nat_m, tr_x, tr_m,
                      da_head_norm[0].reshape(1, DA_WIDTH), bsz, seq)

    bias = ml_gate_bias[0].reshape(N_GATE_COLS)
    bias_row = _pad_lanes(bias[None, :])
    bias_col = bias[:, None]
    gain_b = jnp.broadcast_to(ml_head_norm[0].reshape(ML_WIDTH, 1), (ML_WIDTH, ML_CHUNK))
    ml_t = _mlstm(q_t, k_x, tr_x, gates_t, bias_col, k_m, tr_m, gates_m, gates_m_t, bias_row,
                  gain_b, bsz, seq)

    out = _mix_ffn(attn, ml_t, x_rows, w_out[0].astype(BF16), norm_ffn[0][None, :],
                   w_gate[0].astype(BF16), w_up[0].astype(BF16), w_down[0].astype(BF16),
                   norm_final[None, :])
    return out.reshape(bsz, seq, D_MODEL)
```

```python
import jax
import jax.numpy as jnp
from jax import lax
from jax.experimental import pallas as pl
from jax.experimental.pallas import tpu as pltpu

F32 = jnp.float32
BF16 = jnp.bfloat16

D_MODEL = 2048
N_META = 16
EPS = 1e-6
ROPE_THETA = 500000.0
NEG = -1e30

DA_QK_DIM = 64
DA_V_DIM = 128
DA_HEADS = 8
DA_ROT_DIM = 16
DA_WIDTH = DA_HEADS * DA_V_DIM
LAM_INIT = 0.8 - 0.6 * 1.0
LOG2_E = 1.4426950408889634

ML_HEADS = 4
ML_V_DIM = 256
ML_QK_DIM = 128
ML_WIDTH = ML_HEADS * ML_V_DIM
ML_QK_WIDTH = ML_HEADS * ML_QK_DIM
CONV_W = 5
N_GATES = 4
N_GATE_COLS = N_GATES * ML_HEADS

D_FF = 5632

IN_TN = 1024
NAT_AQ, NAT_AK, NAT_MQK = 0, 1, 2
TR_AV, TR_MV, TR_MO = 0, 1, 2
N_NAT = 3
N_TR = 3

LANES = 128
SUBLANES = 8
VMEM_LIMIT = 56 * 1024 * 1024

IN_TM = 512
ATT_TQ = 256
ATT_KB = 512
ML_CHUNK = 256
OUT_TM = 256
FFN_TM = 512
FFN_TF = 512
CONV_ROWS = 512


def _cparams(sem):
    return pltpu.CompilerParams(dimension_semantics=sem, vmem_limit_bytes=VMEM_LIMIT)


def _dot(a, b):
    return jnp.dot(a, b, preferred_element_type=F32)


def _dot_nt(a, b):
    return lax.dot_general(a, b, (((1,), (1,)), ((), ())), preferred_element_type=F32)


def _dot_tn(a, b):
    return lax.dot_general(a, b, (((0,), (0,)), ((), ())), preferred_element_type=F32)


def _bf16_pieces(x):
    hi = x.astype(BF16)
    rest = x - hi.astype(F32)
    mid = rest.astype(BF16)
    return hi, mid, (rest - mid.astype(F32)).astype(BF16)


def _cumsum_rows(x, visible):
    r = x.shape[0]
    ones = jnp.where(visible, 1.0, 0.0).astype(BF16)
    parts = _dot(jnp.concatenate(_bf16_pieces(x), axis=0), ones)
    return parts[0:r] + parts[r:2 * r] + parts[2 * r:3 * r]


def _cumsum_cols(x, visible):
    c = x.shape[1]
    ones = jnp.where(visible, 1.0, 0.0).astype(BF16)
    parts = _dot(ones, jnp.concatenate(_bf16_pieces(x), axis=1))
    return parts[:, 0:c] + parts[:, c:2 * c] + parts[:, 2 * c:3 * c]


def _sigmoid(x):
    return 1.0 / (1.0 + jnp.exp(-x))


def _log_sigmoid(x):
    return jnp.minimum(x, 0.0) - jnp.log(1.0 + jnp.exp(-jnp.abs(x)))


_NAT_COLS = (0, 1024, 3072)
_TR_COLS = (2048, 4096, 5120)


def _inproj_kernel(x_ref, g_ref, w_ref, wg_ref, cos_ref, sa_ref, sb_ref,
                   nat_ref, tr_ref, gate_ref, gate_t_ref):
    x = x_ref[...]
    ms = jnp.mean(x * x, axis=-1, keepdims=True)
    u = (x * lax.rsqrt(ms + EPS) * g_ref[...]).astype(BF16)
    gates = _dot(u, wg_ref[...])
    gate_ref[...] = gates
    gate_t_ref[...] = gates.T[0:N_GATE_COLS, :]

    def tile(col0):
        return _dot(u, w_ref[0, :, col0:col0 + IN_TN])

    for jj in (NAT_AQ, NAT_AK):
        acc = tile(_NAT_COLS[jj])
        scale = DA_QK_DIM ** -0.5 * LOG2_E if jj == NAT_AQ else 1.0
        cos = cos_ref[...] * scale
        sa = sa_ref[...] * scale
        sb = sb_ref[...] * scale
        for h in range(DA_HEADS):
            t = acc[:, h * LANES:(h + 1) * LANES]
            r = (t * cos + pltpu.roll(t, LANES - DA_ROT_DIM // 2, 1) * sa
                 + pltpu.roll(t, DA_ROT_DIM // 2, 1) * sb)
            nat_ref[:, jj * IN_TN + h * LANES:jj * IN_TN + (h + 1) * LANES] = r.astype(BF16)

    nat_ref[:, NAT_MQK * IN_TN:(NAT_MQK + 1) * IN_TN] = tile(_NAT_COLS[NAT_MQK]).astype(BF16)

    for jj in range(N_TR):
        tr_ref[jj * IN_TN:(jj + 1) * IN_TN, :] = tile(_TR_COLS[jj]).T.astype(BF16)


def _inproj(rows, norm_g, w_main, w_gate, cos_t, sa_t, sb_t, tm):
    m = rows.shape[0]
    tiles_per_seq = cos_t.shape[0] // tm
    tab_spec = pl.BlockSpec((tm, LANES), lambda i: (i % tiles_per_seq, 0))
    return pl.pallas_call(
        _inproj_kernel,
        grid=(m // tm,),
        in_specs=[
            pl.BlockSpec((tm, D_MODEL), lambda i: (i, 0)),
            pl.BlockSpec((1, D_MODEL), lambda i: (0, 0)),
            pl.BlockSpec(w_main.shape, lambda i: (0, 0, 0)),
            pl.BlockSpec((D_MODEL, LANES), lambda i: (0, 0)),
            tab_spec, tab_spec, tab_spec,
        ],
        out_specs=[
            pl.BlockSpec((tm, N_NAT * IN_TN), lambda i: (i, 0)),
            pl.BlockSpec((N_TR * IN_TN, tm), lambda i: (0, i)),
            pl.BlockSpec((tm, LANES), lambda i: (i, 0)),
            pl.BlockSpec((N_GATE_COLS, tm), lambda i: (0, i)),
        ],
        out_shape=[
            jax.ShapeDtypeStruct((m, N_NAT * IN_TN), BF16),
            jax.ShapeDtypeStruct((N_TR * IN_TN, m), BF16),
            jax.ShapeDtypeStruct((m, LANES), F32),
            jax.ShapeDtypeStruct((N_GATE_COLS, m), F32),
        ],
        compiler_params=_cparams(("parallel",)),
        name="inproj",
    )(rows, norm_g, w_main, w_gate, cos_t, sa_t, sb_t)


_CONV_PAD = SUBLANES


def _conv_kernel(x_ref, m_ref, w_ref, b_ref, qt_ref, k_ref, km_ref, s_ref):
    seq = x_ref.shape[0]
    j = pl.program_id(1)
    zeros = jnp.zeros((_CONV_PAD, LANES), F32)
    s_ref[0:_CONV_PAD, :] = zeros
    s_ref[_CONV_PAD:_CONV_PAD + N_META, :] = m_ref[...].astype(F32)
    s_ref[_CONV_PAD + N_META:_CONV_PAD + N_META + seq, :] = x_ref[...].astype(F32)
    s_ref[_CONV_PAD + N_META + seq:, :] = zeros
    w = w_ref[...]
    bias = b_ref[...]

    def conv(pos, n):
        acc = jnp.zeros((n, LANES), F32) + bias
        for t in range(CONV_W):
            start = _CONV_PAD + pos + t - CONV_W // 2
            acc = acc + s_ref[start:start + n, :] * w[t:t + 1, :]
        return acc * _sigmoid(acc)

    @pl.when(j < ML_HEADS)
    def _():
        for c in range(seq // CONV_ROWS):
            y = conv(N_META + c * CONV_ROWS, CONV_ROWS) * (ML_QK_DIM ** -0.5)
            qt_ref[:, c * CONV_ROWS:(c + 1) * CONV_ROWS] = y.T.astype(BF16)

    @pl.when(j >= ML_HEADS)
    def _():
        km_ref[0] = conv(0, N_META).astype(BF16)
        for c in range(seq // CONV_ROWS):
            k_ref[c * CONV_ROWS:(c + 1) * CONV_ROWS, :] = conv(
                N_META + c * CONV_ROWS, CONV_ROWS).astype(BF16)


def _conv(nat_x, nat_m, conv_w, conv_b, bsz, seq):
    col0 = NAT_MQK * (IN_TN // LANES)
    q_j = lambda j: jnp.minimum(j, ML_HEADS - 1)
    k_j = lambda j: jnp.maximum(j - ML_HEADS, 0)
    return pl.pallas_call(
        _conv_kernel,
        grid=(bsz, 2 * ML_HEADS),
        in_specs=[
            pl.BlockSpec((seq, LANES), lambda b, j: (b, col0 + j)),
            pl.BlockSpec((N_META, LANES), lambda b, j: (0, col0 + j)),
            pl.BlockSpec((SUBLANES, LANES), lambda b, j: (0, j)),
            pl.BlockSpec((1, LANES), lambda b, j: (0, j)),
        ],
        out_specs=[
            pl.BlockSpec((ML_QK_DIM, seq), lambda b, j: (q_j(j), b)),
            pl.BlockSpec((seq, ML_QK_DIM), lambda b, j: (b, k_j(j))),
            pl.BlockSpec((1, N_META, ML_QK_DIM), lambda b, j: (b, 0, k_j(j))),
        ],
        out_shape=[
            jax.ShapeDtypeStruct((ML_QK_WIDTH, bsz * seq), BF16),
            jax.ShapeDtypeStruct((bsz * seq, ML_QK_WIDTH), BF16),
            jax.ShapeDtypeStruct((bsz, N_META, ML_QK_WIDTH), BF16),
        ],
        scratch_shapes=[pltpu.VMEM((seq + N_META + 2 * _CONV_PAD, LANES), F32)],
        compiler_params=_cparams(("parallel", "arbitrary")),
        name="mlstm_conv",
    )(nat_x, nat_m, conv_w, conv_b)


_ATT_ACC_ROWS = DA_V_DIM + 16


def _attn_kernel(lam_ref, q_ref, k_ref, vt_ref, km_ref, vmt_ref, gain_ref, o_ref,
                 s_ref, sm_ref, acc_ref, o_t_ref, vte_ref, vmte_ref):
    tq = ATT_TQ
    nq = q_ref.shape[0] // tq
    nkb = k_ref.shape[0] // ATT_KB
    lp = lam_ref[...]
    lam = (jnp.exp(jnp.sum(lp[0:1] * lp[1:2], axis=1, keepdims=True))
           - jnp.exp(jnp.sum(lp[2:3] * lp[3:4], axis=1, keepdims=True)) + LAM_INIT)
    lane = lax.broadcasted_iota(jnp.int32, (tq, LANES), 1)
    comps = range(2)

    def q_rows(qt):
        return pl.ds(pl.multiple_of(qt * tq, tq), tq)

    def masked_q(qt):
        q = q_ref[q_rows(qt), :]
        zero = jnp.zeros_like(q)
        return (jnp.where(lane < DA_QK_DIM, q, zero), jnp.where(lane >= DA_QK_DIM, q, zero))

    def fold(a):
        return a.reshape(a.shape[0] // SUBLANES, SUBLANES, tq)

    def scores(qz, kb, slot):
        k = k_ref[kb * ATT_KB:(kb + 1) * ATT_KB, :]
        bmax = []
        for c in comps:
            s = _dot_nt(k, qz[c])
            s_ref[slot, c] = s
            bm = jnp.max(fold(s), axis=0)
            if kb == 0:
                sm = _dot_nt(km_ref[...], qz[c])
                sm_ref[c] = sm
                bm = jnp.maximum(bm, jnp.max(fold(sm), axis=0))
            bmax.append(bm)
        return tuple(bmax)

    def probs(s, m):
        return jnp.exp2(s - m).astype(BF16)

    def consume(kb, slot, bmax, m):
        vte = vte_ref[:, kb * ATT_KB:(kb + 1) * ATT_KB]
        m_out = []
        for c in comps:
            bm = jnp.max(bmax[c], axis=0, keepdims=True)
            if kb == 0:
                m_new = bm
                acc_ref[c] = (_dot(vte, probs(s_ref[slot, c], m_new))
                              + _dot(vmte_ref[...], probs(sm_ref[c], m_new)))
            else:
                m_new = jnp.maximum(m[c], bm)
                alpha = jnp.exp2(m[c] - m_new)
                acc_ref[c] = alpha * acc_ref[c] + _dot(vte, probs(s_ref[slot, c], m_new))
            m_out.append(m_new)
        return tuple(m_out)

    def finalize(qt):
        o = o_t_ref[...].T
        y = (o * lax.rsqrt(jnp.mean(o * o, axis=1, keepdims=True) + EPS)
             * gain_ref[...] * (1.0 - LAM_INIT))
        o_ref[q_rows(qt), :] = y.astype(BF16)

    def tile(qt, bmax):
        finalize(jnp.maximum(qt - 1, 0))
        qz = masked_q(qt)
        m = None
        for kb in range(nkb):
            if kb + 1 < nkb:
                bmax_next = scores(qz, kb + 1, (kb + 1) % 2)
            else:
                bmax_next = scores(masked_q(jnp.minimum(qt + 1, nq - 1)), 0, 0)
            m = consume(kb, kb % 2, bmax, m)
            bmax = bmax_next
        l1 = acc_ref[0, DA_V_DIM:DA_V_DIM + 1, :]
        l2 = acc_ref[1, DA_V_DIM:DA_V_DIM + 1, :]
        o_t_ref[...] = (acc_ref[0, 0:DA_V_DIM, :] / l1
                        - lam * (acc_ref[1, 0:DA_V_DIM, :] / l2))
        return bmax

    ones_row = jnp.where(
        lax.broadcasted_iota(jnp.int32, (_ATT_ACC_ROWS - DA_V_DIM, 1), 0) == 0, 1.0, 0.0)
    vte_ref[0:DA_V_DIM, :] = vt_ref[...]
    vte_ref[DA_V_DIM:, :] = jnp.broadcast_to(ones_row, (_ATT_ACC_ROWS - DA_V_DIM, vte_ref.shape[1])
                                             ).astype(BF16)
    vmte_ref[0:DA_V_DIM, :] = vmt_ref[...]
    vmte_ref[DA_V_DIM:, :] = jnp.broadcast_to(ones_row, (_ATT_ACC_ROWS - DA_V_DIM, N_META)
                                              ).astype(BF16)
    o_t_ref[...] = jnp.zeros_like(o_t_ref)
    lax.fori_loop(0, nq, tile, scores(masked_q(0), 0, 0), unroll=4)
    finalize(nq - 1)


def _attention(lam_params, nat_x, nat_m, tr_x, tr_m, gain, bsz, seq):
    hb = IN_TN // LANES
    return pl.pallas_call(
        _attn_kernel,
        grid=(bsz, DA_HEADS),
        in_specs=[
            pl.BlockSpec((SUBLANES, LANES), lambda b, h: (0, 0)),
            pl.BlockSpec((seq, LANES), lambda b, h: (b, NAT_AQ * hb + h)),
            pl.BlockSpec((seq, LANES), lambda b, h: (b, NAT_AK * hb + h)),
            pl.BlockSpec((DA_V_DIM, seq), lambda b, h: (TR_AV * DA_HEADS + h, b)),
            pl.BlockSpec((N_META, LANES), lambda b, h: (0, NAT_AK * hb + h)),
            pl.BlockSpec((DA_V_DIM, N_META), lambda b, h: (TR_AV * DA_HEADS + h, 0)),
            pl.BlockSpec((1, LANES), lambda b, h: (0, h)),
        ],
        out_specs=pl.BlockSpec((seq, LANES), lambda b, h: (b, h)),
        out_shape=jax.ShapeDtypeStruct((bsz * seq, DA_WIDTH), BF16),
        scratch_shapes=[
            pltpu.VMEM((2, 2, ATT_KB, ATT_TQ), F32),
            pltpu.VMEM((2, N_META, ATT_TQ), F32),
            pltpu.VMEM((2, _ATT_ACC_ROWS, ATT_TQ), F32),
            pltpu.VMEM((DA_V_DIM, ATT_TQ), F32),
            pltpu.VMEM((_ATT_ACC_ROWS, seq), BF16),
            pltpu.VMEM((_ATT_ACC_ROWS, N_META), BF16),
        ],
        compiler_params=_cparams(("parallel", "parallel")),
        name="diff_attention",
    )(lam_params, nat_x, nat_x, tr_x, nat_m, tr_m, gain)


def _ml_state_update(k, vt, a_wide, b_end, g_row, ct_ref, n_ref, m_ref):
    m_prev = m_ref[0:1, 0:1]
    m_new = jnp.maximum(b_end + m_prev, jnp.max(g_row, axis=1, keepdims=True))
    decay = jnp.exp(b_end + m_prev - m_new)
    kw = (k.astype(F32) * jnp.exp(a_wide + (b_end - m_new))).astype(BF16)
    ct_ref[...] = decay * ct_ref[...] + _dot(vt, kw)
    n_ref[...] = decay * n_ref[...] + jnp.sum(kw.astype(F32), axis=0, keepdims=True)
    m_ref[...] = jnp.broadcast_to(m_new, (SUBLANES, LANES))


def _ml_mask(reverse, chunk):
    row = lax.broadcasted_iota(jnp.int32, (chunk, chunk), 0)
    col = lax.broadcasted_iota(jnp.int32, (chunk, chunk), 1)
    return (row >= col) if reverse else (row <= col)


def _ml_gates(reverse, gt_ref, bc_ref, grow_ref, brow_ref, acols_ref):
    chunk = gt_ref.shape[1]
    grow = gt_ref[...] + bc_ref[...]
    mask = _ml_mask(reverse, chunk)
    brow_all = _cumsum_rows(_log_sigmoid(grow), mask)
    a_rows = grow - pltpu.roll(brow_all, N_GATE_COLS - ML_HEADS, 0)
    a_cols = jnp.concatenate(
        [a_rows, jnp.zeros((LANES - N_GATE_COLS, chunk), F32)], axis=0).T
    grow_ref[...] = grow
    brow_ref[...] = brow_all
    acols_ref[...] = a_cols


def _ml_head(reverse, h, gates, qt_ref, k_ref, vt_ref, ct_ref, n_ref, m_ref, dst_ref):
    grow, brow_all, a_cols = gates
    chunk = k_ref.shape[0]
    mask = _ml_mask(reverse, chunk)
    gate0 = 2 * ML_HEADS if reverse else 0
    end = 0 if reverse else chunk - 1
    gi, gf = gate0 + h, gate0 + ML_HEADS + h
    k = k_ref[:, h * ML_QK_DIM:(h + 1) * ML_QK_DIM]
    qt = qt_ref[h * ML_QK_DIM:(h + 1) * ML_QK_DIM, :]
    vt = vt_ref[h * ML_V_DIM:(h + 1) * ML_V_DIM, :]
    a_wide = jnp.broadcast_to(a_cols[:, gi:gi + 1], (chunk, ML_QK_DIM))
    b_row = brow_all[gf:gf + 1, :]
    li_row = grow[gi:gi + 1, :]
    b_end = b_row[:, end:end + 1]
    m_prev = m_ref[h][0:1, 0:1]

    a_sq = jnp.concatenate([a_wide] * (chunk // ML_QK_DIM), axis=1)
    d = jnp.where(mask, b_row + a_sq, NEG)
    m_inter = b_row + m_prev
    m_t = jnp.maximum(m_inter, jnp.max(d, axis=0, keepdims=True))
    w_inter = jnp.exp(m_inter - m_t)
    sg = _dot(k, qt) * jnp.exp(d - m_t)
    num = w_inter * _dot(ct_ref[h].astype(BF16), qt) + _dot(vt, sg.astype(BF16))
    n16 = jnp.broadcast_to(n_ref[h], (2 * SUBLANES, ML_QK_DIM)).astype(BF16)
    den = w_inter * _dot(n16, qt)[0:1, :] + jnp.sum(sg, axis=0, keepdims=True)
    dst_ref[h * ML_V_DIM:(h + 1) * ML_V_DIM, :] = (
        num / jnp.maximum(jnp.abs(den), jnp.exp(-m_t)))

    _ml_state_update(k, vt, a_wide, b_end, b_end - b_row + li_row,
                     ct_ref.at[h], n_ref.at[h], m_ref.at[h])


def _mlstm_kernel(qtf_ref, kf_ref, vtf_ref, gtf_ref, opf_ref,
                  qtb_ref, kb_ref, vtb_ref, gtb_ref, opb_ref,
                  gtf_next_ref, gtb_next_ref,
                  bc_ref, km_ref, vmt_ref, gm_ref, gmt_ref, br_ref, gain_ref,
                  out_ref, ct_ref, n_ref, m_ref, hf_ref, hb_ref, grow_ref, brow_ref, acols_ref):
    c = pl.program_id(1)
    nc = pl.num_programs(1)
    half = hf_ref.shape[0] - 1

    @pl.when(c == 0)
    def _():
        ct_ref[...] = jnp.zeros_like(ct_ref)
        n_ref[...] = jnp.zeros_like(n_ref)
        m_ref[...] = jnp.zeros_like(m_ref)
        gm = gm_ref[...] + br_ref[...]
        gmt = gmt_ref[...] + bc_ref[...]
        r = lax.broadcasted_iota(jnp.int32, (N_META, N_META), 0)
        s = lax.broadcasted_iota(jnp.int32, (N_META, N_META), 1)
        bcm = _cumsum_cols(_log_sigmoid(gm), s <= r)
        brm = _cumsum_rows(_log_sigmoid(gmt), r <= s)
        for h in range(ML_HEADS):
            gi, gf = h, ML_HEADS + h
            b_end = brm[gf:gf + 1, N_META - 1:N_META]
            _ml_state_update(
                km_ref[0][:, h * ML_QK_DIM:(h + 1) * ML_QK_DIM],
                vmt_ref[h * ML_V_DIM:(h + 1) * ML_V_DIM, :],
                jnp.broadcast_to(gm[:, gi:gi + 1] - bcm[:, gf:gf + 1], (N_META, ML_QK_DIM)), b_end,
                b_end - brm[gf:gf + 1, :] + gmt[gi:gi + 1, :],
                ct_ref.at[0, h], n_ref.at[0, h], m_ref.at[0, h])

    slot = jnp.minimum(c, half)
    def prepare(gtf, gtb):
        _ml_gates(False, gtf, bc_ref, grow_ref.at[0], brow_ref.at[0], acols_ref.at[0])
        _ml_gates(True, gtb, bc_ref, grow_ref.at[1], brow_ref.at[1], acols_ref.at[1])

    @pl.when(c == 0)
    def _():
        prepare(gtf_ref, gtb_ref)

    gates_f = (grow_ref[0], brow_ref[0], acols_ref[0])
    gates_b = (grow_ref[1], brow_ref[1], acols_ref[1])
    prepare(gtf_next_ref, gtb_next_ref)
    for h in range(ML_HEADS):
        _ml_head(False, h, gates_f, qtf_ref, kf_ref, vtf_ref,
                 ct_ref.at[0], n_ref.at[0], m_ref.at[0], hf_ref.at[slot])
        _ml_head(True, h, gates_b, qtb_ref, kb_ref, vtb_ref,
                 ct_ref.at[1], n_ref.at[1], m_ref.at[1], hb_ref.at[slot])

    def finish(ht, op_ref, chunk_idx):
        for h in range(ML_HEADS):
            rows = slice(h * ML_V_DIM, (h + 1) * ML_V_DIM)
            hh = ht[rows, :]
            y = hh * lax.rsqrt(jnp.mean(hh * hh, axis=0, keepdims=True) + EPS) * gain_ref[rows, :]
            out_ref[chunk_idx, rows, :] = (_sigmoid(op_ref[rows, :].astype(F32)) * y).astype(BF16)

    @pl.when(c >= nc - half)
    def _():
        partner = nc - 1 - c
        finish(hf_ref[slot] + hb_ref[partner], opf_ref, c)
        finish(hb_ref[slot] + hf_ref[partner], opb_ref, partner)


def _mlstm(q_t, k_x, tr_x, gates_t, bias_col, k_m, tr_m, gates_m, gates_m_t, bias_row, gain_b,
           bsz, seq):
    nc = seq // ML_CHUNK
    half = nc // 2
    fwd = lambda b, c: b * nc + c
    bwd = lambda b, c: b * nc + (nc - 1 - c)

    def chunk_specs(blk):
        return [
            pl.BlockSpec((ML_QK_WIDTH, ML_CHUNK), lambda b, c: (0, blk(b, c))),
            pl.BlockSpec((ML_CHUNK, ML_QK_WIDTH), lambda b, c: (blk(b, c), 0)),
            pl.BlockSpec((ML_WIDTH, ML_CHUNK), lambda b, c: (TR_MV, blk(b, c))),
            pl.BlockSpec((N_GATE_COLS, ML_CHUNK), lambda b, c: (0, blk(b, c))),
            pl.BlockSpec((ML_WIDTH, ML_CHUNK), lambda b, c: (TR_MO, blk(b, c))),
        ]

    chunk_args = [q_t, k_x, tr_x, gates_t, tr_x]
    nxt = lambda c: jnp.minimum(c + 1, nc - 1)
    in_specs = chunk_specs(fwd) + chunk_specs(bwd) + [
        pl.BlockSpec((N_GATE_COLS, ML_CHUNK), lambda b, c: (0, fwd(b, nxt(c)))),
        pl.BlockSpec((N_GATE_COLS, ML_CHUNK), lambda b, c: (0, bwd(b, nxt(c)))),
        pl.BlockSpec((N_GATE_COLS, 1), lambda b, c: (0, 0)),
        pl.BlockSpec((1, N_META, ML_QK_WIDTH), lambda b, c: (b, 0, 0)),
        pl.BlockSpec((ML_WIDTH, N_META), lambda b, c: (TR_MV, 0)),
        pl.BlockSpec((N_META, LANES), lambda b, c: (0, 0)),
        pl.BlockSpec((N_GATE_COLS, N_META), lambda b, c: (0, 0)),
        pl.BlockSpec((1, LANES), lambda b, c: (0, 0)),
        pl.BlockSpec((ML_WIDTH, ML_CHUNK), lambda b, c: (0, 0)),
    ]
    return pl.pallas_call(
        _mlstm_kernel,
        grid=(bsz, nc),
        in_specs=in_specs,
        out_specs=pl.BlockSpec((nc, ML_WIDTH, ML_CHUNK), lambda b, c: (b, 0, 0)),
        out_shape=jax.ShapeDtypeStruct((bsz * nc, ML_WIDTH, ML_CHUNK), BF16),
        scratch_shapes=[
            pltpu.VMEM((2, ML_HEADS, ML_V_DIM, ML_QK_DIM), F32),
            pltpu.VMEM((2, ML_HEADS, 1, ML_QK_DIM), F32),
            pltpu.VMEM((2, ML_HEADS, SUBLANES, LANES), F32),
            pltpu.VMEM((half + 1, ML_WIDTH, ML_CHUNK), F32),
            pltpu.VMEM((half + 1, ML_WIDTH, ML_CHUNK), F32),
            pltpu.VMEM((2, N_GATE_COLS, ML_CHUNK), F32),
            pltpu.VMEM((2, N_GATE_COLS, ML_CHUNK), F32),
            pltpu.VMEM((2, ML_CHUNK, LANES), F32),
        ],
        compiler_params=_cparams(("parallel", "arbitrary")),
        name="mlstm",
    )(*chunk_args, *chunk_args, gates_t, gates_t,
      bias_col, k_m, tr_m, gates_m, gates_m_t, bias_row, gain_b)


_FFN_TILES = D_FF // FFN_TF


def _mix_ffn_kernel(a_ref, mt_ref, x_ref, wo_ref, g2_ref, gf_ref, wg_hbm, wu_hbm, wd_hbm,
                    o_ref, u_ref, wg_buf, wu_buf, wd_buf, sem):
    def weight_copies(j, slot):
        cols = pl.ds(pl.multiple_of(j * FFN_TF, FFN_TF), FFN_TF)
        return (pltpu.make_async_copy(wg_hbm.at[:, cols], wg_buf.at[slot], sem.at[0, slot]),
                pltpu.make_async_copy(wu_hbm.at[:, cols], wu_buf.at[slot], sem.at[1, slot]),
                pltpu.make_async_copy(wd_hbm.at[cols, :], wd_buf.at[slot], sem.at[2, slot]))

    for cp in weight_copies(0, 0):
        cp.start()

    w_m = wo_ref[DA_WIDTH:DA_WIDTH + ML_WIDTH, :]
    ml = jnp.concatenate([_dot_tn(mt_ref[s], w_m) for s in range(mt_ref.shape[0])], axis=0)
    h = _dot(a_ref[...], wo_ref[0:DA_WIDTH, :]) + ml + x_ref[...]
    o_ref[...] = h
    u = h * lax.rsqrt(jnp.mean(h * h, axis=-1, keepdims=True) + EPS) * g2_ref[...]
    u_ref[...] = u.astype(BF16)

    def step(j, slot, prefetch):
        if prefetch:
            for cp in weight_copies(j + 1, 1 - slot):
                cp.start()
        for cp in weight_copies(j, slot):
            cp.wait()
        u = u_ref[...]
        g = _dot(u, wg_buf[slot])
        ff = (g * _sigmoid(g) * _dot(u, wu_buf[slot])).astype(BF16)
        o_ref[...] += _dot(ff, wd_buf[slot])

    def pair(p, carry):
        step(2 * p, 0, True)
        step(2 * p + 1, 1, True)
        return carry

    lax.fori_loop(0, (_FFN_TILES - 1) // 2, pair, 0)
    step(_FFN_TILES - 1, 0, False)

    y = o_ref[...]
    o_ref[...] = y * lax.rsqrt(jnp.mean(y * y, axis=-1, keepdims=True) + EPS) * gf_ref[...]


def _mix_ffn(attn, ml_t, x_rows, w_out, norm_ffn_g, w_gate, w_up, w_down, norm_final_g):
    assert _FFN_TILES % 2 == 1
    m = x_rows.shape[0]
    return pl.pallas_call(
        _mix_ffn_kernel,
        grid=(m // FFN_TM,),
        in_specs=[
            pl.BlockSpec((FFN_TM, DA_WIDTH), lambda i: (i, 0)),
            pl.BlockSpec((FFN_TM // ML_CHUNK, ML_WIDTH, ML_CHUNK), lambda i: (i, 0, 0)),
            pl.BlockSpec((FFN_TM, D_MODEL), lambda i: (i, 0)),
            pl.BlockSpec((DA_WIDTH + ML_WIDTH, D_MODEL), lambda i: (0, 0)),
            pl.BlockSpec((1, D_MODEL), lambda i: (0, 0)),
            pl.BlockSpec((1, D_MODEL), lambda i: (0, 0)),
            pl.BlockSpec(memory_space=pl.ANY),
            pl.BlockSpec(memory_space=pl.ANY),
            pl.BlockSpec(memory_space=pl.ANY),
        ],
        out_specs=pl.BlockSpec((FFN_TM, D_MODEL), lambda i: (i, 0)),
        out_shape=jax.ShapeDtypeStruct((m, D_MODEL), F32),
        scratch_shapes=[
            pltpu.VMEM((FFN_TM, D_MODEL), BF16),
            pltpu.VMEM((2, D_MODEL, FFN_TF), BF16),
            pltpu.VMEM((2, D_MODEL, FFN_TF), BF16),
            pltpu.VMEM((2, FFN_TF, D_MODEL), BF16),
            pltpu.SemaphoreType.DMA((3, 2)),
        ],
        compiler_params=_cparams(("parallel",)),
        name="mix_ffn",
    )(attn, ml_t, x_rows, w_out, norm_ffn_g, norm_final_g, w_gate, w_up, w_down)


def _rope_tables(pos0, n):
    lane = jnp.arange(LANES) % DA_QK_DIM
    half = DA_ROT_DIM // 2
    inv = ROPE_THETA ** (-jnp.arange(0, DA_ROT_DIM, 2, dtype=F32) / DA_ROT_DIM)
    inv_lane = jnp.where(lane < DA_ROT_DIM, inv[lane % half], 0.0)
    pos = jnp.arange(pos0, pos0 + n, dtype=F32)
    ang = pos[:, None] * inv_lane[None, :]
    sin = jnp.sin(ang)
    sa = jnp.where(lane < half, -sin, 0.0)
    sb = jnp.where((lane >= half) & (lane < DA_ROT_DIM), sin, 0.0)
    return jnp.cos(ang), sa, sb


def _pad_lanes(a, width=LANES):
    return jnp.pad(a, ((0, 0), (0, width - a.shape[1])))


def kernel(x, meta_tokens, norm_mix, w_in, da_lambda_q1, da_lambda_k1, da_lambda_q2, da_lambda_k2,
           da_head_norm, ml_conv_w, ml_conv_b, ml_gate_bias, ml_head_norm, w_out, norm_ffn,
           w_gate, w_up, w_down, norm_final):
    bsz, seq, _ = x.shape
    x_rows = x.reshape(bsz * seq, D_MODEL)

    main_cols = N_NAT * IN_TN + N_TR * IN_TN
    w_main = w_in.astype(BF16)
    w_gates = _pad_lanes(w_in[0][:, main_cols:]).astype(BF16)
    norm_mix0 = norm_mix[0][None, :]

    tabs_x = _rope_tables(N_META, seq)
    tabs_m = _rope_tables(0, N_META)
    nat_x, tr_x, _, gates_t = _inproj(x_rows, norm_mix0, w_main, w_gates, *tabs_x, tm=IN_TM)
    nat_m, tr_m, gates_m, gates_m_t = _inproj(meta_tokens, norm_mix0, w_main, w_gates, *tabs_m,
                                              tm=N_META)

    conv_w = jnp.pad(ml_conv_w[0], ((0, SUBLANES - CONV_W), (0, 0)))
    q_t, k_x, k_m = _conv(nat_x, nat_m, conv_w, ml_conv_b[0][None, :], bsz, seq)

    lam_params = jnp.pad(
        jnp.stack([da_lambda_q1[0], da_lambda_k1[0], da_lambda_q2[0], da_lambda_k2[0]]),
        ((0, SUBLANES - 4), (0, LANES - DA_QK_DIM)))
    attn = _attention(lam_params, nat_x, nat_m, tr_x, tr_m,
                      da_head_norm[0].reshape(1, DA_WIDTH), bsz, seq)

    bias = ml_gate_bias[0].reshape(N_GATE_COLS)
    bias_row = _pad_lanes(bias[None, :])
    bias_col = bias[:, None]
    gain_b = jnp.broadcast_to(ml_head_norm[0].reshape(ML_WIDTH, 1), (ML_WIDTH, ML_CHUNK))
    ml_t = _mlstm(q_t, k_x, tr_x, gates_t, bias_col, k_m, tr_m, gates_m, gates_m_t, bias_row,
                  gain_b, bsz, seq)

    out = _mix_ffn(attn, ml_t, x_rows, w_out[0].astype(BF16), norm_ffn[0][None, :],
                   w_gate[0].astype(BF16), w_up[0].astype(BF16), w_down[0].astype(BF16),
                   norm_final[None, :])
    return out.reshape(bsz, seq, D_MODEL)
```

```python
import jax
import jax.numpy as jnp
from jax import lax
from jax.experimental import pallas as pl
from jax.experimental.pallas import tpu as pltpu

F32 = jnp.float32
BF16 = jnp.bfloat16

D_MODEL = 2048
N_META = 16
EPS = 1e-6
ROPE_THETA = 500000.0
NEG = -1e30

DA_QK_DIM = 64
DA_V_DIM = 128
DA_HEADS = 8
DA_ROT_DIM = 16
DA_WIDTH = DA_HEADS * DA_V_DIM
LAM_INIT = 0.8 - 0.6 * 1.0
LOG2_E = 1.4426950408889634

ML_HEADS = 4
ML_V_DIM = 256
ML_QK_DIM = 128
ML_WIDTH = ML_HEADS * ML_V_DIM
ML_QK_WIDTH = ML_HEADS * ML_QK_DIM
CONV_W = 5
N_GATES = 4
N_GATE_COLS = N_GATES * ML_HEADS

D_FF = 5632

IN_TN = 1024
NAT_AQ, NAT_AK, NAT_MQK = 0, 1, 2
TR_AV, TR_MV, TR_MO = 0, 1, 2
N_NAT = 3
N_TR = 3

LANES = 128
SUBLANES = 8
BF16_SUBLANES = 16
VMEM_LIMIT = 56 * 1024 * 1024

IN_TM = 512
ATT_TQ = 256
ATT_KB = 512
ATT_UNROLL = 8
ML_CHUNK = 256
FFN_TM = 512
FFN_TF = 512
CONV_ROWS = 512


def _cparams(sem):
    return pltpu.CompilerParams(dimension_semantics=sem, vmem_limit_bytes=VMEM_LIMIT)


def _dot(a, b):
    return jnp.dot(a, b, preferred_element_type=F32)


def _dot_nt(a, b):
    return lax.dot_general(a, b, (((1,), (1,)), ((), ())), preferred_element_type=F32)


def _dot_tn(a, b):
    return lax.dot_general(a, b, (((0,), (0,)), ((), ())), preferred_element_type=F32)


def _bf16_pieces(x):
    hi = x.astype(BF16)
    rest = x - hi.astype(F32)
    mid = rest.astype(BF16)
    return hi, mid, (rest - mid.astype(F32)).astype(BF16)


def _cumsum_rows(x, visible):
    r = x.shape[0]
    ones = jnp.where(visible, 1.0, 0.0).astype(BF16)
    parts = _dot(jnp.concatenate(_bf16_pieces(x), axis=0), ones)
    return parts[0:r] + parts[r:2 * r] + parts[2 * r:3 * r]


def _cumsum_cols(x, visible):
    c = x.shape[1]
    ones = jnp.where(visible, 1.0, 0.0).astype(BF16)
    parts = _dot(ones, jnp.concatenate(_bf16_pieces(x), axis=1))
    return parts[:, 0:c] + parts[:, c:2 * c] + parts[:, 2 * c:3 * c]


def _sigmoid(x):
    return 1.0 / (1.0 + jnp.exp(-x))


def _log_sigmoid(x):
    return jnp.minimum(x, 0.0) - jnp.log(1.0 + jnp.exp(-jnp.abs(x)))


_NAT_COLS = (0, 1024, 3072)
_TR_COLS = (2048, 4096, 5120)


def _inproj_kernel(x_ref, g_ref, w_ref, wg_ref, cos_ref, sa_ref, sb_ref,
                   nat_ref, tr_ref, gate_ref, gate_t_ref):
    x = x_ref[...]
    ms = jnp.mean(x * x, axis=-1, keepdims=True)
    u = (x * lax.rsqrt(ms + EPS) * g_ref[...]).astype(BF16)
    gates = _dot(u, wg_ref[...])
    gate_ref[...] = gates
    gate_t_ref[...] = gates.T[0:N_GATE_COLS, :]

    def tile(col0):
        return _dot(u, w_ref[0, :, col0:col0 + IN_TN])

    for jj in (NAT_AQ, NAT_AK):
        acc = tile(_NAT_COLS[jj])
        scale = DA_QK_DIM ** -0.5 * LOG2_E if jj == NAT_AQ else 1.0
        cos = cos_ref[...] * scale
        sa = sa_ref[...] * scale
        sb = sb_ref[...] * scale
        for h in range(DA_HEADS):
            t = acc[:, h * LANES:(h + 1) * LANES]
            r = (t * cos + pltpu.roll(t, LANES - DA_ROT_DIM // 2, 1) * sa
                 + pltpu.roll(t, DA_ROT_DIM // 2, 1) * sb)
            nat_ref[:, jj * IN_TN + h * LANES:jj * IN_TN + (h + 1) * LANES] = r.astype(BF16)

    nat_ref[:, NAT_MQK * IN_TN:(NAT_MQK + 1) * IN_TN] = tile(_NAT_COLS[NAT_MQK]).astype(BF16)

    for jj in range(N_TR):
        tr_ref[jj * IN_TN:(jj + 1) * IN_TN, :] = tile(_TR_COLS[jj]).T.astype(BF16)


def _inproj(rows, norm_g, w_main, w_gate, cos_t, sa_t, sb_t, tm):
    m = rows.shape[0]
    tiles_per_seq = cos_t.shape[0] // tm
    tab_spec = pl.BlockSpec((tm, LANES), lambda i: (i % tiles_per_seq, 0))
    return pl.pallas_call(
        _inproj_kernel,
        grid=(m // tm,),
        in_specs=[
            pl.BlockSpec((tm, D_MODEL), lambda i: (i, 0)),
            pl.BlockSpec((1, D_MODEL), lambda i: (0, 0)),
            pl.BlockSpec(w_main.shape, lambda i: (0, 0, 0)),
            pl.BlockSpec((D_MODEL, LANES), lambda i: (0, 0)),
            tab_spec, tab_spec, tab_spec,
        ],
        out_specs=[
            pl.BlockSpec((tm, N_NAT * IN_TN), lambda i: (i, 0)),
            pl.BlockSpec((N_TR * IN_TN, tm), lambda i: (0, i)),
            pl.BlockSpec((tm, LANES), lambda i: (i, 0)),
            pl.BlockSpec((N_GATE_COLS, tm), lambda i: (0, i)),
        ],
        out_shape=[
            jax.ShapeDtypeStruct((m, N_NAT * IN_TN), BF16),
            jax.ShapeDtypeStruct((N_TR * IN_TN, m), BF16),
            jax.ShapeDtypeStruct((m, LANES), F32),
            jax.ShapeDtypeStruct((N_GATE_COLS, m), F32),
        ],
        compiler_params=_cparams(("parallel",)),
        name="inproj",
    )(rows, norm_g, w_main, w_gate, cos_t, sa_t, sb_t)


_CONV_PAD = SUBLANES


def _conv_kernel(x_ref, m_ref, w_ref, b_ref, qt_ref, k_ref, km_ref, s_ref):
    seq = x_ref.shape[0]
    j = pl.program_id(1)
    zeros = jnp.zeros((_CONV_PAD, LANES), F32)
    s_ref[0:_CONV_PAD, :] = zeros
    s_ref[_CONV_PAD:_CONV_PAD + N_META, :] = m_ref[...].astype(F32)
    s_ref[_CONV_PAD + N_META:_CONV_PAD + N_META + seq, :] = x_ref[...].astype(F32)
    s_ref[_CONV_PAD + N_META + seq:, :] = zeros
    w = w_ref[...]
    bias = b_ref[...]

    def conv(pos, n):
        acc = jnp.zeros((n, LANES), F32) + bias
        for t in range(CONV_W):
            start = _CONV_PAD + pos + t - CONV_W // 2
            acc = acc + s_ref[start:start + n, :] * w[t:t + 1, :]
        return acc * _sigmoid(acc)

    @pl.when(j < ML_HEADS)
    def _():
        for c in range(seq // CONV_ROWS):
            y = conv(N_META + c * CONV_ROWS, CONV_ROWS) * (ML_QK_DIM ** -0.5)
            qt_ref[:, c * CONV_ROWS:(c + 1) * CONV_ROWS] = y.T.astype(BF16)

    @pl.when(j >= ML_HEADS)
    def _():
        km_ref[0] = conv(0, N_META).astype(BF16)
        for c in range(seq // CONV_ROWS):
            k_ref[c * CONV_ROWS:(c + 1) * CONV_ROWS, :] = conv(
                N_META + c * CONV_ROWS, CONV_ROWS).astype(BF16)


def _conv(nat_x, nat_m, conv_w, conv_b, bsz, seq):
    col0 = NAT_MQK * (IN_TN // LANES)
    q_j = lambda j: jnp.minimum(j, ML_HEADS - 1)
    k_j = lambda j: jnp.maximum(j - ML_HEADS, 0)
    return pl.pallas_call(
        _conv_kernel,
        grid=(bsz, 2 * ML_HEADS),
        in_specs=[
            pl.BlockSpec((seq, LANES), lambda b, j: (b, col0 + j)),
            pl.BlockSpec((N_META, LANES), lambda b, j: (0, col0 + j)),
            pl.BlockSpec((SUBLANES, LANES), lambda b, j: (0, j)),
            pl.BlockSpec((1, LANES), lambda b, j: (0, j)),
        ],
        out_specs=[
            pl.BlockSpec((ML_QK_DIM, seq), lambda b, j: (q_j(j), b)),
            pl.BlockSpec((seq, ML_QK_DIM), lambda b, j: (b, k_j(j))),
            pl.BlockSpec((1, N_META, ML_QK_DIM), lambda b, j: (b, 0, k_j(j))),
        ],
        out_shape=[
            jax.ShapeDtypeStruct((ML_QK_WIDTH, bsz * seq), BF16),
            jax.ShapeDtypeStruct((bsz * seq, ML_QK_WIDTH), BF16),
            jax.ShapeDtypeStruct((bsz, N_META, ML_QK_WIDTH), BF16),
        ],
        scratch_shapes=[pltpu.VMEM((seq + N_META + 2 * _CONV_PAD, LANES), F32)],
        compiler_params=_cparams(("parallel", "arbitrary")),
        name="mlstm_conv",
    )(nat_x, nat_m, conv_w, conv_b)


_ATT_ACC_ROWS = DA_V_DIM + BF16_SUBLANES


def _attn_kernel(lam_ref, q_ref, k_ref, vt_ref, km_ref, vmt_ref, gain_ref, o_ref,
                 s_ref, sm_ref, acc_ref, o_t_ref, vte_ref, vmte_ref):
    tq = ATT_TQ
    nq = q_ref.shape[0] // tq
    nkb = k_ref.shape[0] // ATT_KB
    lp = lam_ref[...]
    lam = (jnp.exp(jnp.sum(lp[0:1] * lp[1:2], axis=1, keepdims=True))
           - jnp.exp(jnp.sum(lp[2:3] * lp[3:4], axis=1, keepdims=True)) + LAM_INIT)
    lane = lax.broadcasted_iota(jnp.int32, (tq, LANES), 1)
    comps = range(2)

    def q_rows(qt):
        return pl.ds(pl.multiple_of(qt * tq, tq), tq)

    def masked_q(qt):
        q = q_ref[q_rows(qt), :]
        zero = jnp.zeros_like(q)
        return (jnp.where(lane < DA_QK_DIM, q, zero), jnp.where(lane >= DA_QK_DIM, q, zero))

    def fold(a):
        return a.reshape(a.shape[0] // SUBLANES, SUBLANES, tq)

    def scores(qz, kb, slot):
        k = k_ref[kb * ATT_KB:(kb + 1) * ATT_KB, :]
        bmax = []
        for c in comps:
            s = _dot_nt(k, qz[c])
            s_ref[slot, c] = s
            bm = jnp.max(fold(s), axis=0)
            if kb == 0:
                sm = _dot_nt(km_ref[...], qz[c])
                sm_ref[c] = sm
                bm = jnp.maximum(bm, jnp.max(fold(sm), axis=0))
            bmax.append(bm)
        return tuple(bmax)

    def probs(s, m):
        return jnp.exp2(s - m).astype(BF16)

    def consume(kb, slot, bmax, m):
        vte = vte_ref[:, kb * ATT_KB:(kb + 1) * ATT_KB]
        m_out = []
        for c in comps:
            bm = jnp.max(bmax[c], axis=0, keepdims=True)
            if kb == 0:
                m_new = bm
                acc_ref[c] = (_dot(vte, probs(s_ref[slot, c], m_new))
                              + _dot(vmte_ref[...], probs(sm_ref[c], m_new)))
            else:
                m_new = jnp.maximum(m[c], bm)
                alpha = jnp.exp2(m[c] - m_new)
                acc_ref[c] = alpha * acc_ref[c] + _dot(vte, probs(s_ref[slot, c], m_new))
            m_out.append(m_new)
        return tuple(m_out)

    def finalize(qt):
        o = o_t_ref[...].T
        y = (o * lax.rsqrt(jnp.mean(o * o, axis=1, keepdims=True) + EPS)
             * gain_ref[...] * (1.0 - LAM_INIT))
        o_ref[q_rows(qt), :] = y.astype(BF16)

    def tile(qt, bmax):
        finalize(jnp.maximum(qt - 1, 0))
        qz = masked_q(qt)
        m = None
        for kb in range(nkb):
            if kb + 1 < nkb:
                bmax_next = scores(qz, kb + 1, (kb + 1) % 2)
            else:
                bmax_next = scores(masked_q(jnp.minimum(qt + 1, nq - 1)), 0, 0)
            m = consume(kb, kb % 2, bmax, m)
            bmax = bmax_next
        l1 = acc_ref[0, DA_V_DIM:DA_V_DIM + 1, :]
        l2 = acc_ref[1, DA_V_DIM:DA_V_DIM + 1, :]
        o_t_ref[...] = (acc_ref[0, 0:DA_V_DIM, :] / l1
                        - lam * (acc_ref[1, 0:DA_V_DIM, :] / l2))
        return bmax

    ones_row = jnp.where(
        lax.broadcasted_iota(jnp.int32, (_ATT_ACC_ROWS - DA_V_DIM, 1), 0) == 0, 1.0, 0.0)
    vte_ref[0:DA_V_DIM, :] = vt_ref[...]
    vte_ref[DA_V_DIM:, :] = jnp.broadcast_to(ones_row, (_ATT_ACC_ROWS - DA_V_DIM, vte_ref.shape[1])
                                             ).astype(BF16)
    vmte_ref[0:DA_V_DIM, :] = vmt_ref[...]
    vmte_ref[DA_V_DIM:, :] = jnp.broadcast_to(ones_row, (_ATT_ACC_ROWS - DA_V_DIM, N_META)
                                              ).astype(BF16)
    o_t_ref[...] = jnp.zeros_like(o_t_ref)
    lax.fori_loop(0, nq, tile, scores(masked_q(0), 0, 0), unroll=ATT_UNROLL)
    finalize(nq - 1)


def _attention(lam_params, nat_x, nat_m, tr_x, tr_m, gain, bsz, seq):
    hb = IN_TN // LANES
    return pl.pallas_call(
        _attn_kernel,
        grid=(bsz, DA_HEADS),
        in_specs=[
            pl.BlockSpec((SUBLANES, LANES), lambda b, h: (0, 0)),
            pl.BlockSpec((seq, LANES), lambda b, h: (b, NAT_AQ * hb + h)),
            pl.BlockSpec((seq, LANES), lambda b, h: (b, NAT_AK * hb + h)),
            pl.BlockSpec((DA_V_DIM, seq), lambda b, h: (TR_AV * DA_HEADS + h, b)),
            pl.BlockSpec((N_META, LANES), lambda b, h: (0, NAT_AK * hb + h)),
            pl.BlockSpec((DA_V_DIM, N_META), lambda b, h: (TR_AV * DA_HEADS + h, 0)),
            pl.BlockSpec((1, LANES), lambda b, h: (0, h)),
        ],
        out_specs=pl.BlockSpec((seq, LANES), lambda b, h: (b, h)),
        out_shape=jax.ShapeDtypeStruct((bsz * seq, DA_WIDTH), BF16),
        scratch_shapes=[
            pltpu.VMEM((2, 2, ATT_KB, ATT_TQ), F32),
            pltpu.VMEM((2, N_META, ATT_TQ), F32),
            pltpu.VMEM((2, _ATT_ACC_ROWS, ATT_TQ), F32),
            pltpu.VMEM((DA_V_DIM, ATT_TQ), F32),
            pltpu.VMEM((_ATT_ACC_ROWS, seq), BF16),
            pltpu.VMEM((_ATT_ACC_ROWS, N_META), BF16),
        ],
        compiler_params=_cparams(("parallel", "parallel")),
        name="diff_attention",
    )(lam_params, nat_x, nat_x, tr_x, nat_m, tr_m, gain)


def _ml_state_update(k, vt, a_wide, b_end, g_row, ct_ref, n_ref, m_ref):
    m_prev = m_ref[0:1, 0:1]
    m_new = jnp.maximum(b_end + m_prev, jnp.max(g_row, axis=1, keepdims=True))
    decay = jnp.exp(b_end + m_prev - m_new)
    kw = (k.astype(F32) * jnp.exp(a_wide + (b_end - m_new))).astype(BF16)
    ct_ref[...] = decay * ct_ref[...] + _dot(vt, kw)
    n_ref[...] = decay * n_ref[...] + jnp.sum(kw.astype(F32), axis=0, keepdims=True)
    m_ref[...] = jnp.broadcast_to(m_new, (SUBLANES, LANES))


def _ml_mask(reverse, chunk):
    row = lax.broadcasted_iota(jnp.int32, (chunk, chunk), 0)
    col = lax.broadcasted_iota(jnp.int32, (chunk, chunk), 1)
    return (row >= col) if reverse else (row <= col)


def _ml_gates(reverse, gt_ref, bc_ref, grow_ref, brow_ref, acols_ref):
    chunk = gt_ref.shape[1]
    grow = gt_ref[...] + bc_ref[...]
    mask = _ml_mask(reverse, chunk)
    brow_all = _cumsum_rows(_log_sigmoid(grow), mask)
    a_rows = grow - pltpu.roll(brow_all, N_GATE_COLS - ML_HEADS, 0)
    a_cols = jnp.concatenate(
        [a_rows, jnp.zeros((LANES - N_GATE_COLS, chunk), F32)], axis=0).T
    grow_ref[...] = grow
    brow_ref[...] = brow_all
    acols_ref[...] = a_cols


def _ml_head(reverse, h, gates, qt_ref, k_ref, vt_ref, ct_ref, n_ref, m_ref, dst_ref):
    grow, brow_all, a_cols = gates
    chunk = k_ref.shape[0]
    mask = _ml_mask(reverse, chunk)
    gate0 = 2 * ML_HEADS if reverse else 0
    end = 0 if reverse else chunk - 1
    gi, gf = gate0 + h, gate0 + ML_HEADS + h
    k = k_ref[:, h * ML_QK_DIM:(h + 1) * ML_QK_DIM]
    qt = qt_ref[h * ML_QK_DIM:(h + 1) * ML_QK_DIM, :]
    vt = vt_ref[h * ML_V_DIM:(h + 1) * ML_V_DIM, :]
    a_wide = jnp.broadcast_to(a_cols[:, gi:gi + 1], (chunk, ML_QK_DIM))
    b_row = brow_all[gf:gf + 1, :]
    li_row = grow[gi:gi + 1, :]
    b_end = b_row[:, end:end + 1]
    m_prev = m_ref[h][0:1, 0:1]

    a_sq = jnp.concatenate([a_wide] * (chunk // ML_QK_DIM), axis=1)
    d = jnp.where(mask, b_row + a_sq, NEG)
    m_inter = b_row + m_prev
    m_t = jnp.maximum(m_inter, jnp.max(d, axis=0, keepdims=True))
    w_inter = jnp.exp(m_inter - m_t)
    sg = _dot(k, qt) * jnp.exp(d - m_t)
    num = w_inter * _dot(ct_ref[h].astype(BF16), qt) + _dot(vt, sg.astype(BF16))
    n_rows = jnp.broadcast_to(n_ref[h], (BF16_SUBLANES, ML_QK_DIM)).astype(BF16)
    den = w_inter * _dot(n_rows, qt)[0:1, :] + jnp.sum(sg, axis=0, keepdims=True)
    dst_ref[h * ML_V_DIM:(h + 1) * ML_V_DIM, :] = (
        num / jnp.maximum(jnp.abs(den), jnp.exp(-m_t)))

    _ml_state_update(k, vt, a_wide, b_end, b_end - b_row + li_row,
                     ct_ref.at[h], n_ref.at[h], m_ref.at[h])


def _mlstm_kernel(qtf_ref, kf_ref, vtf_ref, gtf_ref, opf_ref,
                  qtb_ref, kb_ref, vtb_ref, gtb_ref, opb_ref,
                  gtf_next_ref, gtb_next_ref,
                  bc_ref, km_ref, vmt_ref, gm_ref, gmt_ref, br_ref, gain_ref,
                  out_ref, ct_ref, n_ref, m_ref, hf_ref, hb_ref, grow_ref, brow_ref, acols_ref):
    c = pl.program_id(1)
    nc = pl.num_programs(1)
    half = hf_ref.shape[0] - 1

    @pl.when(c == 0)
    def _():
        ct_ref[...] = jnp.zeros_like(ct_ref)
        n_ref[...] = jnp.zeros_like(n_ref)
        m_ref[...] = jnp.zeros_like(m_ref)
        gm = gm_ref[...] + br_ref[...]
        gmt = gmt_ref[...] + bc_ref[...]
        r = lax.broadcasted_iota(jnp.int32, (N_META, N_META), 0)
        s = lax.broadcasted_iota(jnp.int32, (N_META, N_META), 1)
        bcm = _cumsum_cols(_log_sigmoid(gm), s <= r)
        brm = _cumsum_rows(_log_sigmoid(gmt), r <= s)
        for h in range(ML_HEADS):
            gi, gf = h, ML_HEADS + h
            b_end = brm[gf:gf + 1, N_META - 1:N_META]
            _ml_state_update(
                km_ref[0][:, h * ML_QK_DIM:(h + 1) * ML_QK_DIM],
                vmt_ref[h * ML_V_DIM:(h + 1) * ML_V_DIM, :],
                jnp.broadcast_to(gm[:, gi:gi + 1] - bcm[:, gf:gf + 1], (N_META, ML_QK_DIM)), b_end,
                b_end - brm[gf:gf + 1, :] + gmt[gi:gi + 1, :],
                ct_ref.at[0, h], n_ref.at[0, h], m_ref.at[0, h])

    slot = jnp.minimum(c, half)
    def prepare(gtf, gtb):
        _ml_gates(False, gtf, bc_ref, grow_ref.at[0], brow_ref.at[0], acols_ref.at[0])
        _ml_gates(True, gtb, bc_ref, grow_ref.at[1], brow_ref.at[1], acols_ref.at[1])

    @pl.when(c == 0)
    def _():
        prepare(gtf_ref, gtb_ref)

    gates_f = (grow_ref[0], brow_ref[0], acols_ref[0])
    gates_b = (grow_ref[1], brow_ref[1], acols_ref[1])
    prepare(gtf_next_ref, gtb_next_ref)
    for h in range(ML_HEADS):
        _ml_head(False, h, gates_f, qtf_ref, kf_ref, vtf_ref,
                 ct_ref.at[0], n_ref.at[0], m_ref.at[0], hf_ref.at[slot])
        _ml_head(True, h, gates_b, qtb_ref, kb_ref, vtb_ref,
                 ct_ref.at[1], n_ref.at[1], m_ref.at[1], hb_ref.at[slot])

    def finish(ht, op_ref, chunk_idx):
        for h in range(ML_HEADS):
            rows = slice(h * ML_V_DIM, (h + 1) * ML_V_DIM)
            hh = ht[rows, :]
            y = hh * lax.rsqrt(jnp.mean(hh * hh, axis=0, keepdims=True) + EPS) * gain_ref[rows, :]
            out_ref[chunk_idx, rows, :] = (_sigmoid(op_ref[rows, :].astype(F32)) * y).astype(BF16)

    @pl.when(c >= nc - half)
    def _():
        partner = nc - 1 - c
        finish(hf_ref[slot] + hb_ref[partner], opf_ref, c)
        finish(hb_ref[slot] + hf_ref[partner], opb_ref, partner)


def _mlstm(q_t, k_x, tr_x, gates_t, bias_col, k_m, tr_m, gates_m, gates_m_t, bias_row, gain_b,
           bsz, seq):
    nc = seq // ML_CHUNK
    half = nc // 2
    fwd = lambda b, c: b * nc + c
    bwd = lambda b, c: b * nc + (nc - 1 - c)

    def chunk_specs(blk):
        return [
            pl.BlockSpec((ML_QK_WIDTH, ML_CHUNK), lambda b, c: (0, blk(b, c))),
            pl.BlockSpec((ML_CHUNK, ML_QK_WIDTH), lambda b, c: (blk(b, c), 0)),
            pl.BlockSpec((ML_WIDTH, ML_CHUNK), lambda b, c: (TR_MV, blk(b, c))),
            pl.BlockSpec((N_GATE_COLS, ML_CHUNK), lambda b, c: (0, blk(b, c))),
            pl.BlockSpec((ML_WIDTH, ML_CHUNK), lambda b, c: (TR_MO, blk(b, c))),
        ]

    chunk_args = [q_t, k_x, tr_x, gates_t, tr_x]
    nxt = lambda c: jnp.minimum(c + 1, nc - 1)
    in_specs = chunk_specs(fwd) + chunk_specs(bwd) + [
        pl.BlockSpec((N_GATE_COLS, ML_CHUNK), lambda b, c: (0, fwd(b, nxt(c)))),
        pl.BlockSpec((N_GATE_COLS, ML_CHUNK), lambda b, c: (0, bwd(b, nxt(c)))),
        pl.BlockSpec((N_GATE_COLS, 1), lambda b, c: (0, 0)),
        pl.BlockSpec((1, N_META, ML_QK_WIDTH), lambda b, c: (b, 0, 0)),
        pl.BlockSpec((ML_WIDTH, N_META), lambda b, c: (TR_MV, 0)),
        pl.BlockSpec((N_META, LANES), lambda b, c: (0, 0)),
        pl.BlockSpec((N_GATE_COLS, N_META), lambda b, c: (0, 0)),
        pl.BlockSpec((1, LANES), lambda b, c: (0, 0)),
        pl.BlockSpec((ML_WIDTH, ML_CHUNK), lambda b, c: (0, 0)),
    ]
    return pl.pallas_call(
        _mlstm_kernel,
        grid=(bsz, nc),
        in_specs=in_specs,
        out_specs=pl.BlockSpec((nc, ML_WIDTH, ML_CHUNK), lambda b, c: (b, 0, 0)),
        out_shape=jax.ShapeDtypeStruct((bsz * nc, ML_WIDTH, ML_CHUNK), BF16),
        scratch_shapes=[
            pltpu.VMEM((2, ML_HEADS, ML_V_DIM, ML_QK_DIM), F32),
            pltpu.VMEM((2, ML_HEADS, 1, ML_QK_DIM), F32),
            pltpu.VMEM((2, ML_HEADS, SUBLANES, LANES), F32),
            pltpu.VMEM((half + 1, ML_WIDTH, ML_CHUNK), F32),
            pltpu.VMEM((half + 1, ML_WIDTH, ML_CHUNK), F32),
            pltpu.VMEM((2, N_GATE_COLS, ML_CHUNK), F32),
            pltpu.VMEM((2, N_GATE_COLS, ML_CHUNK), F32),
            pltpu.VMEM((2, ML_CHUNK, LANES), F32),
        ],
        compiler_params=_cparams(("parallel", "arbitrary")),
        name="mlstm",
    )(*chunk_args, *chunk_args, gates_t, gates_t,
      bias_col, k_m, tr_m, gates_m, gates_m_t, bias_row, gain_b)


_FFN_TILES = D_FF // FFN_TF


def _mix_ffn_kernel(a_ref, mt_ref, x_ref, wo_ref, g2_ref, gf_ref, wg_hbm, wu_hbm, wd_hbm,
                    o_ref, u_ref, wg_buf, wu_buf, wd_buf, sem):
    def weight_copies(j, slot):
        cols = pl.ds(pl.multiple_of(j * FFN_TF, FFN_TF), FFN_TF)
        return (pltpu.make_async_copy(wg_hbm.at[:, cols], wg_buf.at[slot], sem.at[0, slot]),
                pltpu.make_async_copy(wu_hbm.at[:, cols], wu_buf.at[slot], sem.at[1, slot]),
                pltpu.make_async_copy(wd_hbm.at[cols, :], wd_buf.at[slot], sem.at[2, slot]))

    for cp in weight_copies(0, 0):
        cp.start()

    w_m = wo_ref[DA_WIDTH:DA_WIDTH + ML_WIDTH, :]
    ml = jnp.concatenate([_dot_tn(mt_ref[s], w_m) for s in range(mt_ref.shape[0])], axis=0)
    h = _dot(a_ref[...], wo_ref[0:DA_WIDTH, :]) + ml + x_ref[...]
    o_ref[...] = h
    u = h * lax.rsqrt(jnp.mean(h * h, axis=-1, keepdims=True) + EPS) * g2_ref[...]
    u_ref[...] = u.astype(BF16)

    def step(j, slot, prefetch):
        if prefetch:
            for cp in weight_copies(j + 1, 1 - slot):
                cp.start()
        for cp in weight_copies(j, slot):
            cp.wait()
        u = u_ref[...]
        g = _dot(u, wg_buf[slot])
        ff = (g * _sigmoid(g) * _dot(u, wu_buf[slot])).astype(BF16)
        o_ref[...] += _dot(ff, wd_buf[slot])

    def pair(p, carry):
        step(2 * p, 0, True)
        step(2 * p + 1, 1, True)
        return carry

    lax.fori_loop(0, (_FFN_TILES - 1) // 2, pair, 0)
    step(_FFN_TILES - 1, 0, False)

    y = o_ref[...]
    o_ref[...] = y * lax.rsqrt(jnp.mean(y * y, axis=-1, keepdims=True) + EPS) * gf_ref[...]


def _mix_ffn(attn, ml_t, x_rows, w_out, norm_ffn_g, w_gate, w_up, w_down, norm_final_g):
    assert _FFN_TILES % 2 == 1
    m = x_rows.shape[0]
    return pl.pallas_call(
        _mix_ffn_kernel,
        grid=(m // FFN_TM,),
        in_specs=[
            pl.BlockSpec((FFN_TM, DA_WIDTH), lambda i: (i, 0)),
            pl.BlockSpec((FFN_TM // ML_CHUNK, ML_WIDTH, ML_CHUNK), lambda i: (i, 0, 0)),
            pl.BlockSpec((FFN_TM, D_MODEL), lambda i: (i, 0)),
            pl.BlockSpec((DA_WIDTH + ML_WIDTH, D_MODEL), lambda i: (0, 0)),
            pl.BlockSpec((1, D_MODEL), lambda i: (0, 0)),
            pl.BlockSpec((1, D_MODEL), lambda i: (0, 0)),
            pl.BlockSpec(memory_space=pl.ANY),
            pl.BlockSpec(memory_space=pl.ANY),
            pl.BlockSpec(memory_space=pl.ANY),
        ],
        out_specs=pl.BlockSpec((FFN_TM, D_MODEL), lambda i: (i, 0)),
        out_shape=jax.ShapeDtypeStruct((m, D_MODEL), F32),
        scratch_shapes=[
            pltpu.VMEM((FFN_TM, D_MODEL), BF16),
            pltpu.VMEM((2, D_MODEL, FFN_TF), BF16),
            pltpu.VMEM((2, D_MODEL, FFN_TF), BF16),
            pltpu.VMEM((2, FFN_TF, D_MODEL), BF16),
            pltpu.SemaphoreType.DMA((3, 2)),
        ],
        compiler_params=_cparams(("parallel",)),
        name="mix_ffn",
    )(attn, ml_t, x_rows, w_out, norm_ffn_g, norm_final_g, w_gate, w_up, w_down)


def _rope_tables(pos0, n):
    lane = jnp.arange(LANES) % DA_QK_DIM
    half = DA_ROT_DIM // 2
    inv = ROPE_THETA ** (-jnp.arange(0, DA_ROT_DIM, 2, dtype=F32) / DA_ROT_DIM)
    inv_lane = jnp.where(lane < DA_ROT_DIM, inv[lane % half], 0.0)
    pos = jnp.arange(pos0, pos0 + n, dtype=F32)
    ang = pos[:, None] * inv_lane[None, :]
    sin = jnp.sin(ang)
    sa = jnp.where(lane < half, -sin, 0.0)
    sb = jnp.where((lane >= half) & (lane < DA_ROT_DIM), sin, 0.0)
    return jnp.cos(ang), sa, sb


def _pad_lanes(a, width=LANES):
    return jnp.pad(a, ((0, 0), (0, width - a.shape[1])))


def kernel(x, meta_tokens, norm_mix, w_in, da_lambda_q1, da_lambda_k1, da_lambda_q2, da_lambda_k2,
           da_head_norm, ml_conv_w, ml_conv_b, ml_gate_bias, ml_head_norm, w_out, norm_ffn,
           w_gate, w_up, w_down, norm_final):
    bsz, seq, _ = x.shape
    x_rows = x.reshape(bsz * seq, D_MODEL)

    main_cols = N_NAT * IN_TN + N_TR * IN_TN
    w_main = w_in.astype(BF16)
    w_gates = _pad_lanes(w_in[0][:, main_cols:]).astype(BF16)
    norm_mix0 = norm_mix[0][None, :]

    tabs_x = _rope_tables(N_META, seq)
    tabs_m = _rope_tables(0, N_META)
    nat_x, tr_x, _, gates_t = _inproj(x_rows, norm_mix0, w_main, w_gates, *tabs_x, tm=IN_TM)
    nat_m, tr_m, gates_m, gates_m_t = _inproj(meta_tokens, norm_mix0, w_main, w_gates, *tabs_m,
                                              tm=N_META)

    conv_w = jnp.pad(ml_conv_w[0], ((0, SUBLANES - CONV_W), (0, 0)))
    q_t, k_x, k_m = _conv(nat_x, nat_m, conv_w, ml_conv_b[0][None, :], bsz, seq)

    lam_params = jnp.pad(
        jnp.stack([da_lambda_q1[0], da_lambda_k1[0], da_lambda_q2[0], da_lambda_k2[0]]),
        ((0, SUBLANES - 4), (0, LANES - DA_QK_DIM)))
    attn = _attention(lam_params, nat_x, nat_m, tr_x, tr_m,
                      da_head_norm[0].reshape(1, DA_WIDTH), bsz, seq)

    bias = ml_gate_bias[0].reshape(N_GATE_COLS)
    bias_row = _pad_lanes(bias[None, :])
    bias_col = bias[:, None]
    gain_b = jnp.broadcast_to(ml_head_norm[0].reshape(ML_WIDTH, 1), (ML_WIDTH, ML_CHUNK))
    ml_t = _mlstm(q_t, k_x, tr_x, gates_t, bias_col, k_m, tr_m, gates_m, gates_m_t, bias_row,
                  gain_b, bsz, seq)

    out = _mix_ffn(attn, ml_t, x_rows, w_out[0].astype(BF16), norm_ffn[0][None, :],
                   w_gate[0].astype(BF16), w_up[0].astype(BF16), w_down[0].astype(BF16),
                   norm_final[None, :])
    return out.reshape(bsz, seq, D_MODEL)
```

```python
import jax
import jax.numpy as jnp
from jax import lax
from jax.experimental import pallas as pl
from jax.experimental.pallas import tpu as pltpu

F32 = jnp.float32
BF16 = jnp.bfloat16

D_MODEL = 2048
N_META = 16
EPS = 1e-6
ROPE_THETA = 500000.0
NEG = -1e30

DA_QK_DIM = 64
DA_V_DIM = 128
DA_HEADS = 8
DA_ROT_DIM = 16
DA_WIDTH = DA_HEADS * DA_V_DIM
LAM_INIT = 0.8 - 0.6 * 1.0
LOG2_E = 1.4426950408889634

ML_HEADS = 4
ML_V_DIM = 256
ML_QK_DIM = 128
ML_WIDTH = ML_HEADS * ML_V_DIM
ML_QK_WIDTH = ML_HEADS * ML_QK_DIM
CONV_W = 5
N_GATES = 4
N_GATE_COLS = N_GATES * ML_HEADS

D_FF = 5632

IN_TN = 1024
NAT_AQ, NAT_AK, NAT_MQK = 0, 1, 2
TR_AV, TR_MV, TR_MO = 0, 1, 2
N_NAT = 3
N_TR = 3

LANES = 128
SUBLANES = 8
BF16_SUBLANES = 16
VMEM_LIMIT = 56 * 1024 * 1024

IN_TM = 512
ATT_TQ = 256
ATT_KB = 512
ATT_UNROLL = 8
ML_CHUNK = 256
FFN_TM = 512
FFN_TF = 512
CONV_ROWS = 512


def _cparams(sem):
    return pltpu.CompilerParams(dimension_semantics=sem, vmem_limit_bytes=VMEM_LIMIT)


def _dot(a, b):
    return jnp.dot(a, b, preferred_element_type=F32)


def _dot_nt(a, b):
    return lax.dot_general(a, b, (((1,), (1,)), ((), ())), preferred_element_type=F32)


def _dot_tn(a, b):
    return lax.dot_general(a, b, (((0,), (0,)), ((), ())), preferred_element_type=F32)


def _bf16_pieces(x):
    hi = x.astype(BF16)
    rest = x - hi.astype(F32)
    mid = rest.astype(BF16)
    return hi, mid, (rest - mid.astype(F32)).astype(BF16)


def _cumsum_rows(x, visible):
    r = x.shape[0]
    ones = jnp.where(visible, 1.0, 0.0).astype(BF16)
    parts = _dot(jnp.concatenate(_bf16_pieces(x), axis=0), ones)
    return parts[0:r] + parts[r:2 * r] + parts[2 * r:3 * r]


def _cumsum_cols(x, visible):
    c = x.shape[1]
    ones = jnp.where(visible, 1.0, 0.0).astype(BF16)
    parts = _dot(ones, jnp.concatenate(_bf16_pieces(x), axis=1))
    return parts[:, 0:c] + parts[:, c:2 * c] + parts[:, 2 * c:3 * c]


def _sigmoid(x):
    return 1.0 / (1.0 + jnp.exp(-x))


def _log_sigmoid(x):
    return jnp.minimum(x, 0.0) - jnp.log(1.0 + jnp.exp(-jnp.abs(x)))


_NAT_COLS = (0, 1024, 3072)
_TR_COLS = (2048, 4096, 5120)


def _inproj_kernel(x_ref, g_ref, w_ref, wg_ref, cos_ref, sa_ref, sb_ref,
                   nat_ref, tr_ref, gate_ref, gate_t_ref):
    x = x_ref[...]
    ms = jnp.mean(x * x, axis=-1, keepdims=True)
    u = (x * lax.rsqrt(ms + EPS) * g_ref[...]).astype(BF16)
    gates = _dot(u, wg_ref[...])
    gate_ref[...] = gates
    gate_t_ref[...] = gates.T[0:N_GATE_COLS, :]

    def tile(col0):
        return _dot(u, w_ref[0, :, col0:col0 + IN_TN])

    for jj in (NAT_AQ, NAT_AK):
        acc = tile(_NAT_COLS[jj])
        scale = DA_QK_DIM ** -0.5 * LOG2_E if jj == NAT_AQ else 1.0
        cos = cos_ref[...] * scale
        sa = sa_ref[...] * scale
        sb = sb_ref[...] * scale
        for h in range(DA_HEADS):
            t = acc[:, h * LANES:(h + 1) * LANES]
            r = (t * cos + pltpu.roll(t, LANES - DA_ROT_DIM // 2, 1) * sa
                 + pltpu.roll(t, DA_ROT_DIM // 2, 1) * sb)
            nat_ref[:, jj * IN_TN + h * LANES:jj * IN_TN + (h + 1) * LANES] = r.astype(BF16)

    nat_ref[:, NAT_MQK * IN_TN:(NAT_MQK + 1) * IN_TN] = tile(_NAT_COLS[NAT_MQK]).astype(BF16)

    for jj in range(N_TR):
        tr_ref[jj * IN_TN:(jj + 1) * IN_TN, :] = tile(_TR_COLS[jj]).T.astype(BF16)


def _inproj(rows, norm_g, w_main, w_gate, cos_t, sa_t, sb_t, tm):
    m = rows.shape[0]
    tiles_per_seq = cos_t.shape[0] // tm
    tab_spec = pl.BlockSpec((tm, LANES), lambda i: (i % tiles_per_seq, 0))
    return pl.pallas_call(
        _inproj_kernel,
        grid=(m // tm,),
        in_specs=[
            pl.BlockSpec((tm, D_MODEL), lambda i: (i, 0)),
            pl.BlockSpec((1, D_MODEL), lambda i: (0, 0)),
            pl.BlockSpec(w_main.shape, lambda i: (0, 0, 0)),
            pl.BlockSpec((D_MODEL, LANES), lambda i: (0, 0)),
            tab_spec, tab_spec, tab_spec,
        ],
        out_specs=[
            pl.BlockSpec((tm, N_NAT * IN_TN), lambda i: (i, 0)),
            pl.BlockSpec((N_TR * IN_TN, tm), lambda i: (0, i)),
            pl.BlockSpec((tm, LANES), lambda i: (i, 0)),
            pl.BlockSpec((N_GATE_COLS, tm), lambda i: (0, i)),
        ],
        out_shape=[
            jax.ShapeDtypeStruct((m, N_NAT * IN_TN), BF16),
            jax.ShapeDtypeStruct((N_TR * IN_TN, m), BF16),
            jax.ShapeDtypeStruct((m, LANES), F32),
            jax.ShapeDtypeStruct((N_GATE_COLS, m), F32),
        ],
        compiler_params=_cparams(("parallel",)),
        name="inproj",
    )(rows, norm_g, w_main, w_gate, cos_t, sa_t, sb_t)


_CONV_PAD = SUBLANES


def _conv_kernel(x_ref, m_ref, w_ref, b_ref, qt_ref, k_ref, km_ref, s_ref):
    seq = x_ref.shape[0]
    j = pl.program_id(1)
    zeros = jnp.zeros((_CONV_PAD, LANES), F32)
    s_ref[0:_CONV_PAD, :] = zeros
    s_ref[_CONV_PAD:_CONV_PAD + N_META, :] = m_ref[...].astype(F32)
    s_ref[_CONV_PAD + N_META:_CONV_PAD + N_META + seq, :] = x_ref[...].astype(F32)
    s_ref[_CONV_PAD + N_META + seq:, :] = zeros
    w = w_ref[...]
    bias = b_ref[...]

    def conv(pos, n):
        acc = bias
        for t in range(CONV_W):
            start = _CONV_PAD + pos + t - CONV_W // 2
            acc = acc + s_ref[start:start + n, :] * w[t:t + 1, :]
        return acc * (0.5 * jnp.tanh(0.5 * acc) + 0.5)

    @pl.when(j < ML_HEADS)
    def _():
        for c in range(seq // CONV_ROWS):
            y = conv(N_META + c * CONV_ROWS, CONV_ROWS) * (ML_QK_DIM ** -0.5)
            qt_ref[:, c * CONV_ROWS:(c + 1) * CONV_ROWS] = y.T.astype(BF16)

    @pl.when(j >= ML_HEADS)
    def _():
        km_ref[0] = conv(0, N_META).astype(BF16)
        for c in range(seq // CONV_ROWS):
            k_ref[c * CONV_ROWS:(c + 1) * CONV_ROWS, :] = conv(
                N_META + c * CONV_ROWS, CONV_ROWS).astype(BF16)


def _conv(nat_x, nat_m, conv_w, conv_b, bsz, seq):
    col0 = NAT_MQK * (IN_TN // LANES)
    q_j = lambda j: jnp.minimum(j, ML_HEADS - 1)
    k_j = lambda j: jnp.maximum(j - ML_HEADS, 0)
    return pl.pallas_call(
        _conv_kernel,
        grid=(bsz, 2 * ML_HEADS),
        in_specs=[
            pl.BlockSpec((seq, LANES), lambda b, j: (b, col0 + j)),
            pl.BlockSpec((N_META, LANES), lambda b, j: (0, col0 + j)),
            pl.BlockSpec((SUBLANES, LANES), lambda b, j: (0, j)),
            pl.BlockSpec((1, LANES), lambda b, j: (0, j)),
        ],
        out_specs=[
            pl.BlockSpec((ML_QK_DIM, seq), lambda b, j: (q_j(j), b)),
            pl.BlockSpec((seq, ML_QK_DIM), lambda b, j: (b, k_j(j))),
            pl.BlockSpec((1, N_META, ML_QK_DIM), lambda b, j: (b, 0, k_j(j))),
        ],
        out_shape=[
            jax.ShapeDtypeStruct((ML_QK_WIDTH, bsz * seq), BF16),
            jax.ShapeDtypeStruct((bsz * seq, ML_QK_WIDTH), BF16),
            jax.ShapeDtypeStruct((bsz, N_META, ML_QK_WIDTH), BF16),
        ],
        scratch_shapes=[pltpu.VMEM((seq + N_META + 2 * _CONV_PAD, LANES), F32)],
        compiler_params=_cparams(("parallel", "arbitrary")),
        name="mlstm_conv",
    )(nat_x, nat_m, conv_w, conv_b)


_ATT_ACC_ROWS = DA_V_DIM + BF16_SUBLANES


def _attn_kernel(lam_ref, q_ref, k_ref, vt_ref, km_ref, vmt_ref, gain_ref, o_ref,
                 s_ref, sm_ref, acc_ref, o_t_ref, vte_ref, vmte_ref):
    tq = ATT_TQ
    nq = q_ref.shape[0] // tq
    nkb = k_ref.shape[0] // ATT_KB
    lp = lam_ref[...]
    lam = (jnp.exp(jnp.sum(lp[0:1] * lp[1:2], axis=1, keepdims=True))
           - jnp.exp(jnp.sum(lp[2:3] * lp[3:4], axis=1, keepdims=True)) + LAM_INIT)
    lane = lax.broadcasted_iota(jnp.int32, (tq, LANES), 1)
    comps = range(2)

    def q_rows(qt):
        return pl.ds(pl.multiple_of(qt * tq, tq), tq)

    def masked_q(qt):
        q = q_ref[q_rows(qt), :]
        zero = jnp.zeros_like(q)
        return (jnp.where(lane < DA_QK_DIM, q, zero), jnp.where(lane >= DA_QK_DIM, q, zero))

    def fold(a):
        return a.reshape(a.shape[0] // SUBLANES, SUBLANES, tq)

    def scores(qz, kb, slot):
        k = k_ref[kb * ATT_KB:(kb + 1) * ATT_KB, :]
        bmax = []
        for c in comps:
            s = _dot_nt(k, qz[c])
            s_ref[slot, c] = s
            bm = jnp.max(fold(s), axis=0)
            if kb == 0:
                sm = _dot_nt(km_ref[...], qz[c])
                sm_ref[c] = sm
                bm = jnp.maximum(bm, jnp.max(fold(sm), axis=0))
            bmax.append(bm)
        return tuple(bmax)

    def probs(s, m):
        return jnp.exp2(s - m).astype(BF16)

    def consume(kb, slot, bmax, m):
        vte = vte_ref[:, kb * ATT_KB:(kb + 1) * ATT_KB]
        m_out = []
        for c in comps:
            bm = jnp.max(bmax[c], axis=0, keepdims=True)
            if kb == 0:
                m_new = bm
                acc_ref[c] = (_dot(vte, probs(s_ref[slot, c], m_new))
                              + _dot(vmte_ref[...], probs(sm_ref[c], m_new)))
            else:
                m_new = jnp.maximum(m[c], bm)
                alpha = jnp.exp2(m[c] - m_new)
                acc_ref[c] = alpha * acc_ref[c] + _dot(vte, probs(s_ref[slot, c], m_new))
            m_out.append(m_new)
        return tuple(m_out)

    def finalize(qt):
        o = o_t_ref[...].T
        y = (o * lax.rsqrt(jnp.mean(o * o, axis=1, keepdims=True) + EPS)
             * gain_ref[...] * (1.0 - LAM_INIT))
        o_ref[q_rows(qt), :] = y.astype(BF16)

    def tile(qt, bmax):
        finalize(jnp.maximum(qt - 1, 0))
        qz = masked_q(qt)
        m = None
        for kb in range(nkb):
            if kb + 1 < nkb:
                bmax_next = scores(qz, kb + 1, (kb + 1) % 2)
            else:
                bmax_next = scores(masked_q(jnp.minimum(qt + 1, nq - 1)), 0, 0)
            m = consume(kb, kb % 2, bmax, m)
            bmax = bmax_next
        l1 = acc_ref[0, DA_V_DIM:DA_V_DIM + 1, :]
        l2 = acc_ref[1, DA_V_DIM:DA_V_DIM + 1, :]
        o_t_ref[...] = (acc_ref[0, 0:DA_V_DIM, :] / l1
                        - lam * (acc_ref[1, 0:DA_V_DIM, :] / l2))
        return bmax

    ones_row = jnp.where(
        lax.broadcasted_iota(jnp.int32, (_ATT_ACC_ROWS - DA_V_DIM, 1), 0) == 0, 1.0, 0.0)
    vte_ref[0:DA_V_DIM, :] = vt_ref[...]
    vte_ref[DA_V_DIM:, :] = jnp.broadcast_to(ones_row, (_ATT_ACC_ROWS - DA_V_DIM, vte_ref.shape[1])
                                             ).astype(BF16)
    vmte_ref[0:DA_V_DIM, :] = vmt_ref[...]
    vmte_ref[DA_V_DIM:, :] = jnp.broadcast_to(ones_row, (_ATT_ACC_ROWS - DA_V_DIM, N_META)
                                              ).astype(BF16)
    o_t_ref[...] = jnp.zeros_like(o_t_ref)
    lax.fori_loop(0, nq, tile, scores(masked_q(0), 0, 0), unroll=ATT_UNROLL)
    finalize(nq - 1)


def _attention(lam_params, nat_x, nat_m, tr_x, tr_m, gain, bsz, seq):
    hb = IN_TN // LANES
    return pl.pallas_call(
        _attn_kernel,
        grid=(bsz, DA_HEADS),
        in_specs=[
            pl.BlockSpec((SUBLANES, LANES), lambda b, h: (0, 0)),
            pl.BlockSpec((seq, LANES), lambda b, h: (b, NAT_AQ * hb + h)),
            pl.BlockSpec((seq, LANES), lambda b, h: (b, NAT_AK * hb + h)),
            pl.BlockSpec((DA_V_DIM, seq), lambda b, h: (TR_AV * DA_HEADS + h, b)),
            pl.BlockSpec((N_META, LANES), lambda b, h: (0, NAT_AK * hb + h)),
            pl.BlockSpec((DA_V_DIM, N_META), lambda b, h: (TR_AV * DA_HEADS + h, 0)),
            pl.BlockSpec((1, LANES), lambda b, h: (0, h)),
        ],
        out_specs=pl.BlockSpec((seq, LANES), lambda b, h: (b, h)),
        out_shape=jax.ShapeDtypeStruct((bsz * seq, DA_WIDTH), BF16),
        scratch_shapes=[
            pltpu.VMEM((2, 2, ATT_KB, ATT_TQ), F32),
            pltpu.VMEM((2, N_META, ATT_TQ), F32),
            pltpu.VMEM((2, _ATT_ACC_ROWS, ATT_TQ), F32),
            pltpu.VMEM((DA_V_DIM, ATT_TQ), F32),
            pltpu.VMEM((_ATT_ACC_ROWS, seq), BF16),
            pltpu.VMEM((_ATT_ACC_ROWS, N_META), BF16),
        ],
        compiler_params=_cparams(("parallel", "parallel")),
        name="diff_attention",
    )(lam_params, nat_x, nat_x, tr_x, nat_m, tr_m, gain)


def _ml_state_update(k, vt, a_wide, b_end, g_row, ct_ref, n_ref, m_ref):
    m_prev = m_ref[0:1, 0:1]
    m_new = jnp.maximum(b_end + m_prev, jnp.max(g_row, axis=1, keepdims=True))
    decay = jnp.exp2(b_end + m_prev - m_new)
    kw = (k.astype(F32) * jnp.exp2(a_wide + (b_end - m_new))).astype(BF16)
    ct_ref[...] = decay * ct_ref[...] + _dot(vt, kw)
    n_ref[...] = decay * n_ref[...] + jnp.sum(kw.astype(F32), axis=0, keepdims=True)
    m_ref[...] = jnp.broadcast_to(m_new, (SUBLANES, LANES))


def _ml_mask(reverse, chunk):
    row = lax.broadcasted_iota(jnp.int32, (chunk, chunk), 0)
    col = lax.broadcasted_iota(jnp.int32, (chunk, chunk), 1)
    return (row >= col) if reverse else (row <= col)


def _ml_gates(reverse, gt_ref, bc_ref, grow_ref, brow_ref, acols_ref):
    chunk = gt_ref.shape[1]
    gates = gt_ref[...] + bc_ref[...]
    grow = gates * LOG2_E
    mask = _ml_mask(reverse, chunk)
    brow_all = _cumsum_rows(_log_sigmoid(gates) * LOG2_E, mask)
    a_rows = grow - pltpu.roll(brow_all, N_GATE_COLS - ML_HEADS, 0)
    a_cols = jnp.concatenate(
        [a_rows, jnp.zeros((LANES - N_GATE_COLS, chunk), F32)], axis=0).T
    grow_ref[...] = grow
    brow_ref[...] = brow_all
    acols_ref[...] = a_cols


def _ml_head(reverse, h, gates, qt_ref, k_ref, vt_ref, ct_ref, n_ref, m_ref, dst_ref):
    grow, brow_all, a_cols = gates
    chunk = k_ref.shape[0]
    mask = _ml_mask(reverse, chunk)
    gate0 = 2 * ML_HEADS if reverse else 0
    end = 0 if reverse else chunk - 1
    gi, gf = gate0 + h, gate0 + ML_HEADS + h
    k = k_ref[:, h * ML_QK_DIM:(h + 1) * ML_QK_DIM]
    qt = qt_ref[h * ML_QK_DIM:(h + 1) * ML_QK_DIM, :]
    vt = vt_ref[h * ML_V_DIM:(h + 1) * ML_V_DIM, :]
    a_wide = jnp.broadcast_to(a_cols[:, gi:gi + 1], (chunk, ML_QK_DIM))
    b_row = brow_all[gf:gf + 1, :]
    li_row = grow[gi:gi + 1, :]
    b_end = b_row[:, end:end + 1]
    m_prev = m_ref[h][0:1, 0:1]

    a_sq = jnp.concatenate([a_wide] * (chunk // ML_QK_DIM), axis=1)
    d = jnp.where(mask, b_row + a_sq, NEG)
    m_inter = b_row + m_prev
    m_t = jnp.maximum(m_inter, jnp.max(d, axis=0, keepdims=True))
    w_inter = jnp.exp2(m_inter - m_t)
    sg = _dot(k, qt) * jnp.exp2(d - m_t)
    num = w_inter * _dot(ct_ref[h].astype(BF16), qt) + _dot(vt, sg.astype(BF16))
    n_rows = jnp.broadcast_to(n_ref[h], (BF16_SUBLANES, ML_QK_DIM)).astype(BF16)
    den = w_inter * _dot(n_rows, qt)[0:1, :] + jnp.sum(sg, axis=0, keepdims=True)
    dst_ref[h * ML_V_DIM:(h + 1) * ML_V_DIM, :] = (
        num / jnp.maximum(jnp.abs(den), jnp.exp2(-m_t)))

    _ml_state_update(k, vt, a_wide, b_end, b_end - b_row + li_row,
                     ct_ref.at[h], n_ref.at[h], m_ref.at[h])


def _mlstm_kernel(qtf_ref, kf_ref, vtf_ref, gtf_ref, opf_ref,
                  qtb_ref, kb_ref, vtb_ref, gtb_ref, opb_ref,
                  gtf_next_ref, gtb_next_ref,
                  bc_ref, km_ref, vmt_ref, gm_ref, gmt_ref, br_ref, gain_ref,
                  out_ref, ct_ref, n_ref, m_ref, hf_ref, hb_ref, grow_ref, brow_ref, acols_ref):
    c = pl.program_id(1)
    nc = pl.num_programs(1)
    half = hf_ref.shape[0] - 1

    @pl.when(c == 0)
    def _():
        ct_ref[...] = jnp.zeros_like(ct_ref)
        n_ref[...] = jnp.zeros_like(n_ref)
        m_ref[...] = jnp.zeros_like(m_ref)
        gates_m = gm_ref[...] + br_ref[...]
        gates_mt = gmt_ref[...] + bc_ref[...]
        gm = gates_m * LOG2_E
        gmt = gates_mt * LOG2_E
        r = lax.broadcasted_iota(jnp.int32, (N_META, N_META), 0)
        s = lax.broadcasted_iota(jnp.int32, (N_META, N_META), 1)
        bcm = _cumsum_cols(_log_sigmoid(gates_m) * LOG2_E, s <= r)
        brm = _cumsum_rows(_log_sigmoid(gates_mt) * LOG2_E, r <= s)
        for h in range(ML_HEADS):
            gi, gf = h, ML_HEADS + h
            b_end = brm[gf:gf + 1, N_META - 1:N_META]
            _ml_state_update(
                km_ref[0][:, h * ML_QK_DIM:(h + 1) * ML_QK_DIM],
                vmt_ref[h * ML_V_DIM:(h + 1) * ML_V_DIM, :],
                jnp.broadcast_to(gm[:, gi:gi + 1] - bcm[:, gf:gf + 1], (N_META, ML_QK_DIM)), b_end,
                b_end - brm[gf:gf + 1, :] + gmt[gi:gi + 1, :],
                ct_ref.at[0, h], n_ref.at[0, h], m_ref.at[0, h])

    slot = jnp.minimum(c, half)
    def prepare(gtf, gtb):
        _ml_gates(False, gtf, bc_ref, grow_ref.at[0], brow_ref.at[0], acols_ref.at[0])
        _ml_gates(True, gtb, bc_ref, grow_ref.at[1], brow_ref.at[1], acols_ref.at[1])

    @pl.when(c == 0)
    def _():
        prepare(gtf_ref, gtb_ref)

    gates_f = (grow_ref[0], brow_ref[0], acols_ref[0])
    gates_b = (grow_ref[1], brow_ref[1], acols_ref[1])
    prepare(gtf_next_ref, gtb_next_ref)
    for h in range(ML_HEADS):
        _ml_head(False, h, gates_f, qtf_ref, kf_ref, vtf_ref,
                 ct_ref.at[0], n_ref.at[0], m_ref.at[0], hf_ref.at[slot])
        _ml_head(True, h, gates_b, qtb_ref, kb_ref, vtb_ref,
                 ct_ref.at[1], n_ref.at[1], m_ref.at[1], hb_ref.at[slot])

    def finish(ht, op_ref, chunk_idx):
        for h in range(ML_HEADS):
            rows = slice(h * ML_V_DIM, (h + 1) * ML_V_DIM)
            hh = ht[rows, :]
            y = hh * lax.rsqrt(jnp.mean(hh * hh, axis=0, keepdims=True) + EPS) * gain_ref[rows, :]
            out_ref[chunk_idx, rows, :] = (_sigmoid(op_ref[rows, :].astype(F32)) * y).astype(BF16)

    @pl.when(c >= nc - half)
    def _():
        partner = nc - 1 - c
        finish(hf_ref[slot] + hb_ref[partner], opf_ref, c)
        finish(hb_ref[slot] + hf_ref[partner], opb_ref, partner)


def _mlstm(q_t, k_x, tr_x, gates_t, bias_col, k_m, tr_m, gates_m, gates_m_t, bias_row, gain_b,
           bsz, seq):
    nc = seq // ML_CHUNK
    half = nc // 2
    fwd = lambda b, c: b * nc + c
    bwd = lambda b, c: b * nc + (nc - 1 - c)

    def chunk_specs(blk):
        return [
            pl.BlockSpec((ML_QK_WIDTH, ML_CHUNK), lambda b, c: (0, blk(b, c))),
            pl.BlockSpec((ML_CHUNK, ML_QK_WIDTH), lambda b, c: (blk(b, c), 0)),
            pl.BlockSpec((ML_WIDTH, ML_CHUNK), lambda b, c: (TR_MV, blk(b, c))),
            pl.BlockSpec((N_GATE_COLS, ML_CHUNK), lambda b, c: (0, blk(b, c))),
            pl.BlockSpec((ML_WIDTH, ML_CHUNK), lambda b, c: (TR_MO, blk(b, c))),
        ]

    chunk_args = [q_t, k_x, tr_x, gates_t, tr_x]
    nxt = lambda c: jnp.minimum(c + 1, nc - 1)
    in_specs = chunk_specs(fwd) + chunk_specs(bwd) + [
        pl.BlockSpec((N_GATE_COLS, ML_CHUNK), lambda b, c: (0, fwd(b, nxt(c)))),
        pl.BlockSpec((N_GATE_COLS, ML_CHUNK), lambda b, c: (0, bwd(b, nxt(c)))),
        pl.BlockSpec((N_GATE_COLS, 1), lambda b, c: (0, 0)),
        pl.BlockSpec((1, N_META, ML_QK_WIDTH), lambda b, c: (b, 0, 0)),
        pl.BlockSpec((ML_WIDTH, N_META), lambda b, c: (TR_MV, 0)),
        pl.BlockSpec((N_META, LANES), lambda b, c: (0, 0)),
        pl.BlockSpec((N_GATE_COLS, N_META), lambda b, c: (0, 0)),
        pl.BlockSpec((1, LANES), lambda b, c: (0, 0)),
        pl.BlockSpec((ML_WIDTH, ML_CHUNK), lambda b, c: (0, 0)),
    ]
    return pl.pallas_call(
        _mlstm_kernel,
        grid=(bsz, nc),
        in_specs=in_specs,
        out_specs=pl.BlockSpec((nc, ML_WIDTH, ML_CHUNK), lambda b, c: (b, 0, 0)),
        out_shape=jax.ShapeDtypeStruct((bsz * nc, ML_WIDTH, ML_CHUNK), BF16),
        scratch_shapes=[
            pltpu.VMEM((2, ML_HEADS, ML_V_DIM, ML_QK_DIM), F32),
            pltpu.VMEM((2, ML_HEADS, 1, ML_QK_DIM), F32),
            pltpu.VMEM((2, ML_HEADS, SUBLANES, LANES), F32),
            pltpu.VMEM((half + 1, ML_WIDTH, ML_CHUNK), F32),
            pltpu.VMEM((half + 1, ML_WIDTH, ML_CHUNK), F32),
            pltpu.VMEM((2, N_GATE_COLS, ML_CHUNK), F32),
            pltpu.VMEM((2, N_GATE_COLS, ML_CHUNK), F32),
            pltpu.VMEM((2, ML_CHUNK, LANES), F32),
        ],
        compiler_params=_cparams(("parallel", "arbitrary")),
        name="mlstm",
    )(*chunk_args, *chunk_args, gates_t, gates_t,
      bias_col, k_m, tr_m, gates_m, gates_m_t, bias_row, gain_b)


_FFN_TILES = D_FF // FFN_TF


def _mix_ffn_kernel(a_ref, mt_ref, x_ref, wo_ref, g2_ref, gf_ref, wg_hbm, wu_hbm, wd_hbm,
                    o_ref, u_ref, wg_buf, wu_buf, wd_buf, sem):
    def weight_copies(j, slot):
        cols = pl.ds(pl.multiple_of(j * FFN_TF, FFN_TF), FFN_TF)
        return (pltpu.make_async_copy(wg_hbm.at[:, cols], wg_buf.at[slot], sem.at[0, slot]),
                pltpu.make_async_copy(wu_hbm.at[:, cols], wu_buf.at[slot], sem.at[1, slot]),
                pltpu.make_async_copy(wd_hbm.at[cols, :], wd_buf.at[slot], sem.at[2, slot]))

    for cp in weight_copies(0, 0):
        cp.start()

    w_m = wo_ref[DA_WIDTH:DA_WIDTH + ML_WIDTH, :]
    ml = jnp.concatenate([_dot_tn(mt_ref[s], w_m) for s in range(mt_ref.shape[0])], axis=0)
    h = _dot(a_ref[...], wo_ref[0:DA_WIDTH, :]) + ml + x_ref[...]
    o_ref[...] = h
    u = h * lax.rsqrt(jnp.mean(h * h, axis=-1, keepdims=True) + EPS) * g2_ref[...]
    u_ref[...] = u.astype(BF16)

    def step(j, slot, prefetch):
        if prefetch:
            for cp in weight_copies(j + 1, 1 - slot):
                cp.start()
        for cp in weight_copies(j, slot):
            cp.wait()
        u = u_ref[...]
        g = _dot(u, wg_buf[slot])
        ff = (g * _sigmoid(g) * _dot(u, wu_buf[slot])).astype(BF16)
        o_ref[...] += _dot(ff, wd_buf[slot])

    def pair(p, carry):
        step(2 * p, 0, True)
        step(2 * p + 1, 1, True)
        return carry

    lax.fori_loop(0, (_FFN_TILES - 1) // 2, pair, 0)
    step(_FFN_TILES - 1, 0, False)

    y = o_ref[...]
    o_ref[...] = y * lax.rsqrt(jnp.mean(y * y, axis=-1, keepdims=True) + EPS) * gf_ref[...]


def _mix_ffn(attn, ml_t, x_rows, w_out, norm_ffn_g, w_gate, w_up, w_down, norm_final_g):
    assert _FFN_TILES % 2 == 1
    m = x_rows.shape[0]
    return pl.pallas_call(
        _mix_ffn_kernel,
        grid=(m // FFN_TM,),
        in_specs=[
            pl.BlockSpec((FFN_TM, DA_WIDTH), lambda i: (i, 0)),
            pl.BlockSpec((FFN_TM // ML_CHUNK, ML_WIDTH, ML_CHUNK), lambda i: (i, 0, 0)),
            pl.BlockSpec((FFN_TM, D_MODEL), lambda i: (i, 0)),
            pl.BlockSpec((DA_WIDTH + ML_WIDTH, D_MODEL), lambda i: (0, 0)),
            pl.BlockSpec((1, D_MODEL), lambda i: (0, 0)),
            pl.BlockSpec((1, D_MODEL), lambda i: (0, 0)),
            pl.BlockSpec(memory_space=pl.ANY),
            pl.BlockSpec(memory_space=pl.ANY),
            pl.BlockSpec(memory_space=pl.ANY),
        ],
        out_specs=pl.BlockSpec((FFN_TM, D_MODEL), lambda i: (i, 0)),
        out_shape=jax.ShapeDtypeStruct((m, D_MODEL), F32),
        scratch_shapes=[
            pltpu.VMEM((FFN_TM, D_MODEL), BF16),
            pltpu.VMEM((2, D_MODEL, FFN_TF), BF16),
            pltpu.VMEM((2, D_MODEL, FFN_TF), BF16),
            pltpu.VMEM((2, FFN_TF, D_MODEL), BF16),
            pltpu.SemaphoreType.DMA((3, 2)),
        ],
        compiler_params=_cparams(("parallel",)),
        name="mix_ffn",
    )(attn, ml_t, x_rows, w_out, norm_ffn_g, norm_final_g, w_gate, w_up, w_down)


def _rope_tables(pos0, n):
    lane = jnp.arange(LANES) % DA_QK_DIM
    half = DA_ROT_DIM // 2
    inv = ROPE_THETA ** (-jnp.arange(0, DA_ROT_DIM, 2, dtype=F32) / DA_ROT_DIM)
    inv_lane = jnp.where(lane < DA_ROT_DIM, inv[lane % half], 0.0)
    pos = jnp.arange(pos0, pos0 + n, dtype=F32)
    ang = pos[:, None] * inv_lane[None, :]
    sin = jnp.sin(ang)
    sa = jnp.where(lane < half, -sin, 0.0)
    sb = jnp.where((lane >= half) & (lane < DA_ROT_DIM), sin, 0.0)
    return jnp.cos(ang), sa, sb


def _pad_lanes(a, width=LANES):
    return jnp.pad(a, ((0, 0), (0, width - a.shape[1])))


def kernel(x, meta_tokens, norm_mix, w_in, da_lambda_q1, da_lambda_k1, da_lambda_q2, da_lambda_k2,
           da_head_norm, ml_conv_w, ml_conv_b, ml_gate_bias, ml_head_norm, w_out, norm_ffn,
           w_gate, w_up, w_down, norm_final):
    bsz, seq, _ = x.shape
    x_rows = x.reshape(bsz * seq, D_MODEL)

    main_cols = N_NAT * IN_TN + N_TR * IN_TN
    w_main = w_in.astype(BF16)
    w_gates = _pad_lanes(w_in[0][:, main_cols:]).astype(BF16)
    norm_mix0 = norm_mix[0][None, :]

    tabs_x = _rope_tables(N_META, seq)
    tabs_m = _rope_tables(0, N_META)
    nat_x, tr_x, _, gates_t = _inproj(x_rows, norm_mix0, w_main, w_gates, *tabs_x, tm=IN_TM)
    nat_m, tr_m, gates_m, gates_m_t = _inproj(meta_tokens, norm_mix0, w_main, w_gates, *tabs_m,
                                              tm=N_META)

    conv_w = jnp.pad(ml_conv_w[0], ((0, SUBLANES - CONV_W), (0, 0)))
    q_t, k_x, k_m = _conv(nat_x, nat_m, conv_w, ml_conv_b[0][None, :], bsz, seq)

    lam_params = jnp.pad(
        jnp.stack([da_lambda_q1[0], da_lambda_k1[0], da_lambda_q2[0], da_lambda_k2[0]]),
        ((0, SUBLANES - 4), (0, LANES - DA_QK_DIM)))
    attn = _attention(lam_params, nat_x, nat_m, tr_x, tr_m,
                      da_head_norm[0].reshape(1, DA_WIDTH), bsz, seq)

    bias = ml_gate_bias[0].reshape(N_GATE_COLS)
    bias_row = _pad_lanes(bias[None, :])
    bias_col = bias[:, None]
    gain_b = jnp.broadcast_to(ml_head_norm[0].reshape(ML_WIDTH, 1), (ML_WIDTH, ML_CHUNK))
    ml_t = _mlstm(q_t, k_x, tr_x, gates_t, bias_col, k_m, tr_m, gates_m, gates_m_t, bias_row,
                  gain_b, bsz, seq)

    out = _mix_ffn(attn, ml_t, x_rows, w_out[0].astype(BF16), norm_ffn[0][None, :],
                   w_gate[0].astype(BF16), w_up[0].astype(BF16), w_down[0].astype(BF16),
                   norm_final[None, :])
    return out.reshape(bsz, seq, D_MODEL)
```

```python
import jax
import jax.numpy as jnp
from jax import lax
from jax.experimental import pallas as pl
from jax.experimental.pallas import tpu as pltpu

F32 = jnp.float32
BF16 = jnp.bfloat16

D_MODEL = 2048
N_META = 16
EPS = 1e-6
ROPE_THETA = 500000.0
NEG = -1e30

DA_QK_DIM = 64
DA_V_DIM = 128
DA_HEADS = 8
DA_ROT_DIM = 16
DA_WIDTH = DA_HEADS * DA_V_DIM
LAM_INIT = 0.8 - 0.6 * 1.0
LOG2_E = 1.4426950408889634

ML_HEADS = 4
ML_V_DIM = 256
ML_QK_DIM = 128
ML_WIDTH = ML_HEADS * ML_V_DIM
ML_QK_WIDTH = ML_HEADS * ML_QK_DIM
CONV_W = 5
N_GATES = 4
N_GATE_COLS = N_GATES * ML_HEADS

D_FF = 5632

IN_TN = 1024
NAT_AQ, NAT_AK, NAT_MQK = 0, 1, 2
TR_AV, TR_MV, TR_MO = 0, 1, 2
N_NAT = 3
N_TR = 3

LANES = 128
SUBLANES = 8
BF16_SUBLANES = 16
VMEM_LIMIT = 56 * 1024 * 1024

IN_TM = 512
ATT_TQ = 256
ATT_KB = 512
ATT_UNROLL = 8
ML_CHUNK = 256
FFN_TM = 512
FFN_TF = 512
CONV_ROWS = 512


def _cparams(sem):
    return pltpu.CompilerParams(dimension_semantics=sem, vmem_limit_bytes=VMEM_LIMIT)


def _dot(a, b):
    return jnp.dot(a, b, preferred_element_type=F32)


def _dot_nt(a, b):
    return lax.dot_general(a, b, (((1,), (1,)), ((), ())), preferred_element_type=F32)


def _dot_tn(a, b):
    return lax.dot_general(a, b, (((0,), (0,)), ((), ())), preferred_element_type=F32)


def _bf16_pieces(x):
    hi = x.astype(BF16)
    rest = x - hi.astype(F32)
    mid = rest.astype(BF16)
    return hi, mid, (rest - mid.astype(F32)).astype(BF16)


def _cumsum_rows(x, visible):
    r = x.shape[0]
    ones = jnp.where(visible, 1.0, 0.0).astype(BF16)
    parts = _dot(jnp.concatenate(_bf16_pieces(x), axis=0), ones)
    return parts[0:r] + parts[r:2 * r] + parts[2 * r:3 * r]


def _cumsum_cols(x, visible):
    c = x.shape[1]
    ones = jnp.where(visible, 1.0, 0.0).astype(BF16)
    parts = _dot(ones, jnp.concatenate(_bf16_pieces(x), axis=1))
    return parts[:, 0:c] + parts[:, c:2 * c] + parts[:, 2 * c:3 * c]


def _sigmoid(x):
    return 1.0 / (1.0 + jnp.exp(-x))


def _log_sigmoid(x):
    return jnp.minimum(x, 0.0) - jnp.log(1.0 + jnp.exp(-jnp.abs(x)))


_NAT_COLS = (0, 1024, 3072)
_TR_COLS = (2048, 4096, 5120)


def _inproj_kernel(x_ref, g_ref, w_ref, wg_ref, cos_ref, sa_ref, sb_ref,
                   nat_ref, tr_ref, gate_ref, gate_t_ref):
    x = x_ref[...]
    ms = jnp.mean(x * x, axis=-1, keepdims=True)
    u = (x * lax.rsqrt(ms + EPS) * g_ref[...]).astype(BF16)
    gates = _dot(u, wg_ref[...])
    gate_ref[...] = gates
    gate_t_ref[...] = gates.T[0:N_GATE_COLS, :]

    def tile(col0):
        return _dot(u, w_ref[0, :, col0:col0 + IN_TN])

    for jj in (NAT_AQ, NAT_AK):
        acc = tile(_NAT_COLS[jj])
        scale = DA_QK_DIM ** -0.5 * LOG2_E if jj == NAT_AQ else 1.0
        cos = cos_ref[...] * scale
        sa = sa_ref[...] * scale
        sb = sb_ref[...] * scale
        for h in range(DA_HEADS):
            t = acc[:, h * LANES:(h + 1) * LANES]
            r = (t * cos + pltpu.roll(t, LANES - DA_ROT_DIM // 2, 1) * sa
                 + pltpu.roll(t, DA_ROT_DIM // 2, 1) * sb)
            nat_ref[:, jj * IN_TN + h * LANES:jj * IN_TN + (h + 1) * LANES] = r.astype(BF16)

    nat_ref[:, NAT_MQK * IN_TN:(NAT_MQK + 1) * IN_TN] = tile(_NAT_COLS[NAT_MQK]).astype(BF16)

    for jj in range(N_TR):
        tr_ref[jj * IN_TN:(jj + 1) * IN_TN, :] = tile(_TR_COLS[jj]).T.astype(BF16)


def _inproj(rows, norm_g, w_main, w_gate, cos_t, sa_t, sb_t, tm):
    m = rows.shape[0]
    tiles_per_seq = cos_t.shape[0] // tm
    tab_spec = pl.BlockSpec((tm, LANES), lambda i: (i % tiles_per_seq, 0))
    return pl.pallas_call(
        _inproj_kernel,
        grid=(m // tm,),
        in_specs=[
            pl.BlockSpec((tm, D_MODEL), lambda i: (i, 0)),
            pl.BlockSpec((1, D_MODEL), lambda i: (0, 0)),
            pl.BlockSpec(w_main.shape, lambda i: (0, 0, 0)),
            pl.BlockSpec((D_MODEL, LANES), lambda i: (0, 0)),
            tab_spec, tab_spec, tab_spec,
        ],
        out_specs=[
            pl.BlockSpec((tm, N_NAT * IN_TN), lambda i: (i, 0)),
            pl.BlockSpec((N_TR * IN_TN, tm), lambda i: (0, i)),
            pl.BlockSpec((tm, LANES), lambda i: (i, 0)),
            pl.BlockSpec((N_GATE_COLS, tm), lambda i: (0, i)),
        ],
        out_shape=[
            jax.ShapeDtypeStruct((m, N_NAT * IN_TN), BF16),
            jax.ShapeDtypeStruct((N_TR * IN_TN, m), BF16),
            jax.ShapeDtypeStruct((m, LANES), F32),
            jax.ShapeDtypeStruct((N_GATE_COLS, m), F32),
        ],
        compiler_params=_cparams(("parallel",)),
        name="inproj",
    )(rows, norm_g, w_main, w_gate, cos_t, sa_t, sb_t)


_CONV_PAD = SUBLANES


def _conv_kernel(x_ref, m_ref, w_ref, b_ref, qt_ref, k_ref, km_ref, s_ref):
    seq = x_ref.shape[0]
    j = pl.program_id(1)
    zeros = jnp.zeros((_CONV_PAD, LANES), F32)
    s_ref[0:_CONV_PAD, :] = zeros
    s_ref[_CONV_PAD:_CONV_PAD + N_META, :] = m_ref[...].astype(F32)
    s_ref[_CONV_PAD + N_META:_CONV_PAD + N_META + seq, :] = x_ref[...].astype(F32)
    s_ref[_CONV_PAD + N_META + seq:, :] = zeros
    w = w_ref[...]
    bias = b_ref[...]

    def conv(pos, n):
        acc = bias
        for t in range(CONV_W):
            start = _CONV_PAD + pos + t - CONV_W // 2
            acc = acc + s_ref[start:start + n, :] * w[t:t + 1, :]
        return acc * (0.5 * jnp.tanh(0.5 * acc) + 0.5)

    @pl.when(j < ML_HEADS)
    def _():
        for c in range(seq // CONV_ROWS):
            y = conv(N_META + c * CONV_ROWS, CONV_ROWS) * (ML_QK_DIM ** -0.5)
            qt_ref[:, c * CONV_ROWS:(c + 1) * CONV_ROWS] = y.T.astype(BF16)

    @pl.when(j >= ML_HEADS)
    def _():
        km_ref[0] = conv(0, N_META).astype(BF16)
        for c in range(seq // CONV_ROWS):
            k_ref[c * CONV_ROWS:(c + 1) * CONV_ROWS, :] = conv(
                N_META + c * CONV_ROWS, CONV_ROWS).astype(BF16)


def _conv(nat_x, nat_m, conv_w, conv_b, bsz, seq):
    col0 = NAT_MQK * (IN_TN // LANES)
    q_j = lambda j: jnp.minimum(j, ML_HEADS - 1)
    k_j = lambda j: jnp.maximum(j - ML_HEADS, 0)
    return pl.pallas_call(
        _conv_kernel,
        grid=(bsz, 2 * ML_HEADS),
        in_specs=[
            pl.BlockSpec((seq, LANES), lambda b, j: (b, col0 + j)),
            pl.BlockSpec((N_META, LANES), lambda b, j: (0, col0 + j)),
            pl.BlockSpec((SUBLANES, LANES), lambda b, j: (0, j)),
            pl.BlockSpec((1, LANES), lambda b, j: (0, j)),
        ],
        out_specs=[
            pl.BlockSpec((ML_QK_DIM, seq), lambda b, j: (q_j(j), b)),
            pl.BlockSpec((seq, ML_QK_DIM), lambda b, j: (b, k_j(j))),
            pl.BlockSpec((1, N_META, ML_QK_DIM), lambda b, j: (b, 0, k_j(j))),
        ],
        out_shape=[
            jax.ShapeDtypeStruct((ML_QK_WIDTH, bsz * seq), BF16),
            jax.ShapeDtypeStruct((bsz * seq, ML_QK_WIDTH), BF16),
            jax.ShapeDtypeStruct((bsz, N_META, ML_QK_WIDTH), BF16),
        ],
        scratch_shapes=[pltpu.VMEM((seq + N_META + 2 * _CONV_PAD, LANES), F32)],
        compiler_params=_cparams(("parallel", "arbitrary")),
        name="mlstm_conv",
    )(nat_x, nat_m, conv_w, conv_b)


_ATT_ACC_ROWS = DA_V_DIM + BF16_SUBLANES


def _attn_kernel(lam_ref, q_ref, k_ref, vt_ref, km_ref, vmt_ref, gain_ref, o_ref,
                 s_ref, sm_ref, acc_ref, o_t_ref, vte_ref, vmte_ref):
    tq = ATT_TQ
    nq = q_ref.shape[0] // tq
    nkb = k_ref.shape[0] // ATT_KB
    lp = lam_ref[...]
    lam = (jnp.exp(jnp.sum(lp[0:1] * lp[1:2], axis=1, keepdims=True))
           - jnp.exp(jnp.sum(lp[2:3] * lp[3:4], axis=1, keepdims=True)) + LAM_INIT)
    lane = lax.broadcasted_iota(jnp.int32, (tq, LANES), 1)
    comps = range(2)

    def q_rows(qt):
        return pl.ds(pl.multiple_of(qt * tq, tq), tq)

    def masked_q(qt):
        q = q_ref[q_rows(qt), :]
        zero = jnp.zeros_like(q)
        return (jnp.where(lane < DA_QK_DIM, q, zero), jnp.where(lane >= DA_QK_DIM, q, zero))

    def fold(a):
        return a.reshape(a.shape[0] // SUBLANES, SUBLANES, tq)

    def scores(qz, kb, slot):
        k = k_ref[kb * ATT_KB:(kb + 1) * ATT_KB, :]
        bmax = []
        for c in comps:
            s = _dot_nt(k, qz[c])
            s_ref[slot, c] = s
            bm = jnp.max(fold(s), axis=0)
            if kb == 0:
                sm = _dot_nt(km_ref[...], qz[c])
                sm_ref[c] = sm
                bm = jnp.maximum(bm, jnp.max(fold(sm), axis=0))
            bmax.append(bm)
        return tuple(bmax)

    def probs(s, m):
        return jnp.exp2(s - m).astype(BF16)

    def pv(kb, slot, c, m_new):
        half = ATT_KB // 2
        out = None
        for part in range(2):
            keys = slice(kb * ATT_KB + part * half, kb * ATT_KB + (part + 1) * half)
            term = _dot(vte_ref[:, keys],
                        probs(s_ref[slot, c, part * half:(part + 1) * half, :], m_new))
            out = term if out is None else out + term
        return out

    def consume(kb, slot, bmax, m):
        m_out = []
        for c in comps:
            bm = jnp.max(bmax[c], axis=0, keepdims=True)
            if kb == 0:
                m_new = bm
                acc_ref[c] = (pv(kb, slot, c, m_new)
                              + _dot(vmte_ref[...], probs(sm_ref[c], m_new)))
            else:
                m_new = jnp.maximum(m[c], bm)
                alpha = jnp.exp2(m[c] - m_new)
                acc_ref[c] = alpha * acc_ref[c] + pv(kb, slot, c, m_new)
            m_out.append(m_new)
        return tuple(m_out)

    def finalize(qt):
        o = o_t_ref[...].T
        y = (o * lax.rsqrt(jnp.mean(o * o, axis=1, keepdims=True) + EPS)
             * gain_ref[...] * (1.0 - LAM_INIT))
        o_ref[q_rows(qt), :] = y.astype(BF16)

    def tile(qt, bmax):
        finalize(jnp.maximum(qt - 1, 0))
        qz = masked_q(qt)
        m = None
        for kb in range(nkb):
            if kb + 1 < nkb:
                bmax_next = scores(qz, kb + 1, (kb + 1) % 2)
            else:
                bmax_next = scores(masked_q(jnp.minimum(qt + 1, nq - 1)), 0, 0)
            m = consume(kb, kb % 2, bmax, m)
            bmax = bmax_next
        l1 = acc_ref[0, DA_V_DIM:DA_V_DIM + 1, :]
        l2 = acc_ref[1, DA_V_DIM:DA_V_DIM + 1, :]
        o_t_ref[...] = (acc_ref[0, 0:DA_V_DIM, :] / l1
                        - lam * (acc_ref[1, 0:DA_V_DIM, :] / l2))
        return bmax

    ones_row = jnp.where(
        lax.broadcasted_iota(jnp.int32, (_ATT_ACC_ROWS - DA_V_DIM, 1), 0) == 0, 1.0, 0.0)
    vte_ref[0:DA_V_DIM, :] = vt_ref[...]
    vte_ref[DA_V_DIM:, :] = jnp.broadcast_to(ones_row, (_ATT_ACC_ROWS - DA_V_DIM, vte_ref.shape[1])
                                             ).astype(BF16)
    vmte_ref[0:DA_V_DIM, :] = vmt_ref[...]
    vmte_ref[DA_V_DIM:, :] = jnp.broadcast_to(ones_row, (_ATT_ACC_ROWS - DA_V_DIM, N_META)
                                              ).astype(BF16)
    o_t_ref[...] = jnp.zeros_like(o_t_ref)
    lax.fori_loop(0, nq, tile, scores(masked_q(0), 0, 0), unroll=ATT_UNROLL)
    finalize(nq - 1)


def _attention(lam_params, nat_x, nat_m, tr_x, tr_m, gain, bsz, seq):
    hb = IN_TN // LANES
    return pl.pallas_call(
        _attn_kernel,
        grid=(bsz, DA_HEADS),
        in_specs=[
            pl.BlockSpec((SUBLANES, LANES), lambda b, h: (0, 0)),
            pl.BlockSpec((seq, LANES), lambda b, h: (b, NAT_AQ * hb + h)),
            pl.BlockSpec((seq, LANES), lambda b, h: (b, NAT_AK * hb + h)),
            pl.BlockSpec((DA_V_DIM, seq), lambda b, h: (TR_AV * DA_HEADS + h, b)),
            pl.BlockSpec((N_META, LANES), lambda b, h: (0, NAT_AK * hb + h)),
            pl.BlockSpec((DA_V_DIM, N_META), lambda b, h: (TR_AV * DA_HEADS + h, 0)),
            pl.BlockSpec((1, LANES), lambda b, h: (0, h)),
        ],
        out_specs=pl.BlockSpec((seq, LANES), lambda b, h: (b, h)),
        out_shape=jax.ShapeDtypeStruct((bsz * seq, DA_WIDTH), BF16),
        scratch_shapes=[
            pltpu.VMEM((2, 2, ATT_KB, ATT_TQ), F32),
            pltpu.VMEM((2, N_META, ATT_TQ), F32),
            pltpu.VMEM((2, _ATT_ACC_ROWS, ATT_TQ), F32),
            pltpu.VMEM((DA_V_DIM, ATT_TQ), F32),
            pltpu.VMEM((_ATT_ACC_ROWS, seq), BF16),
            pltpu.VMEM((_ATT_ACC_ROWS, N_META), BF16),
        ],
        compiler_params=_cparams(("parallel", "parallel")),
        name="diff_attention",
    )(lam_params, nat_x, nat_x, tr_x, nat_m, tr_m, gain)


def _ml_state_update(k, vt, a_wide, b_end, g_row, ct_ref, n_ref, m_ref):
    m_prev = m_ref[0:1, 0:1]
    m_new = jnp.maximum(b_end + m_prev, jnp.max(g_row, axis=1, keepdims=True))
    decay = jnp.exp2(b_end + m_prev - m_new)
    kw = (k.astype(F32) * jnp.exp2(a_wide + (b_end - m_new))).astype(BF16)
    ct_ref[...] = decay * ct_ref[...] + _dot(vt, kw)
    n_ref[...] = decay * n_ref[...] + jnp.sum(kw.astype(F32), axis=0, keepdims=True)
    m_ref[...] = jnp.broadcast_to(m_new, (SUBLANES, LANES))


def _ml_mask(reverse, chunk):
    row = lax.broadcasted_iota(jnp.int32, (chunk, chunk), 0)
    col = lax.broadcasted_iota(jnp.int32, (chunk, chunk), 1)
    return (row >= col) if reverse else (row <= col)


def _ml_gates(reverse, gt_ref, bc_ref, grow_ref, brow_ref, acols_ref):
    chunk = gt_ref.shape[1]
    gates = gt_ref[...] + bc_ref[...]
    grow = gates * LOG2_E
    mask = _ml_mask(reverse, chunk)
    brow_all = _cumsum_rows(_log_sigmoid(gates) * LOG2_E, mask)
    a_rows = grow - pltpu.roll(brow_all, N_GATE_COLS - ML_HEADS, 0)
    a_cols = jnp.concatenate(
        [a_rows, jnp.zeros((LANES - N_GATE_COLS, chunk), F32)], axis=0).T
    grow_ref[...] = grow
    brow_ref[...] = brow_all
    acols_ref[...] = a_cols


def _ml_head(reverse, h, gates, qt_ref, k_ref, vt_ref, ct_ref, n_ref, m_ref, dst_ref):
    grow, brow_all, a_cols = gates
    chunk = k_ref.shape[0]
    mask = _ml_mask(reverse, chunk)
    gate0 = 2 * ML_HEADS if reverse else 0
    end = 0 if reverse else chunk - 1
    gi, gf = gate0 + h, gate0 + ML_HEADS + h
    k = k_ref[:, h * ML_QK_DIM:(h + 1) * ML_QK_DIM]
    qt = qt_ref[h * ML_QK_DIM:(h + 1) * ML_QK_DIM, :]
    vt = vt_ref[h * ML_V_DIM:(h + 1) * ML_V_DIM, :]
    a_wide = jnp.broadcast_to(a_cols[:, gi:gi + 1], (chunk, ML_QK_DIM))
    b_row = brow_all[gf:gf + 1, :]
    li_row = grow[gi:gi + 1, :]
    b_end = b_row[:, end:end + 1]
    m_prev = m_ref[h][0:1, 0:1]

    a_sq = jnp.concatenate([a_wide] * (chunk // ML_QK_DIM), axis=1)
    d = jnp.where(mask, b_row + a_sq, NEG)
    m_inter = b_row + m_prev
    m_t = jnp.maximum(m_inter, jnp.max(d, axis=0, keepdims=True))
    w_inter = jnp.exp2(m_inter - m_t)
    sg = _dot(k, qt) * jnp.exp2(d - m_t)
    num = w_inter * _dot(ct_ref[h].astype(BF16), qt) + _dot(vt, sg.astype(BF16))
    n_rows = jnp.broadcast_to(n_ref[h], (BF16_SUBLANES, ML_QK_DIM)).astype(BF16)
    den = w_inter * _dot(n_rows, qt)[0:1, :] + jnp.sum(sg, axis=0, keepdims=True)
    dst_ref[h * ML_V_DIM:(h + 1) * ML_V_DIM, :] = (
        num / jnp.maximum(jnp.abs(den), jnp.exp2(-m_t)))

    _ml_state_update(k, vt, a_wide, b_end, b_end - b_row + li_row,
                     ct_ref.at[h], n_ref.at[h], m_ref.at[h])


def _mlstm_kernel(qtf_ref, kf_ref, vtf_ref, gtf_ref, opf_ref,
                  qtb_ref, kb_ref, vtb_ref, gtb_ref, opb_ref,
                  gtf_next_ref, gtb_next_ref,
                  bc_ref, km_ref, vmt_ref, gm_ref, gmt_ref, br_ref, gain_ref,
                  out_ref, ct_ref, n_ref, m_ref, hf_ref, hb_ref, grow_ref, brow_ref, acols_ref):
    c = pl.program_id(1)
    nc = pl.num_programs(1)
    half = hf_ref.shape[0] - 1

    @pl.when(c == 0)
    def _():
        ct_ref[...] = jnp.zeros_like(ct_ref)
        n_ref[...] = jnp.zeros_like(n_ref)
        m_ref[...] = jnp.zeros_like(m_ref)
        gates_m = gm_ref[...] + br_ref[...]
        gates_mt = gmt_ref[...] + bc_ref[...]
        gm = gates_m * LOG2_E
        gmt = gates_mt * LOG2_E
        r = lax.broadcasted_iota(jnp.int32, (N_META, N_META), 0)
        s = lax.broadcasted_iota(jnp.int32, (N_META, N_META), 1)
        bcm = _cumsum_cols(_log_sigmoid(gates_m) * LOG2_E, s <= r)
        brm = _cumsum_rows(_log_sigmoid(gates_mt) * LOG2_E, r <= s)
        for h in range(ML_HEADS):
            gi, gf = h, ML_HEADS + h
            b_end = brm[gf:gf + 1, N_META - 1:N_META]
            _ml_state_update(
                km_ref[0][:, h * ML_QK_DIM:(h + 1) * ML_QK_DIM],
                vmt_ref[h * ML_V_DIM:(h + 1) * ML_V_DIM, :],
                jnp.broadcast_to(gm[:, gi:gi + 1] - bcm[:, gf:gf + 1], (N_META, ML_QK_DIM)), b_end,
                b_end - brm[gf:gf + 1, :] + gmt[gi:gi + 1, :],
                ct_ref.at[0, h], n_ref.at[0, h], m_ref.at[0, h])

    slot = jnp.minimum(c, half)
    def prepare(gtf, gtb):
        _ml_gates(False, gtf, bc_ref, grow_ref.at[0], brow_ref.at[0], acols_ref.at[0])
        _ml_gates(True, gtb, bc_ref, grow_ref.at[1], brow_ref.at[1], acols_ref.at[1])

    @pl.when(c == 0)
    def _():
        prepare(gtf_ref, gtb_ref)

    gates_f = (grow_ref[0], brow_ref[0], acols_ref[0])
    gates_b = (grow_ref[1], brow_ref[1], acols_ref[1])
    prepare(gtf_next_ref, gtb_next_ref)
    for h in range(ML_HEADS):
        _ml_head(False, h, gates_f, qtf_ref, kf_ref, vtf_ref,
                 ct_ref.at[0], n_ref.at[0], m_ref.at[0], hf_ref.at[slot])
        _ml_head(True, h, gates_b, qtb_ref, kb_ref, vtb_ref,
                 ct_ref.at[1], n_ref.at[1], m_ref.at[1], hb_ref.at[slot])

    def finish(ht, op_ref, chunk_idx):
        for h in range(ML_HEADS):
            rows = slice(h * ML_V_DIM, (h + 1) * ML_V_DIM)
            hh = ht[rows, :]
            y = hh * lax.rsqrt(jnp.mean(hh * hh, axis=0, keepdims=True) + EPS) * gain_ref[rows, :]
            out_ref[chunk_idx, rows, :] = (_sigmoid(op_ref[rows, :].astype(F32)) * y).astype(BF16)

    @pl.when(c >= nc - half)
    def _():
        partner = nc - 1 - c
        finish(hf_ref[slot] + hb_ref[partner], opf_ref, c)
        finish(hb_ref[slot] + hf_ref[partner], opb_ref, partner)


def _mlstm(q_t, k_x, tr_x, gates_t, bias_col, k_m, tr_m, gates_m, gates_m_t, bias_row, gain_b,
           bsz, seq):
    nc = seq // ML_CHUNK
    half = nc // 2
    fwd = lambda b, c: b * nc + c
    bwd = lambda b, c: b * nc + (nc - 1 - c)

    def chunk_specs(blk):
        return [
            pl.BlockSpec((ML_QK_WIDTH, ML_CHUNK), lambda b, c: (0, blk(b, c))),
            pl.BlockSpec((ML_CHUNK, ML_QK_WIDTH), lambda b, c: (blk(b, c), 0)),
            pl.BlockSpec((ML_WIDTH, ML_CHUNK), lambda b, c: (TR_MV, blk(b, c))),
            pl.BlockSpec((N_GATE_COLS, ML_CHUNK), lambda b, c: (0, blk(b, c))),
            pl.BlockSpec((ML_WIDTH, ML_CHUNK), lambda b, c: (TR_MO, blk(b, c))),
        ]

    chunk_args = [q_t, k_x, tr_x, gates_t, tr_x]
    nxt = lambda c: jnp.minimum(c + 1, nc - 1)
    in_specs = chunk_specs(fwd) + chunk_specs(bwd) + [
        pl.BlockSpec((N_GATE_COLS, ML_CHUNK), lambda b, c: (0, fwd(b, nxt(c)))),
        pl.BlockSpec((N_GATE_COLS, ML_CHUNK), lambda b, c: (0, bwd(b, nxt(c)))),
        pl.BlockSpec((N_GATE_COLS, 1), lambda b, c: (0, 0)),
        pl.BlockSpec((1, N_META, ML_QK_WIDTH), lambda b, c: (b, 0, 0)),
        pl.BlockSpec((ML_WIDTH, N_META), lambda b, c: (TR_MV, 0)),
        pl.BlockSpec((N_META, LANES), lambda b, c: (0, 0)),
        pl.BlockSpec((N_GATE_COLS, N_META), lambda b, c: (0, 0)),
        pl.BlockSpec((1, LANES), lambda b, c: (0, 0)),
        pl.BlockSpec((ML_WIDTH, ML_CHUNK), lambda b, c: (0, 0)),
    ]
    return pl.pallas_call(
        _mlstm_kernel,
        grid=(bsz, nc),
        in_specs=in_specs,
        out_specs=pl.BlockSpec((nc, ML_WIDTH, ML_CHUNK), lambda b, c: (b, 0, 0)),
        out_shape=jax.ShapeDtypeStruct((bsz * nc, ML_WIDTH, ML_CHUNK), BF16),
        scratch_shapes=[
            pltpu.VMEM((2, ML_HEADS, ML_V_DIM, ML_QK_DIM), F32),
            pltpu.VMEM((2, ML_HEADS, 1, ML_QK_DIM), F32),
            pltpu.VMEM((2, ML_HEADS, SUBLANES, LANES), F32),
            pltpu.VMEM((half + 1, ML_WIDTH, ML_CHUNK), F32),
            pltpu.VMEM((half + 1, ML_WIDTH, ML_CHUNK), F32),
            pltpu.VMEM((2, N_GATE_COLS, ML_CHUNK), F32),
            pltpu.VMEM((2, N_GATE_COLS, ML_CHUNK), F32),
            pltpu.VMEM((2, ML_CHUNK, LANES), F32),
        ],
        compiler_params=_cparams(("parallel", "arbitrary")),
        name="mlstm",
    )(*chunk_args, *chunk_args, gates_t, gates_t,
      bias_col, k_m, tr_m, gates_m, gates_m_t, bias_row, gain_b)


_FFN_TILES = D_FF // FFN_TF


def _mix_ffn_kernel(a_ref, mt_ref, x_ref, wo_ref, g2_ref, gf_ref, wg_hbm, wu_hbm, wd_hbm,
                    o_ref, u_ref, wg_buf, wu_buf, wd_buf, sem):
    def weight_copies(j, slot):
        cols = pl.ds(pl.multiple_of(j * FFN_TF, FFN_TF), FFN_TF)
        return (pltpu.make_async_copy(wg_hbm.at[:, cols], wg_buf.at[slot], sem.at[0, slot]),
                pltpu.make_async_copy(wu_hbm.at[:, cols], wu_buf.at[slot], sem.at[1, slot]),
                pltpu.make_async_copy(wd_hbm.at[cols, :], wd_buf.at[slot], sem.at[2, slot]))

    for cp in weight_copies(0, 0):
        cp.start()

    w_m = wo_ref[DA_WIDTH:DA_WIDTH + ML_WIDTH, :]
    ml = jnp.concatenate([_dot_tn(mt_ref[s], w_m) for s in range(mt_ref.shape[0])], axis=0)
    h = _dot(a_ref[...], wo_ref[0:DA_WIDTH, :]) + ml + x_ref[...]
    o_ref[...] = h
    u = h * lax.rsqrt(jnp.mean(h * h, axis=-1, keepdims=True) + EPS) * g2_ref[...]
    u_ref[...] = u.astype(BF16)

    def step(j, slot, prefetch):
        if prefetch:
            for cp in weight_copies(j + 1, 1 - slot):
                cp.start()
        for cp in weight_copies(j, slot):
            cp.wait()
        u = u_ref[...]
        g = _dot(u, wg_buf[slot])
        ff = (g * _sigmoid(g) * _dot(u, wu_buf[slot])).astype(BF16)
        o_ref[...] += _dot(ff, wd_buf[slot])

    def pair(p, carry):
        step(2 * p, 0, True)
        step(2 * p + 1, 1, True)
        return carry

    lax.fori_loop(0, (_FFN_TILES - 1) // 2, pair, 0)
    step(_FFN_TILES - 1, 0, False)

    y = o_ref[...]
    o_ref[...] = y * lax.rsqrt(jnp.mean(y * y, axis=-1, keepdims=True) + EPS) * gf_ref[...]


def _mix_ffn(attn, ml_t, x_rows, w_out, norm_ffn_g, w_gate, w_up, w_down, norm_final_g):
    assert _FFN_TILES % 2 == 1
    m = x_rows.shape[0]
    return pl.pallas_call(
        _mix_ffn_kernel,
        grid=(m // FFN_TM,),
        in_specs=[
            pl.BlockSpec((FFN_TM, DA_WIDTH), lambda i: (i, 0)),
            pl.BlockSpec((FFN_TM // ML_CHUNK, ML_WIDTH, ML_CHUNK), lambda i: (i, 0, 0)),
            pl.BlockSpec((FFN_TM, D_MODEL), lambda i: (i, 0)),
            pl.BlockSpec((DA_WIDTH + ML_WIDTH, D_MODEL), lambda i: (0, 0)),
            pl.BlockSpec((1, D_MODEL), lambda i: (0, 0)),
            pl.BlockSpec((1, D_MODEL), lambda i: (0, 0)),
            pl.BlockSpec(memory_space=pl.ANY),
            pl.BlockSpec(memory_space=pl.ANY),
            pl.BlockSpec(memory_space=pl.ANY),
        ],
        out_specs=pl.BlockSpec((FFN_TM, D_MODEL), lambda i: (i, 0)),
        out_shape=jax.ShapeDtypeStruct((m, D_MODEL), F32),
        scratch_shapes=[
            pltpu.VMEM((FFN_TM, D_MODEL), BF16),
            pltpu.VMEM((2, D_MODEL, FFN_TF), BF16),
            pltpu.VMEM((2, D_MODEL, FFN_TF), BF16),
            pltpu.VMEM((2, FFN_TF, D_MODEL), BF16),
            pltpu.SemaphoreType.DMA((3, 2)),
        ],
        compiler_params=_cparams(("parallel",)),
        name="mix_ffn",
    )(attn, ml_t, x_rows, w_out, norm_ffn_g, norm_final_g, w_gate, w_up, w_down)


def _rope_tables(pos0, n):
    lane = jnp.arange(LANES) % DA_QK_DIM
    half = DA_ROT_DIM // 2
    inv = ROPE_THETA ** (-jnp.arange(0, DA_ROT_DIM, 2, dtype=F32) / DA_ROT_DIM)
    inv_lane = jnp.where(lane < DA_ROT_DIM, inv[lane % half], 0.0)
    pos = jnp.arange(pos0, pos0 + n, dtype=F32)
    ang = pos[:, None] * inv_lane[None, :]
    sin = jnp.sin(ang)
    sa = jnp.where(lane < half, -sin, 0.0)
    sb = jnp.where((lane >= half) & (lane < DA_ROT_DIM), sin, 0.0)
    return jnp.cos(ang), sa, sb


def _pad_lanes(a, width=LANES):
    return jnp.pad(a, ((0, 0), (0, width - a.shape[1])))


def kernel(x, meta_tokens, norm_mix, w_in, da_lambda_q1, da_lambda_k1, da_lambda_q2, da_lambda_k2,
           da_head_norm, ml_conv_w, ml_conv_b, ml_gate_bias, ml_head_norm, w_out, norm_ffn,
           w_gate, w_up, w_down, norm_final):
    bsz, seq, _ = x.shape
    x_rows = x.reshape(bsz * seq, D_MODEL)

    main_cols = N_NAT * IN_TN + N_TR * IN_TN
    w_main = w_in.astype(BF16)
    w_gates = _pad_lanes(w_in[0][:, main_cols:]).astype(BF16)
    norm_mix0 = norm_mix[0][None, :]

    tabs_x = _rope_tables(N_META, seq)
    tabs_m = _rope_tables(0, N_META)
    nat_x, tr_x, _, gates_t = _inproj(x_rows, norm_mix0, w_main, w_gates, *tabs_x, tm=IN_TM)
    nat_m, tr_m, gates_m, gates_m_t = _inproj(meta_tokens, norm_mix0, w_main, w_gates, *tabs_m,
                                              tm=N_META)

    conv_w = jnp.pad(ml_conv_w[0], ((0, SUBLANES - CONV_W), (0, 0)))
    q_t, k_x, k_m = _conv(nat_x, nat_m, conv_w, ml_conv_b[0][None, :], bsz, seq)

    lam_params = jnp.pad(
        jnp.stack([da_lambda_q1[0], da_lambda_k1[0], da_lambda_q2[0], da_lambda_k2[0]]),
        ((0, SUBLANES - 4), (0, LANES - DA_QK_DIM)))
    attn = _attention(lam_params, nat_x, nat_m, tr_x, tr_m,
                      da_head_norm[0].reshape(1, DA_WIDTH), bsz, seq)

    bias = ml_gate_bias[0].reshape(N_GATE_COLS)
    bias_row = _pad_lanes(bias[None, :])
    bias_col = bias[:, None]
    gain_b = jnp.broadcast_to(ml_head_norm[0].reshape(ML_WIDTH, 1), (ML_WIDTH, ML_CHUNK))
    ml_t = _mlstm(q_t, k_x, tr_x, gates_t, bias_col, k_m, tr_m, gates_m, gates_m_t, bias_row,
                  gain_b, bsz, seq)

    out = _mix_ffn(attn, ml_t, x_rows, w_out[0].astype(BF16), norm_ffn[0][None, :],
                   w_gate[0].astype(BF16), w_up[0].astype(BF16), w_down[0].astype(BF16),
                   norm_final[None, :])
    return out.reshape(bsz, seq, D_MODEL)
```

```python
import jax
import jax.numpy as jnp
from jax import lax
from jax.experimental import pallas as pl
from jax.experimental.pallas import tpu as pltpu

F32 = jnp.float32
BF16 = jnp.bfloat16

D_MODEL = 2048
N_META = 16
EPS = 1e-6
ROPE_THETA = 500000.0
NEG = -1e30

DA_QK_DIM = 64
DA_V_DIM = 128
DA_HEADS = 8
DA_ROT_DIM = 16
DA_WIDTH = DA_HEADS * DA_V_DIM
LAM_INIT = 0.8 - 0.6 * 1.0
LOG2_E = 1.4426950408889634

ML_HEADS = 4
ML_V_DIM = 256
ML_QK_DIM = 128
ML_WIDTH = ML_HEADS * ML_V_DIM
ML_QK_WIDTH = ML_HEADS * ML_QK_DIM
CONV_W = 5
N_GATES = 4
N_GATE_COLS = N_GATES * ML_HEADS

D_FF = 5632

IN_TN = 1024
NAT_AQ, NAT_AK, NAT_MQK = 0, 1, 2
TR_AV, TR_MV, TR_MO = 0, 1, 2
N_NAT = 3
N_TR = 3

LANES = 128
SUBLANES = 8
BF16_SUBLANES = 16
VMEM_LIMIT = 56 * 1024 * 1024

IN_TM = 512
ATT_TQ = 256
ATT_KB = 512
ATT_UNROLL = 8
ML_CHUNK = 256
FFN_TM = 512
FFN_TF = 512
CONV_ROWS = 512


def _cparams(sem):
    return pltpu.CompilerParams(dimension_semantics=sem, vmem_limit_bytes=VMEM_LIMIT)


def _dot(a, b):
    return jnp.dot(a, b, preferred_element_type=F32)


def _dot_nt(a, b):
    return lax.dot_general(a, b, (((1,), (1,)), ((), ())), preferred_element_type=F32)


def _dot_tn(a, b):
    return lax.dot_general(a, b, (((0,), (0,)), ((), ())), preferred_element_type=F32)


def _bf16_pieces(x):
    hi = x.astype(BF16)
    rest = x - hi.astype(F32)
    mid = rest.astype(BF16)
    return hi, mid, (rest - mid.astype(F32)).astype(BF16)


def _cumsum_rows(x, visible):
    r = x.shape[0]
    ones = jnp.where(visible, 1.0, 0.0).astype(BF16)
    parts = _dot(jnp.concatenate(_bf16_pieces(x), axis=0), ones)
    return parts[0:r] + parts[r:2 * r] + parts[2 * r:3 * r]


def _cumsum_cols(x, visible):
    c = x.shape[1]
    ones = jnp.where(visible, 1.0, 0.0).astype(BF16)
    parts = _dot(ones, jnp.concatenate(_bf16_pieces(x), axis=1))
    return parts[:, 0:c] + parts[:, c:2 * c] + parts[:, 2 * c:3 * c]


def _sigmoid(x):
    return 1.0 / (1.0 + jnp.exp(-x))


def _log_sigmoid(x):
    return jnp.minimum(x, 0.0) - jnp.log(1.0 + jnp.exp(-jnp.abs(x)))


_NAT_COLS = (0, 1024, 3072)
_TR_COLS = (2048, 4096, 5120)


def _inproj_kernel(x_ref, g_ref, w_ref, wg_ref, cos_ref, sa_ref, sb_ref,
                   nat_ref, tr_ref, gate_ref, gate_t_ref):
    x = x_ref[...]
    ms = jnp.mean(x * x, axis=-1, keepdims=True)
    u = (x * lax.rsqrt(ms + EPS) * g_ref[...]).astype(BF16)
    gates = _dot(u, wg_ref[...])
    gate_ref[...] = gates
    gate_t_ref[...] = gates.T[0:N_GATE_COLS, :]

    def tile(col0):
        return _dot(u, w_ref[0, :, col0:col0 + IN_TN])

    for jj in (NAT_AQ, NAT_AK):
        acc = tile(_NAT_COLS[jj])
        scale = DA_QK_DIM ** -0.5 * LOG2_E if jj == NAT_AQ else 1.0
        cos = cos_ref[...] * scale
        sa = sa_ref[...] * scale
        sb = sb_ref[...] * scale
        for h in range(DA_HEADS):
            t = acc[:, h * LANES:(h + 1) * LANES]
            r = (t * cos + pltpu.roll(t, LANES - DA_ROT_DIM // 2, 1) * sa
                 + pltpu.roll(t, DA_ROT_DIM // 2, 1) * sb)
            nat_ref[:, jj * IN_TN + h * LANES:jj * IN_TN + (h + 1) * LANES] = r.astype(BF16)

    nat_ref[:, NAT_MQK * IN_TN:(NAT_MQK + 1) * IN_TN] = tile(_NAT_COLS[NAT_MQK]).astype(BF16)

    for jj in range(N_TR):
        tr_ref[jj * IN_TN:(jj + 1) * IN_TN, :] = tile(_TR_COLS[jj]).T.astype(BF16)


def _inproj(rows, norm_g, w_main, w_gate, cos_t, sa_t, sb_t, tm):
    m = rows.shape[0]
    tiles_per_seq = cos_t.shape[0] // tm
    tab_spec = pl.BlockSpec((tm, LANES), lambda i: (i % tiles_per_seq, 0))
    return pl.pallas_call(
        _inproj_kernel,
        grid=(m // tm,),
        in_specs=[
            pl.BlockSpec((tm, D_MODEL), lambda i: (i, 0)),
            pl.BlockSpec((1, D_MODEL), lambda i: (0, 0)),
            pl.BlockSpec(w_main.shape, lambda i: (0, 0, 0)),
            pl.BlockSpec((D_MODEL, LANES), lambda i: (0, 0)),
            tab_spec, tab_spec, tab_spec,
        ],
        out_specs=[
            pl.BlockSpec((tm, N_NAT * IN_TN), lambda i: (i, 0)),
            pl.BlockSpec((N_TR * IN_TN, tm), lambda i: (0, i)),
            pl.BlockSpec((tm, LANES), lambda i: (i, 0)),
            pl.BlockSpec((N_GATE_COLS, tm), lambda i: (0, i)),
        ],
        out_shape=[
            jax.ShapeDtypeStruct((m, N_NAT * IN_TN), BF16),
            jax.ShapeDtypeStruct((N_TR * IN_TN, m), BF16),
            jax.ShapeDtypeStruct((m, LANES), F32),
            jax.ShapeDtypeStruct((N_GATE_COLS, m), F32),
        ],
        compiler_params=_cparams(("parallel",)),
        name="inproj",
    )(rows, norm_g, w_main, w_gate, cos_t, sa_t, sb_t)


_CONV_PAD = SUBLANES


def _conv_kernel(x_ref, m_ref, w_ref, b_ref, qt_ref, k_ref, km_ref, s_ref):
    seq = x_ref.shape[0]
    j = pl.program_id(1)
    zeros = jnp.zeros((_CONV_PAD, LANES), F32)
    s_ref[0:_CONV_PAD, :] = zeros
    s_ref[_CONV_PAD:_CONV_PAD + N_META, :] = m_ref[...].astype(F32)
    s_ref[_CONV_PAD + N_META:_CONV_PAD + N_META + seq, :] = x_ref[...].astype(F32)
    s_ref[_CONV_PAD + N_META + seq:, :] = zeros
    w = w_ref[...]
    bias = b_ref[...]

    def conv(pos, n):
        acc = bias
        for t in range(CONV_W):
            start = _CONV_PAD + pos + t - CONV_W // 2
            acc = acc + s_ref[start:start + n, :] * w[t:t + 1, :]
        return acc * (0.5 * jnp.tanh(0.5 * acc) + 0.5)

    @pl.when(j < ML_HEADS)
    def _():
        for c in range(seq // CONV_ROWS):
            y = conv(N_META + c * CONV_ROWS, CONV_ROWS) * (ML_QK_DIM ** -0.5)
            qt_ref[:, c * CONV_ROWS:(c + 1) * CONV_ROWS] = y.T.astype(BF16)

    @pl.when(j >= ML_HEADS)
    def _():
        km_ref[0] = conv(0, N_META).astype(BF16)
        for c in range(seq // CONV_ROWS):
            k_ref[c * CONV_ROWS:(c + 1) * CONV_ROWS, :] = conv(
                N_META + c * CONV_ROWS, CONV_ROWS).astype(BF16)


def _conv(nat_x, nat_m, conv_w, conv_b, bsz, seq):
    col0 = NAT_MQK * (IN_TN // LANES)
    q_j = lambda j: jnp.minimum(j, ML_HEADS - 1)
    k_j = lambda j: jnp.maximum(j - ML_HEADS, 0)
    return pl.pallas_call(
        _conv_kernel,
        grid=(bsz, 2 * ML_HEADS),
        in_specs=[
            pl.BlockSpec((seq, LANES), lambda b, j: (b, col0 + j)),
            pl.BlockSpec((N_META, LANES), lambda b, j: (0, col0 + j)),
            pl.BlockSpec((SUBLANES, LANES), lambda b, j: (0, j)),
            pl.BlockSpec((1, LANES), lambda b, j: (0, j)),
        ],
        out_specs=[
            pl.BlockSpec((ML_QK_DIM, seq), lambda b, j: (q_j(j), b)),
            pl.BlockSpec((seq, ML_QK_DIM), lambda b, j: (b, k_j(j))),
            pl.BlockSpec((1, N_META, ML_QK_DIM), lambda b, j: (b, 0, k_j(j))),
        ],
        out_shape=[
            jax.ShapeDtypeStruct((ML_QK_WIDTH, bsz * seq), BF16),
            jax.ShapeDtypeStruct((bsz * seq, ML_QK_WIDTH), BF16),
            jax.ShapeDtypeStruct((bsz, N_META, ML_QK_WIDTH), BF16),
        ],
        scratch_shapes=[pltpu.VMEM((seq + N_META + 2 * _CONV_PAD, LANES), F32)],
        compiler_params=_cparams(("parallel", "arbitrary")),
        name="mlstm_conv",
    )(nat_x, nat_m, conv_w, conv_b)


_ATT_ACC_ROWS = DA_V_DIM + BF16_SUBLANES


def _attn_kernel(lam_ref, q_ref, k_ref, vt_ref, km_ref, vmt_ref, gain_ref, o_ref,
                 s_ref, sm_ref, acc_ref, o_t_ref, vte_ref, vmte_ref):
    tq = ATT_TQ
    nq = q_ref.shape[0] // tq
    nkb = k_ref.shape[0] // ATT_KB
    lp = lam_ref[...]
    lam = (jnp.exp(jnp.sum(lp[0:1] * lp[1:2], axis=1, keepdims=True))
           - jnp.exp(jnp.sum(lp[2:3] * lp[3:4], axis=1, keepdims=True)) + LAM_INIT)
    lane = lax.broadcasted_iota(jnp.int32, (tq, LANES), 1)
    comps = range(2)

    def q_rows(qt):
        return pl.ds(pl.multiple_of(qt * tq, tq), tq)

    def masked_q(qt):
        q = q_ref[q_rows(qt), :]
        zero = jnp.zeros_like(q)
        return (jnp.where(lane < DA_QK_DIM, q, zero), jnp.where(lane >= DA_QK_DIM, q, zero))

    def fold(a):
        return a.reshape(a.shape[0] // SUBLANES, SUBLANES, tq)

    def scores(qz, kb, slot):
        k = k_ref[kb * ATT_KB:(kb + 1) * ATT_KB, :]
        bmax = []
        for c in comps:
            s = _dot_nt(k, qz[c])
            s_ref[slot, c] = s
            bm = jnp.max(fold(s), axis=0)
            if kb == 0:
                sm = _dot_nt(km_ref[...], qz[c])
                sm_ref[c] = sm
                bm = jnp.maximum(bm, jnp.max(fold(sm), axis=0))
            bmax.append(bm)
        return tuple(bmax)

    def probs(s, m):
        return jnp.exp2(s - m).astype(BF16)

    def pv(kb, slot, c, m_new):
        half = ATT_KB // 2
        out = None
        for part in range(2):
            keys = slice(kb * ATT_KB + part * half, kb * ATT_KB + (part + 1) * half)
            term = _dot(vte_ref[:, keys],
                        probs(s_ref[slot, c, part * half:(part + 1) * half, :], m_new))
            out = term if out is None else out + term
        return out

    def consume(kb, slot, bmax, m):
        m_out = []
        for c in comps:
            bm = jnp.max(bmax[c], axis=0, keepdims=True)
            if kb == 0:
                m_new = bm
                acc_ref[c] = (pv(kb, slot, c, m_new)
                              + _dot(vmte_ref[...], probs(sm_ref[c], m_new)))
            else:
                m_new = jnp.maximum(m[c], bm)
                alpha = jnp.exp2(m[c] - m_new)
                acc_ref[c] = alpha * acc_ref[c] + pv(kb, slot, c, m_new)
            m_out.append(m_new)
        return tuple(m_out)

    def finalize(qt):
        o = o_t_ref[...].T
        y = (o * lax.rsqrt(jnp.mean(o * o, axis=1, keepdims=True) + EPS)
             * gain_ref[...] * (1.0 - LAM_INIT))
        o_ref[q_rows(qt), :] = y.astype(BF16)

    def tile(qt, bmax):
        finalize(jnp.maximum(qt - 1, 0))
        qz = masked_q(qt)
        m = None
        for kb in range(nkb):
            if kb + 1 < nkb:
                bmax_next = scores(qz, kb + 1, (kb + 1) % 2)
            else:
                bmax_next = scores(masked_q(jnp.minimum(qt + 1, nq - 1)), 0, 0)
            m = consume(kb, kb % 2, bmax, m)
            bmax = bmax_next
        l1 = acc_ref[0, DA_V_DIM:DA_V_DIM + 1, :]
        l2 = acc_ref[1, DA_V_DIM:DA_V_DIM + 1, :]
        o_t_ref[...] = (acc_ref[0, 0:DA_V_DIM, :] / l1
                        - lam * (acc_ref[1, 0:DA_V_DIM, :] / l2))
        return bmax

    ones_row = jnp.where(
        lax.broadcasted_iota(jnp.int32, (_ATT_ACC_ROWS - DA_V_DIM, 1), 0) == 0, 1.0, 0.0)
    vte_ref[0:DA_V_DIM, :] = vt_ref[...]
    vte_ref[DA_V_DIM:, :] = jnp.broadcast_to(ones_row, (_ATT_ACC_ROWS - DA_V_DIM, vte_ref.shape[1])
                                             ).astype(BF16)
    vmte_ref[0:DA_V_DIM, :] = vmt_ref[...]
    vmte_ref[DA_V_DIM:, :] = jnp.broadcast_to(ones_row, (_ATT_ACC_ROWS - DA_V_DIM, N_META)
                                              ).astype(BF16)
    o_t_ref[...] = jnp.zeros_like(o_t_ref)
    lax.fori_loop(0, nq, tile, scores(masked_q(0), 0, 0), unroll=ATT_UNROLL)
    finalize(nq - 1)


def _attention(lam_params, nat_x, nat_m, tr_x, tr_m, gain, bsz, seq):
    hb = IN_TN // LANES
    return pl.pallas_call(
        _attn_kernel,
        grid=(bsz, DA_HEADS),
        in_specs=[
            pl.BlockSpec((SUBLANES, LANES), lambda b, h: (0, 0)),
            pl.BlockSpec((seq, LANES), lambda b, h: (b, NAT_AQ * hb + h)),
            pl.BlockSpec((seq, LANES), lambda b, h: (b, NAT_AK * hb + h)),
            pl.BlockSpec((DA_V_DIM, seq), lambda b, h: (TR_AV * DA_HEADS + h, b)),
            pl.BlockSpec((N_META, LANES), lambda b, h: (0, NAT_AK * hb + h)),
            pl.BlockSpec((DA_V_DIM, N_META), lambda b, h: (TR_AV * DA_HEADS + h, 0)),
            pl.BlockSpec((1, LANES), lambda b, h: (0, h)),
        ],
        out_specs=pl.BlockSpec((seq, LANES), lambda b, h: (b, h)),
        out_shape=jax.ShapeDtypeStruct((bsz * seq, DA_WIDTH), BF16),
        scratch_shapes=[
            pltpu.VMEM((2, 2, ATT_KB, ATT_TQ), F32),
            pltpu.VMEM((2, N_META, ATT_TQ), F32),
            pltpu.VMEM((2, _ATT_ACC_ROWS, ATT_TQ), F32),
            pltpu.VMEM((DA_V_DIM, ATT_TQ), F32),
            pltpu.VMEM((_ATT_ACC_ROWS, seq), BF16),
            pltpu.VMEM((_ATT_ACC_ROWS, N_META), BF16),
        ],
        compiler_params=_cparams(("parallel", "parallel")),
        name="diff_attention",
    )(lam_params, nat_x, nat_x, tr_x, nat_m, tr_m, gain)


def _ml_state_update(k, vt, a_wide, b_end, g_row, ct_ref, n_ref, m_ref):
    m_prev = m_ref[0:1, 0:1]
    m_new = jnp.maximum(b_end + m_prev, jnp.max(g_row, axis=1, keepdims=True))
    decay = jnp.exp2(b_end + m_prev - m_new)
    kw = (k.astype(F32) * jnp.exp2(a_wide + (b_end - m_new))).astype(BF16)
    ct_ref[...] = decay * ct_ref[...] + _dot(vt, kw)
    n_ref[...] = decay * n_ref[...] + jnp.sum(kw.astype(F32), axis=0, keepdims=True)
    m_ref[...] = jnp.broadcast_to(m_new, (SUBLANES, LANES))


def _ml_mask(reverse, chunk):
    row = lax.broadcasted_iota(jnp.int32, (chunk, chunk), 0)
    col = lax.broadcasted_iota(jnp.int32, (chunk, chunk), 1)
    return (row >= col) if reverse else (row <= col)


def _ml_gates(reverse, gt_ref, bc_ref, grow_ref, brow_ref, acols_ref):
    chunk = gt_ref.shape[1]
    gates = gt_ref[...] + bc_ref[...]
    grow = gates * LOG2_E
    mask = _ml_mask(reverse, chunk)
    brow_all = _cumsum_rows(_log_sigmoid(gates) * LOG2_E, mask)
    a_rows = grow - pltpu.roll(brow_all, N_GATE_COLS - ML_HEADS, 0)
    a_cols = jnp.concatenate(
        [a_rows, jnp.zeros((LANES - N_GATE_COLS, chunk), F32)], axis=0).T
    grow_ref[...] = grow
    brow_ref[...] = brow_all
    acols_ref[...] = a_cols


def _ml_head(reverse, h, gates, qt_ref, k_ref, vt_ref, ct_ref, n_ref, m_ref, dst_ref):
    grow, brow_all, a_cols = gates
    chunk = k_ref.shape[0]
    mask = _ml_mask(reverse, chunk)
    gate0 = 2 * ML_HEADS if reverse else 0
    end = 0 if reverse else chunk - 1
    gi, gf = gate0 + h, gate0 + ML_HEADS + h
    k = k_ref[:, h * ML_QK_DIM:(h + 1) * ML_QK_DIM]
    qt = qt_ref[h * ML_QK_DIM:(h + 1) * ML_QK_DIM, :]
    vt = vt_ref[h * ML_V_DIM:(h + 1) * ML_V_DIM, :]
    a_wide = jnp.broadcast_to(a_cols[:, gi:gi + 1], (chunk, ML_QK_DIM))
    b_row = brow_all[gf:gf + 1, :]
    li_row = grow[gi:gi + 1, :]
    b_end = b_row[:, end:end + 1]
    m_prev = m_ref[h][0:1, 0:1]

    a_sq = jnp.concatenate([a_wide] * (chunk // ML_QK_DIM), axis=1)
    d = jnp.where(mask, b_row + a_sq, NEG)
    m_inter = b_row + m_prev
    m_t = jnp.maximum(m_inter, jnp.max(d, axis=0, keepdims=True))
    w_inter = jnp.exp2(m_inter - m_t)
    sg = _dot(k, qt) * jnp.exp2(d - m_t)
    num = w_inter * _dot(ct_ref[h].astype(BF16), qt) + _dot(vt, sg.astype(BF16))
    n_rows = jnp.broadcast_to(n_ref[h], (BF16_SUBLANES, ML_QK_DIM)).astype(BF16)
    den = w_inter * _dot(n_rows, qt)[0:1, :] + jnp.sum(sg, axis=0, keepdims=True)
    dst_ref[h * ML_V_DIM:(h + 1) * ML_V_DIM, :] = (
        num / jnp.maximum(jnp.abs(den), jnp.exp2(-m_t)))

    _ml_state_update(k, vt, a_wide, b_end, b_end - b_row + li_row,
                     ct_ref.at[h], n_ref.at[h], m_ref.at[h])


def _mlstm_kernel(qtf_ref, kf_ref, vtf_ref, gtf_ref, opf_ref,
                  qtb_ref, kb_ref, vtb_ref, gtb_ref, opb_ref,
                  gtf_next_ref, gtb_next_ref,
                  bc_ref, km_ref, vmt_ref, gm_ref, gmt_ref, br_ref, gain_ref,
                  out_ref, ct_ref, n_ref, m_ref, hf_ref, hb_ref, grow_ref, brow_ref, acols_ref):
    c = pl.program_id(1)
    nc = pl.num_programs(1)
    half = hf_ref.shape[0] - 1

    @pl.when(c == 0)
    def _():
        ct_ref[...] = jnp.zeros_like(ct_ref)
        n_ref[...] = jnp.zeros_like(n_ref)
        m_ref[...] = jnp.zeros_like(m_ref)
        gates_m = gm_ref[...] + br_ref[...]
        gates_mt = gmt_ref[...] + bc_ref[...]
        gm = gates_m * LOG2_E
        gmt = gates_mt * LOG2_E
        r = lax.broadcasted_iota(jnp.int32, (N_META, N_META), 0)
        s = lax.broadcasted_iota(jnp.int32, (N_META, N_META), 1)
        bcm = _cumsum_cols(_log_sigmoid(gates_m) * LOG2_E, s <= r)
        brm = _cumsum_rows(_log_sigmoid(gates_mt) * LOG2_E, r <= s)
        for h in range(ML_HEADS):
            gi, gf = h, ML_HEADS + h
            b_end = brm[gf:gf + 1, N_META - 1:N_META]
            _ml_state_update(
                km_ref[0][:, h * ML_QK_DIM:(h + 1) * ML_QK_DIM],
                vmt_ref[h * ML_V_DIM:(h + 1) * ML_V_DIM, :],
                jnp.broadcast_to(gm[:, gi:gi + 1] - bcm[:, gf:gf + 1], (N_META, ML_QK_DIM)), b_end,
                b_end - brm[gf:gf + 1, :] + gmt[gi:gi + 1, :],
                ct_ref.at[0, h], n_ref.at[0, h], m_ref.at[0, h])

    slot = jnp.minimum(c, half)
    def prepare(gtf, gtb):
        _ml_gates(False, gtf, bc_ref, grow_ref.at[0], brow_ref.at[0], acols_ref.at[0])
        _ml_gates(True, gtb, bc_ref, grow_ref.at[1], brow_ref.at[1], acols_ref.at[1])

    @pl.when(c == 0)
    def _():
        prepare(gtf_ref, gtb_ref)

    gates_f = (grow_ref[0], brow_ref[0], acols_ref[0])
    gates_b = (grow_ref[1], brow_ref[1], acols_ref[1])
    prepare(gtf_next_ref, gtb_next_ref)
    for h in range(ML_HEADS):
        _ml_head(False, h, gates_f, qtf_ref, kf_ref, vtf_ref,
                 ct_ref.at[0], n_ref.at[0], m_ref.at[0], hf_ref.at[slot])
        _ml_head(True, h, gates_b, qtb_ref, kb_ref, vtb_ref,
                 ct_ref.at[1], n_ref.at[1], m_ref.at[1], hb_ref.at[slot])

    def finish(ht, op_ref, chunk_idx):
        for h in range(ML_HEADS):
            rows = slice(h * ML_V_DIM, (h + 1) * ML_V_DIM)
            hh = ht[rows, :]
            y = hh * lax.rsqrt(jnp.mean(hh * hh, axis=0, keepdims=True) + EPS) * gain_ref[rows, :]
            out_ref[chunk_idx, rows, :] = (_sigmoid(op_ref[rows, :].astype(F32)) * y).astype(BF16)

    @pl.when(c >= nc - half)
    def _():
        partner = nc - 1 - c
        finish(hf_ref[slot] + hb_ref[partner], opf_ref, c)
        finish(hb_ref[slot] + hf_ref[partner], opb_ref, partner)


def _mlstm(q_t, k_x, tr_x, gates_t, bias_col, k_m, tr_m, gates_m, gates_m_t, bias_row, gain_b,
           bsz, seq):
    nc = seq // ML_CHUNK
    half = nc // 2
    fwd = lambda b, c: b * nc + c
    bwd = lambda b, c: b * nc + (nc - 1 - c)

    def chunk_specs(blk):
        return [
            pl.BlockSpec((ML_QK_WIDTH, ML_CHUNK), lambda b, c: (0, blk(b, c))),
            pl.BlockSpec((ML_CHUNK, ML_QK_WIDTH), lambda b, c: (blk(b, c), 0)),
            pl.BlockSpec((ML_WIDTH, ML_CHUNK), lambda b, c: (TR_MV, blk(b, c))),
            pl.BlockSpec((N_GATE_COLS, ML_CHUNK), lambda b, c: (0, blk(b, c))),
            pl.BlockSpec((ML_WIDTH, ML_CHUNK), lambda b, c: (TR_MO, blk(b, c))),
        ]

    chunk_args = [q_t, k_x, tr_x, gates_t, tr_x]
    nxt = lambda c: jnp.minimum(c + 1, nc - 1)
    in_specs = chunk_specs(fwd) + chunk_specs(bwd) + [
        pl.BlockSpec((N_GATE_COLS, ML_CHUNK), lambda b, c: (0, fwd(b, nxt(c)))),
        pl.BlockSpec((N_GATE_COLS, ML_CHUNK), lambda b, c: (0, bwd(b, nxt(c)))),
        pl.BlockSpec((N_GATE_COLS, 1), lambda b, c: (0, 0)),
        pl.BlockSpec((1, N_META, ML_QK_WIDTH), lambda b, c: (b, 0, 0)),
        pl.BlockSpec((ML_WIDTH, N_META), lambda b, c: (TR_MV, 0)),
        pl.BlockSpec((N_META, LANES), lambda b, c: (0, 0)),
        pl.BlockSpec((N_GATE_COLS, N_META), lambda b, c: (0, 0)),
        pl.BlockSpec((1, LANES), lambda b, c: (0, 0)),
        pl.BlockSpec((ML_WIDTH, ML_CHUNK), lambda b, c: (0, 0)),
    ]
    return pl.pallas_call(
        _mlstm_kernel,
        grid=(bsz, nc),
        in_specs=in_specs,
        out_specs=pl.BlockSpec((nc, ML_WIDTH, ML_CHUNK), lambda b, c: (b, 0, 0)),
        out_shape=jax.ShapeDtypeStruct((bsz * nc, ML_WIDTH, ML_CHUNK), BF16),
        scratch_shapes=[
            pltpu.VMEM((2, ML_HEADS, ML_V_DIM, ML_QK_DIM), F32),
            pltpu.VMEM((2, ML_HEADS, 1, ML_QK_DIM), F32),
            pltpu.VMEM((2, ML_HEADS, SUBLANES, LANES), F32),
            pltpu.VMEM((half + 1, ML_WIDTH, ML_CHUNK), F32),
            pltpu.VMEM((half + 1, ML_WIDTH, ML_CHUNK), F32),
            pltpu.VMEM((2, N_GATE_COLS, ML_CHUNK), F32),
            pltpu.VMEM((2, N_GATE_COLS, ML_CHUNK), F32),
            pltpu.VMEM((2, ML_CHUNK, LANES), F32),
        ],
        compiler_params=_cparams(("parallel", "arbitrary")),
        name="mlstm",
    )(*chunk_args, *chunk_args, gates_t, gates_t,
      bias_col, k_m, tr_m, gates_m, gates_m_t, bias_row, gain_b)


_FFN_TILES = D_FF // FFN_TF


def _mix_ffn_kernel(a_ref, mt_ref, x_ref, wo_ref, g2_ref, gf_ref, wg_hbm, wu_hbm, wd_hbm,
                    o_ref, u_ref, wg_buf, wu_buf, wd_buf, sem):
    def weight_copies(j, slot):
        cols = pl.ds(pl.multiple_of(j * FFN_TF, FFN_TF), FFN_TF)
        return (pltpu.make_async_copy(wg_hbm.at[:, cols], wg_buf.at[slot], sem.at[0, slot]),
                pltpu.make_async_copy(wu_hbm.at[:, cols], wu_buf.at[slot], sem.at[1, slot]),
                pltpu.make_async_copy(wd_hbm.at[cols, :], wd_buf.at[slot], sem.at[2, slot]))

    for cp in weight_copies(0, 0):
        cp.start()

    w_m = wo_ref[DA_WIDTH:DA_WIDTH + ML_WIDTH, :]
    ml = jnp.concatenate([_dot_tn(mt_ref[s], w_m) for s in range(mt_ref.shape[0])], axis=0)
    h = _dot(a_ref[...], wo_ref[0:DA_WIDTH, :]) + ml + x_ref[...]
    o_ref[...] = h
    u = h * lax.rsqrt(jnp.mean(h * h, axis=-1, keepdims=True) + EPS) * g2_ref[...]
    u_ref[...] = u.astype(BF16)

    def step(j, slot, prefetch):
        if prefetch:
            for cp in weight_copies(j + 1, 1 - slot):
                cp.start()
        for cp in weight_copies(j, slot):
            cp.wait()
        u = u_ref[...]
        half = FFN_TF // 2
        down = None
        for part in range(2):
            cols = slice(part * half, (part + 1) * half)
            g = _dot(u, wg_buf[slot, :, cols])
            ff = (g * _sigmoid(g) * _dot(u, wu_buf[slot, :, cols])).astype(BF16)
            term = _dot(ff, wd_buf[slot, cols, :])
            down = term if down is None else down + term
        o_ref[...] += down

    def pair(p, carry):
        step(2 * p, 0, True)
        step(2 * p + 1, 1, True)
        return carry

    lax.fori_loop(0, (_FFN_TILES - 1) // 2, pair, 0)
    step(_FFN_TILES - 1, 0, False)

    y = o_ref[...]
    o_ref[...] = y * lax.rsqrt(jnp.mean(y * y, axis=-1, keepdims=True) + EPS) * gf_ref[...]


def _mix_ffn(attn, ml_t, x_rows, w_out, norm_ffn_g, w_gate, w_up, w_down, norm_final_g):
    assert _FFN_TILES % 2 == 1
    m = x_rows.shape[0]
    return pl.pallas_call(
        _mix_ffn_kernel,
        grid=(m // FFN_TM,),
        in_specs=[
            pl.BlockSpec((FFN_TM, DA_WIDTH), lambda i: (i, 0)),
            pl.BlockSpec((FFN_TM // ML_CHUNK, ML_WIDTH, ML_CHUNK), lambda i: (i, 0, 0)),
            pl.BlockSpec((FFN_TM, D_MODEL), lambda i: (i, 0)),
            pl.BlockSpec((DA_WIDTH + ML_WIDTH, D_MODEL), lambda i: (0, 0)),
            pl.BlockSpec((1, D_MODEL), lambda i: (0, 0)),
            pl.BlockSpec((1, D_MODEL), lambda i: (0, 0)),
            pl.BlockSpec(memory_space=pl.ANY),
            pl.BlockSpec(memory_space=pl.ANY),
            pl.BlockSpec(memory_space=pl.ANY),
        ],
        out_specs=pl.BlockSpec((FFN_TM, D_MODEL), lambda i: (i, 0)),
        out_shape=jax.ShapeDtypeStruct((m, D_MODEL), F32),
        scratch_shapes=[
            pltpu.VMEM((FFN_TM, D_MODEL), BF16),
            pltpu.VMEM((2, D_MODEL, FFN_TF), BF16),
            pltpu.VMEM((2, D_MODEL, FFN_TF), BF16),
            pltpu.VMEM((2, FFN_TF, D_MODEL), BF16),
            pltpu.SemaphoreType.DMA((3, 2)),
        ],
        compiler_params=_cparams(("parallel",)),
        name="mix_ffn",
    )(attn, ml_t, x_rows, w_out, norm_ffn_g, norm_final_g, w_gate, w_up, w_down)


def _rope_tables(pos0, n):
    lane = jnp.arange(LANES) % DA_QK_DIM
    half = DA_ROT_DIM // 2
    inv = ROPE_THETA ** (-jnp.arange(0, DA_ROT_DIM, 2, dtype=F32) / DA_ROT_DIM)
    inv_lane = jnp.where(lane < DA_ROT_DIM, inv[lane % half], 0.0)
    pos = jnp.arange(pos0, pos0 + n, dtype=F32)
    ang = pos[:, None] * inv_lane[None, :]
    sin = jnp.sin(ang)
    sa = jnp.where(lane < half, -sin, 0.0)
    sb = jnp.where((lane >= half) & (lane < DA_ROT_DIM), sin, 0.0)
    return jnp.cos(ang), sa, sb


def _pad_lanes(a, width=LANES):
    return jnp.pad(a, ((0, 0), (0, width - a.shape[1])))


def kernel(x, meta_tokens, norm_mix, w_in, da_lambda_q1, da_lambda_k1, da_lambda_q2, da_lambda_k2,
           da_head_norm, ml_conv_w, ml_conv_b, ml_gate_bias, ml_head_norm, w_out, norm_ffn,
           w_gate, w_up, w_down, norm_final):
    bsz, seq, _ = x.shape
    x_rows = x.reshape(bsz * seq, D_MODEL)

    main_cols = N_NAT * IN_TN + N_TR * IN_TN
    w_main = w_in.astype(BF16)
    w_gates = _pad_lanes(w_in[0][:, main_cols:]).astype(BF16)
    norm_mix0 = norm_mix[0][None, :]

    tabs_x = _rope_tables(N_META, seq)
    tabs_m = _rope_tables(0, N_META)
    nat_x, tr_x, _, gates_t = _inproj(x_rows, norm_mix0, w_main, w_gates, *tabs_x, tm=IN_TM)
    nat_m, tr_m, gates_m, gates_m_t = _inproj(meta_tokens, norm_mix0, w_main, w_gates, *tabs_m,
                                              tm=N_META)

    conv_w = jnp.pad(ml_conv_w[0], ((0, SUBLANES - CONV_W), (0, 0)))
    q_t, k_x, k_m = _conv(nat_x, nat_m, conv_w, ml_conv_b[0][None, :], bsz, seq)

    lam_params = jnp.pad(
        jnp.stack([da_lambda_q1[0], da_lambda_k1[0], da_lambda_q2[0], da_lambda_k2[0]]),
        ((0, SUBLANES - 4), (0, LANES - DA_QK_DIM)))
    attn = _attention(lam_params, nat_x, nat_m, tr_x, tr_m,
                      da_head_norm[0].reshape(1, DA_WIDTH), bsz, seq)

    bias = ml_gate_bias[0].reshape(N_GATE_COLS)
    bias_row = _pad_lanes(bias[None, :])
    bias_col = bias[:, None]
    gain_b = jnp.broadcast_to(ml_head_norm[0].reshape(ML_WIDTH, 1), (ML_WIDTH, ML_CHUNK))
    ml_t = _mlstm(q_t, k_x, tr_x, gates_t, bias_col, k_m, tr_m, gates_m, gates_m_t, bias_row,
                  gain_b, bsz, seq)

    out = _mix_ffn(attn, ml_t, x_rows, w_out[0].astype(BF16), norm_ffn[0][None, :],
                   w_gate[0].astype(BF16), w_up[0].astype(BF16), w_down[0].astype(BF16),
                   norm_final[None, :])
    return out.reshape(bsz, seq, D_MODEL)
```
